```python
import jax
import jax.numpy as jnp
from jax import lax
import numpy as np

D_MODEL = 2048
BATCH = 1
SEQ = 8192
DEPTH = 2

MEM_LEN = 256
EPS = 1e-6
NEG_INF = -1e30

N_BRANCH = 3
BRANCH_WIDTH = D_MODEL // 2

MOBA_HEADS = 8
MOBA_HEAD_DIM = 128
MOBA_WIDTH = MOBA_HEADS * MOBA_HEAD_DIM
MOBA_BLOCK = 256
MOBA_TOPK = 3
MOBA_Q_CHUNK = 64
ROPE_THETA = 10000.0

GLA_HEADS = 4
GLA_DK = 128
GLA_DV = 256
GLA_K_WIDTH = GLA_HEADS * GLA_DK
GLA_V_WIDTH = GLA_HEADS * GLA_DV
GLA_RANK = 16
GLA_TAU = 16.0
GLA_CHUNK = 64

POOL_WINDOWS = (2, 4, 8, 16)
POOL_GROUPS = 4
POOL_GROUP_DIM = 256
POOL_WIDTH = POOL_GROUPS * POOL_GROUP_DIM

IN_SPLITS = (MOBA_WIDTH, MOBA_WIDTH, MOBA_WIDTH, GLA_K_WIDTH, GLA_K_WIDTH, GLA_V_WIDTH, GLA_V_WIDTH, GLA_RANK, POOL_WIDTH)
IN_WIDTH = 3 * MOBA_WIDTH + 2 * GLA_K_WIDTH + 2 * GLA_V_WIDTH + GLA_RANK + POOL_WIDTH

XATTN_HEADS = 4
XATTN_HEAD_DIM = 128
XATTN_WIDTH = XATTN_HEADS * XATTN_HEAD_DIM

MOE_GROUPS = 8
MOE_EXPERTS_PER_GROUP = 8
MOE_EXPERTS = MOE_GROUPS * MOE_EXPERTS_PER_GROUP
MOE_TOPK = 2
MOE_D_FF = 512
MOE_ROW_BLOCK = 128

kernel_name = "hybrid_moba_gla_pool_hmoe"


def rms_norm(x, g):
    xf = x.astype(jnp.float32)
    y = xf * lax.rsqrt(jnp.mean(xf * xf, axis=-1, keepdims=True) + EPS)
    return (y * g.astype(jnp.float32)).astype(x.dtype)


def rope_tables(positions):
    inv = 1.0 / (ROPE_THETA ** (jnp.arange(0, MOBA_HEAD_DIM, 2, dtype=jnp.float32) / MOBA_HEAD_DIM))
    ang = positions.astype(jnp.float32)[:, None, :, None] * inv
    return jnp.cos(ang), jnp.sin(ang)


def apply_rope(x, cos, sin):
    x1, x2 = jnp.split(x.astype(jnp.float32), 2, axis=-1)
    return jnp.concatenate([x1 * cos - x2 * sin, x2 * cos + x1 * sin], axis=-1).astype(x.dtype)


def moba_attention(q, k, v):
    B, H, S, hd = q.shape
    BLK, QC = MOBA_BLOCK, MOBA_Q_CHUNK
    nb = -(-S // BLK)
    s_pad = nb * BLK
    pad = ((0, 0), (0, 0), (0, s_pad - S), (0, 0))
    q, k, v = jnp.pad(q, pad), jnp.pad(k, pad), jnp.pad(v, pad)
    kb = k.reshape(B, H, nb, BLK, hd)
    vb = v.reshape(B, H, nb, BLK, hd)
    k_mean = jnp.mean(kb.astype(jnp.float32), axis=3)
    topk = min(MOBA_TOPK, nb)
    n_chunks = s_pad // QC
    q_chunks = jnp.moveaxis(q.reshape(B, H, n_chunks, QC, hd), 2, 0)
    scale = hd ** -0.5
    bi = jnp.arange(B)[:, None, None, None]
    hi = jnp.arange(H)[None, :, None, None]

    def one_chunk(args):
        q_c, c = args
        q_pos = c * QC + jnp.arange(QC)
        own = (c * QC) // BLK
        gate = jnp.einsum('bhqd,bhnd->bhqn', q_c.astype(jnp.float32), k_mean)
        gate = jnp.where(jnp.arange(nb) < own, gate, NEG_INF)
        _, idx = lax.top_k(gate, topk)
        k_sel = kb[bi, hi, idx]
        v_sel = vb[bi, hi, idx]
        s_sel = jnp.einsum('bhqd,bhqnkd->bhqnk', q_c, k_sel, preferred_element_type=jnp.float32) * scale
        s_sel = jnp.where((jnp.arange(topk) < own)[:, None], s_sel, NEG_INF)
        k_own = lax.dynamic_index_in_dim(kb, own, axis=2, keepdims=False)
        v_own = lax.dynamic_index_in_dim(vb, own, axis=2, keepdims=False)
        s_own = jnp.einsum('bhqd,bhkd->bhqk', q_c, k_own, preferred_element_type=jnp.float32) * scale
        s_own = jnp.where(own * BLK + jnp.arange(BLK)[None, :] <= q_pos[:, None], s_own, NEG_INF)
        scores = jnp.concatenate([s_sel.reshape(B, H, QC, topk * BLK), s_own], axis=-1)
        p = jax.nn.softmax(scores, axis=-1).astype(v.dtype)
        p_sel = p[..., :topk * BLK].reshape(B, H, QC, topk, BLK)
        p_own = p[..., topk * BLK:]
        out = (jnp.einsum('bhqnk,bhqnkd->bhqd', p_sel, v_sel, preferred_element_type=jnp.float32)
               + jnp.einsum('bhqk,bhkd->bhqd', p_own, v_own, preferred_element_type=jnp.float32))
        return out.astype(q.dtype)

    out = lax.map(one_chunk, (q_chunks, jnp.arange(n_chunks)))
    out = jnp.moveaxis(out, 0, 2).reshape(B, H, s_pad, hd)
    return out[:, :, :S]


def gla_attention(q, k, v, log_a):
    B, H, S, dk = q.shape
    dv = v.shape[-1]
    C = GLA_CHUNK
    nc = S // C

    def chunked(t):
        return t.astype(jnp.float32).reshape(B, H, nc, C, t.shape[-1])

    q = chunked(q) * (dk ** -0.5)
    k = chunked(k)
    v = chunked(v)
    b = jnp.cumsum(chunked(log_a), axis=3)
    b_last = b[:, :, :, -1:, :]
    q_dec = q * jnp.exp(b)
    k_inv = k * jnp.exp(-b)
    k_end = k * jnp.exp(b_last - b)
    causal = jnp.tril(jnp.ones((C, C), dtype=bool))
    attn = jnp.where(causal, jnp.einsum('bhncd,bhnjd->bhncj', q_dec, k_inv), 0.0)
    o_intra = jnp.einsum('bhncj,bhnje->bhnce', attn, v)

    def step(state, inp):
        q_c, k_c, v_c, decay_c = inp
        o_c = jnp.einsum('bhcd,bhde->bhce', q_c, state)
        state = decay_c[..., None] * state + jnp.einsum('bhcd,bhce->bhde', k_c, v_c)
        return state, o_c

    xs = (jnp.moveaxis(q_dec, 2, 0), jnp.moveaxis(k_end, 2, 0), jnp.moveaxis(v, 2, 0),
          jnp.moveaxis(jnp.exp(b_last[:, :, :, 0, :]), 2, 0))
    _, o_inter = lax.scan(step, jnp.zeros((B, H, dk, dv), jnp.float32), xs)
    o = o_intra + jnp.moveaxis(o_inter, 0, 2)
    return o.reshape(B, H, S, dv)


def pool_mixer(u, w_pool, pool_scale):
    B, S, _ = u.shape
    uf = u.astype(jnp.float32).reshape(B, S, POOL_GROUPS, POOL_GROUP_DIM)
    cs = jnp.concatenate([jnp.zeros_like(uf[:, :1]), jnp.cumsum(uf, axis=1)], axis=1)
    t = jnp.arange(S)[:, None]
    win = jnp.array(POOL_WINDOWS, dtype=jnp.int32)[None, :]
    start = jnp.maximum(t + 1 - win, 0)
    count = (t + 1 - start).astype(jnp.float32)
    window_sum = cs[:, 1:] - cs[:, start, jnp.arange(POOL_GROUPS)[None, :]]
    mixed = window_sum / count[None, :, :, None] - uf
    y = jnp.einsum('bsgc,gce->bsge', mixed, w_pool.astype(jnp.float32)).reshape(B, S, POOL_WIDTH)
    return (y * pool_scale.astype(jnp.float32)).astype(u.dtype)


def token_mixing(h, cos, sin, w_in, w_gla_decay, b_gla_decay, gla_norm, w_pool, pool_scale,
                 w_branch, w_gate, b_gate, w_mix_out):
    B, S, D = h.shape
    proj = h @ w_in
    pieces = []
    start = 0
    for width in IN_SPLITS:
        pieces.append(proj[..., start:start + width])
        start += width
    mq, mk, mv, gq, gk, gv, gr, g_low, pu = pieces

    def heads(t, n):
        return t.reshape(B, S, n, -1).transpose(0, 2, 1, 3)

    mq = apply_rope(heads(mq, MOBA_HEADS), cos, sin)
    mk = apply_rope(heads(mk, MOBA_HEADS), cos, sin)
    o_moba = moba_attention(mq, mk, heads(mv, MOBA_HEADS))
    o_moba = o_moba.transpose(0, 2, 1, 3).reshape(B, S, MOBA_WIDTH).astype(h.dtype)
    log_a = jax.nn.log_sigmoid((g_low @ w_gla_decay + b_gla_decay).astype(jnp.float32)) / GLA_TAU
    o_gla = gla_attention(heads(gq, GLA_HEADS), heads(gk, GLA_HEADS), heads(gv, GLA_HEADS), heads(log_a, GLA_HEADS))
    o_gla = rms_norm(o_gla.transpose(0, 2, 1, 3), gla_norm).reshape(B, S, GLA_V_WIDTH)
    o_gla = (o_gla * jax.nn.silu(gr.astype(jnp.float32))).astype(h.dtype)
    o_pool = pool_mixer(pu, w_pool, pool_scale)
    branches = jnp.stack([o_moba, o_gla, o_pool], axis=2)
    y = jnp.einsum('bsnc,ncd->bsnd', branches, w_branch)
    gates = jax.nn.sigmoid(h @ w_gate + b_gate).reshape(B, S, N_BRANCH, D)
    merged = jnp.sum(gates * y, axis=2)
    return merged @ w_mix_out


def cross_attention(h, mem_n, w_q, w_kv, w_o):
    B, S, _ = h.shape
    M = mem_n.shape[1]
    q = (h @ w_q).reshape(B, S, XATTN_HEADS, XATTN_HEAD_DIM)
    kv = (mem_n @ w_kv).reshape(B, M, 2, XATTN_HEADS, XATTN_HEAD_DIM)
    k, v = kv[:, :, 0], kv[:, :, 1]
    s = jnp.einsum('bshd,bmhd->bhsm', q, k, preferred_element_type=jnp.float32) * (XATTN_HEAD_DIM ** -0.5)
    p = jax.nn.softmax(s, axis=-1).astype(v.dtype)
    o = jnp.einsum('bhsm,bmhd->bshd', p, v)
    return o.reshape(B, S, XATTN_WIDTH) @ w_o


def hier_moe(h, w_route_group, b_route_group, w_route_expert, b_route_expert, w1, w3, w2):
    B, S, D = h.shape
    N = B * S
    E, RB = MOE_EXPERTS, MOE_ROW_BLOCK
    t = h.reshape(N, D)
    g_prob = jax.nn.softmax((t @ w_route_group).astype(jnp.float32) + b_route_group, axis=-1)
    g_w, g_idx = lax.top_k(g_prob, 1)
    e_logits = ((t @ w_route_expert).astype(jnp.float32) + b_route_expert).reshape(N, MOE_GROUPS, MOE_EXPERTS_PER_GROUP)
    e_logits = jnp.take_along_axis(e_logits, g_idx[:, :, None], axis=1)[:, 0]
    e_w, e_local = lax.top_k(jax.nn.softmax(e_logits, axis=-1), MOE_TOPK)
    weights = g_w * e_w / jnp.sum(e_w, axis=-1, keepdims=True)
    expert = g_idx * MOE_EXPERTS_PER_GROUP + e_local
    A = N * MOE_TOPK
    e_flat = expert.reshape(A)
    tok_flat = jnp.repeat(jnp.arange(N, dtype=jnp.int32), MOE_TOPK)
    w_flat = weights.reshape(A)
    order = jnp.argsort(e_flat)
    e_sorted = e_flat[order]
    counts = jnp.zeros((E,), jnp.int32).at[e_flat].add(1)
    starts = jnp.cumsum(counts) - counts
    padded = (counts + RB - 1) // RB * RB
    pad_ends = jnp.cumsum(padded)
    pad_starts = pad_ends - padded
    dest = pad_starts[e_sorted] + (jnp.arange(A) - starts[e_sorted])
    n_blocks = (A + E * (RB - 1) + RB - 1) // RB
    P = n_blocks * RB
    row_tok = jnp.zeros((P,), jnp.int32).at[dest].set(tok_flat[order])
    row_w = jnp.zeros((P,), jnp.float32).at[dest].set(w_flat[order])
    block_expert = jnp.minimum(jnp.searchsorted(pad_ends, jnp.arange(n_blocks) * RB, side='right'), E - 1)
    x_rows = t[row_tok].reshape(n_blocks, RB, D)

    def expert_block(args):
        xb, e = args
        return (jax.nn.silu(xb @ w1[e]) * (xb @ w3[e])) @ w2[e]

    y_rows = lax.map(expert_block, (x_rows, block_expert)).reshape(P, D)
    y = jnp.zeros((N, D), jnp.float32).at[row_tok].add(y_rows.astype(jnp.float32) * row_w[:, None])
    return y.astype(h.dtype).reshape(B, S, D)


def setup_inputs(seed: int = 0) -> dict:
    key = jax.random.key(seed)
    ks = iter(jax.random.split(key, 32))
    f32 = jnp.float32
    L, D = DEPTH, D_MODEL

    def w(shape, fan_in):
        return jax.random.normal(next(ks), shape, f32) * (fan_in ** -0.5)

    def gain(shape):
        return 1.0 + 0.05 * jax.random.normal(next(ks), shape, f32)

    def bias(shape, s=0.02):
        return s * jax.random.normal(next(ks), shape, f32)

    return {
        "x": jax.random.normal(next(ks), (BATCH, SEQ, D), f32),
        "mem": jax.random.normal(next(ks), (BATCH, MEM_LEN, D), f32),
        "positions": jnp.broadcast_to(jnp.arange(SEQ, dtype=jnp.int32)[None, :], (BATCH, SEQ)),
        "norm_mix": gain((L, D)),
        "w_in": w((L, D, IN_WIDTH), D),
        "w_gla_decay": w((L, GLA_RANK, GLA_K_WIDTH), GLA_RANK),
        "b_gla_decay": bias((L, GLA_K_WIDTH), 0.5),
        "gla_norm": gain((L, GLA_DV)),
        "w_pool": w((L, POOL_GROUPS, POOL_GROUP_DIM, POOL_GROUP_DIM), POOL_GROUP_DIM),
        "pool_scale": gain((L, POOL_WIDTH)),
        "w_branch": w((L, N_BRANCH, BRANCH_WIDTH, D), BRANCH_WIDTH),
        "w_gate": w((L, D, N_BRANCH * D), D),
        "b_gate": bias((L, N_BRANCH * D)),
        "w_mix_out": w((L, D, D), D),
        "norm_xattn": gain((L, D)),
        "norm_mem": gain((L, D)),
        "w_xq": w((L, D, XATTN_WIDTH), D),
        "w_xkv": w((L, D, 2 * XATTN_WIDTH), D),
        "w_xo": w((L, XATTN_WIDTH, D), XATTN_WIDTH),
        "norm_moe": gain((L, D)),
        "w_route_group": w((L, D, MOE_GROUPS), D),
        "b_route_group": bias((L, MOE_GROUPS), 0.01),
        "w_route_expert": w((L, D, MOE_EXPERTS), D),
        "b_route_expert": bias((L, MOE_EXPERTS), 0.01),
        "w_exp_gate": w((L, MOE_EXPERTS, D, MOE_D_FF), D),
        "w_exp_up": w((L, MOE_EXPERTS, D, MOE_D_FF), D),
        "w_exp_down": w((L, MOE_EXPERTS, MOE_D_FF, D), MOE_D_FF),
        "norm_final": gain((D,)),
    }


def reference(x, mem, positions, norm_mix, w_in, w_gla_decay, b_gla_decay, gla_norm, w_pool, pool_scale,
              w_branch, w_gate, b_gate, w_mix_out, norm_xattn, norm_mem, w_xq, w_xkv, w_xo, norm_moe,
              w_route_group, b_route_group, w_route_expert, b_route_expert, w_exp_gate, w_exp_up, w_exp_down,
              norm_final):
    cos, sin = rope_tables(positions)
    for l in range(DEPTH):
        h = rms_norm(x, norm_mix[l])
        x = x + token_mixing(h, cos, sin, w_in[l], w_gla_decay[l], b_gla_decay[l], gla_norm[l], w_pool[l],
                             pool_scale[l], w_branch[l], w_gate[l], b_gate[l], w_mix_out[l])
        x = x + cross_attention(rms_norm(x, norm_xattn[l]), rms_norm(mem, norm_mem[l]), w_xq[l], w_xkv[l], w_xo[l])
        x = x + hier_moe(rms_norm(x, norm_moe[l]), w_route_group[l], b_route_group[l], w_route_expert[l],
                         b_route_expert[l], w_exp_gate[l], w_exp_up[l], w_exp_down[l])
    return rms_norm(x, norm_final)
```

```python
import functools

import jax
import jax.numpy as jnp
from jax import lax
from jax.experimental import pallas as pl
from jax.experimental.pallas import tpu as pltpu

F32 = jnp.float32
BF16 = jnp.bfloat16

EPS = 1e-6
NEG_INF = -1e30

MOBA_HEADS = 8
MOBA_HEAD_DIM = 128
MOBA_WIDTH = MOBA_HEADS * MOBA_HEAD_DIM
MOBA_BLOCK = 256
MOBA_TOPK = 3
ROPE_THETA = 10000.0

GLA_HEADS = 4
GLA_DK = 128
GLA_DV = 256
GLA_K_WIDTH = GLA_HEADS * GLA_DK
GLA_V_WIDTH = GLA_HEADS * GLA_DV
GLA_RANK = 16
GLA_TAU = 16.0
GLA_CHUNK = 64

POOL_WINDOWS = (2, 4, 8, 16)
POOL_GROUP_DIM = 256
POOL_WIDTH = len(POOL_WINDOWS) * POOL_GROUP_DIM
POOL_HALO = 16

XATTN_HEADS = 4
XATTN_HEAD_DIM = 128
XATTN_WIDTH = XATTN_HEADS * XATTN_HEAD_DIM

MOE_GROUPS = 8
MOE_EXPERTS_PER_GROUP = 8
MOE_EXPERTS = MOE_GROUPS * MOE_EXPERTS_PER_GROUP
MOE_TOPK = 2
MOE_ROW_BLOCK = 128

LANES = 128

COL_MQ = 0
COL_MK = COL_MQ + MOBA_WIDTH
COL_MV = COL_MK + MOBA_WIDTH
COL_GQ = COL_MV + MOBA_WIDTH
COL_GK = COL_GQ + GLA_K_WIDTH
COL_GV = COL_GK + GLA_K_WIDTH
COL_GR = COL_GV + GLA_V_WIDTH
COL_PU = COL_GR + GLA_V_WIDTH

VMEM_LIMIT = 56 * 1024 * 1024


def _params(*semantics):
    return pltpu.CompilerParams(dimension_semantics=semantics, vmem_limit_bytes=VMEM_LIMIT)


def _rms(x, g):
    return x * lax.rsqrt(jnp.mean(x * x, axis=-1, keepdims=True) + EPS) * g


def _dot(a, b):
    return jnp.dot(a, b, preferred_element_type=F32)


def _dot_nt(a, b):
    return lax.dot_general(a, b, (((1,), (1,)), ((), ())), preferred_element_type=F32)


def _dot_tn(a, b):
    return lax.dot_general(a, b, (((0,), (0,)), ((), ())), preferred_element_type=F32)


def _split3(x):
    hi = x.astype(BF16)
    r1 = x - hi.astype(F32)
    mid = r1.astype(BF16)
    lo = (r1 - mid.astype(F32)).astype(BF16)
    return hi, mid, lo


def _inproj_kernel(x_ref, g_ref, w_ref, cos_ref, sin_ref, o_ref, h_ref, *, tn):
    j = pl.program_id(1)

    @pl.when(j == 0)
    def _():
        h_ref[...] = _rms(x_ref[...], g_ref[...]).astype(BF16)

    acc = _dot(h_ref[...], w_ref[...])
    n_rope = (2 * MOBA_WIDTH) // tn
    n_q = MOBA_WIDTH // tn

    @pl.when(j < n_rope)
    def _():
        scale = jnp.where(j < n_q, MOBA_HEAD_DIM ** -0.5, 1.0).astype(F32)
        cos = cos_ref[...] * scale
        sin = sin_ref[...] * scale
        for hh in range(tn // MOBA_HEAD_DIM):
            cols = slice(hh * MOBA_HEAD_DIM, (hh + 1) * MOBA_HEAD_DIM)
            a = acc[:, cols]
            rot = pltpu.roll(a, MOBA_HEAD_DIM // 2, axis=1)
            o_ref[:, cols] = (a * cos + rot * sin).astype(o_ref.dtype)

    @pl.when(j >= n_rope)
    def _():
        o_ref[...] = acc.astype(o_ref.dtype)


def _inproj(x, g, w, cos_f, sin_f, *, tm=1024, tn=512):
    n, d = x.shape
    width = w.shape[1]
    tm = min(tm, n)
    return pl.pallas_call(
        functools.partial(_inproj_kernel, tn=tn),
        out_shape=(jax.ShapeDtypeStruct((n, width), BF16), jax.ShapeDtypeStruct((n, d), BF16)),
        grid=(n // tm, width // tn),
        in_specs=[
            pl.BlockSpec((tm, d), lambda i, j: (i, 0)),
            pl.BlockSpec((1, d), lambda i, j: (0, 0)),
            pl.BlockSpec((d, tn), lambda i, j: (0, j)),
            pl.BlockSpec((tm, MOBA_HEAD_DIM), lambda i, j: (i, 0)),
            pl.BlockSpec((tm, MOBA_HEAD_DIM), lambda i, j: (i, 0)),
        ],
        out_specs=(
            pl.BlockSpec((tm, tn), lambda i, j: (i, j)),
            pl.BlockSpec((tm, d), lambda i, j: (i, 0)),
        ),
        compiler_params=_params("parallel", "arbitrary"),
        name="inproj",
    )(x, g, w, cos_f, sin_f)


def _moba_kernel(q_ref, k_ref, v_ref, o_ref, kmean_ref, *, n_blocks):
    qi = pl.program_id(1)
    blk = MOBA_BLOCK

    @pl.when(qi == 0)
    def _():
        kmean_ref[...] = jnp.zeros_like(kmean_ref)

        def fill(b, carry):
            kb = k_ref[pl.ds(pl.multiple_of(b * blk, blk), blk), :].astype(F32)
            kmean_ref[pl.ds(b, 1), :] = jnp.mean(kb, axis=0, keepdims=True)
            return carry

        lax.fori_loop(0, n_blocks, fill, 0)

    q = q_ref[...]
    gate = _dot_nt(q, kmean_ref[...].astype(BF16))
    lane = lax.broadcasted_iota(jnp.int32, gate.shape, 1)
    lane_f = lane.astype(F32)
    gate = jnp.where(lane < qi, gate, NEG_INF)
    sel = jnp.zeros(gate.shape, F32)
    for _ in range(MOBA_TOPK):
        top = jnp.max(gate, axis=1, keepdims=True)
        idx = jnp.min(jnp.where(gate == top, lane_f, float(LANES)), axis=1, keepdims=True)
        hit = lane_f == idx
        sel = jnp.where(hit & (top > 0.5 * NEG_INF), 1.0, sel)
        gate = jnp.where(hit, NEG_INF, gate)

    start = pl.multiple_of(qi * blk, blk)
    s = _dot_nt(q, k_ref[pl.ds(start, blk), :])
    row = lax.broadcasted_iota(jnp.int32, s.shape, 0)
    col = lax.broadcasted_iota(jnp.int32, s.shape, 1)
    s = jnp.where(col <= row, s, NEG_INF)
    m = jnp.max(s, axis=1, keepdims=True)
    p = jnp.exp(s - m)
    l = jnp.sum(p, axis=1, keepdims=True)
    acc = _dot(p.astype(BF16), v_ref[pl.ds(start, blk), :])

    def body(j, carry):
        m, l, acc = carry
        off = pl.multiple_of(j * blk, blk)
        s = _dot_nt(q, k_ref[pl.ds(off, blk), :])
        picked = jnp.sum(jnp.where(lane == j, sel, 0.0), axis=1, keepdims=True)
        s = jnp.where(picked > 0.0, s, NEG_INF)
        m_new = jnp.maximum(m, jnp.max(s, axis=1, keepdims=True))
        alpha = jnp.exp(m - m_new)
        p = jnp.exp(s - m_new)
        l = alpha * l + jnp.sum(p, axis=1, keepdims=True)
        acc = alpha * acc + _dot(p.astype(BF16), v_ref[pl.ds(off, blk), :])
        return m_new, l, acc

    m, l, acc = lax.fori_loop(0, qi, body, (m, l, acc))
    o_ref[...] = (acc / l).astype(o_ref.dtype)


def _moba(proj):
    n = proj.shape[0]
    n_blocks = n // MOBA_BLOCK
    hd = MOBA_HEAD_DIM
    return pl.pallas_call(
        functools.partial(_moba_kernel, n_blocks=n_blocks),
        out_shape=jax.ShapeDtypeStruct((n, MOBA_WIDTH), BF16),
        grid=(MOBA_HEADS, n_blocks),
        in_specs=[
            pl.BlockSpec((MOBA_BLOCK, hd), lambda h, i: (i, COL_MQ // hd + h)),
            pl.BlockSpec((n, hd), lambda h, i: (0, COL_MK // hd + h)),
            pl.BlockSpec((n, hd), lambda h, i: (0, COL_MV // hd + h)),
        ],
        out_specs=pl.BlockSpec((MOBA_BLOCK, hd), lambda h, i: (i, h)),
        scratch_shapes=[pltpu.VMEM((LANES, hd), F32)],
        compiler_params=_params("parallel", "arbitrary"),
        name="moba",
    )(proj, proj, proj)


def _gla_kernel(q_ref, k_ref, v_ref, r_ref, h_ref, wlow_ref, wdec_ref, bdec_ref, gn_ref, o_ref, state_ref, *, tb):
    i = pl.program_id(0)
    c = GLA_CHUNK
    n_chunks = tb // c
    chunk_shift = c.bit_length() - 1

    @pl.when(i == 0)
    def _():
        state_ref[...] = jnp.zeros_like(state_ref)

    g_low = _dot(h_ref[...], wlow_ref[...]).astype(BF16)
    z = _dot(g_low, wdec_ref[...]) + bdec_ref[...]
    log_a = jax.nn.log_sigmoid(z) / GLA_TAU

    row = lax.broadcasted_iota(jnp.int32, (tb, tb), 0)
    col = lax.broadcasted_iota(jnp.int32, (tb, tb), 1)
    same_chunk = (row >> chunk_shift) == (col >> chunk_shift)
    causal = same_chunk & (col <= row)
    tri = causal.astype(BF16)
    ones = same_chunk.astype(BF16)
    parts = _split3(log_a)
    b = sum(_dot(tri, part) for part in parts)
    b_last = sum(_dot(ones, part) for part in parts)

    q_dec = (q_ref[...].astype(F32) * (GLA_DK ** -0.5) * jnp.exp(b)).astype(BF16)
    kf = k_ref[...].astype(F32)
    k_inv = (kf * jnp.exp(-b)).astype(BF16)
    k_end = (kf * jnp.exp(b_last - b)).astype(BF16)
    decay = jnp.exp(b_last)

    for h in range(GLA_HEADS):
        kc = slice(h * GLA_DK, (h + 1) * GLA_DK)
        vc = slice(h * GLA_DV, (h + 1) * GLA_DV)
        v = v_ref[:, vc]
        attn = jnp.where(causal, _dot_nt(q_dec[:, kc], k_inv[:, kc]), 0.0)
        o = _dot(attn.astype(BF16), v)
        state = state_ref[h]
        inter = []
        for ci in range(n_chunks):
            rows = slice(ci * c, (ci + 1) * c)
            inter.append(_dot_nt(q_dec[rows, kc], state.astype(BF16)))
            update = _dot_tn(v[rows, :], k_end[rows, kc])
            state = decay[ci * c:ci * c + 1, kc] * state + update
        state_ref[h] = state
        o = o + jnp.concatenate(inter, axis=0)
        o = _rms(o, gn_ref[...])
        o_ref[:, vc] = (o * jax.nn.silu(r_ref[:, vc].astype(F32))).astype(o_ref.dtype)


def _gla(proj, h, w_low, w_dec, b_dec, gla_norm, *, tb=512):
    n, d = h.shape
    tb = min(tb, n)
    return pl.pallas_call(
        functools.partial(_gla_kernel, tb=tb),
        out_shape=jax.ShapeDtypeStruct((n, GLA_V_WIDTH), BF16),
        grid=(n // tb,),
        in_specs=[
            pl.BlockSpec((tb, GLA_K_WIDTH), lambda i: (i, COL_GQ // GLA_K_WIDTH)),
            pl.BlockSpec((tb, GLA_K_WIDTH), lambda i: (i, COL_GK // GLA_K_WIDTH)),
            pl.BlockSpec((tb, GLA_V_WIDTH), lambda i: (i, COL_GV // GLA_V_WIDTH)),
            pl.BlockSpec((tb, GLA_V_WIDTH), lambda i: (i, COL_GR // GLA_V_WIDTH)),
            pl.BlockSpec((tb, d), lambda i: (i, 0)),
            pl.BlockSpec((d, LANES), lambda i: (0, 0)),
            pl.BlockSpec((LANES, GLA_K_WIDTH), lambda i: (0, 0)),
            pl.BlockSpec((1, GLA_K_WIDTH), lambda i: (0, 0)),
            pl.BlockSpec((1, GLA_DV), lambda i: (0, 0)),
        ],
        out_specs=pl.BlockSpec((tb, GLA_V_WIDTH), lambda i: (i, 0)),
        scratch_shapes=[pltpu.VMEM((GLA_HEADS, GLA_DV, GLA_DK), F32)],
        compiler_params=_params("arbitrary"),
        name="gla",
    )(proj, proj, proj, proj, h, w_low, w_dec, b_dec, gla_norm)


def _pool_kernel(u_ref, halo_ref, w_ref, sc_ref, o_ref, ext_ref, *, tb):
    i = pl.program_id(0)
    ext_ref[pl.ds(POOL_HALO, tb), :] = u_ref[...].astype(F32)
    ext_ref[pl.ds(0, POOL_HALO), :] = jnp.where(i == 0, 0.0, halo_ref[...].astype(F32))
    t = i * tb + lax.broadcasted_iota(jnp.int32, (tb, 1), 0)
    for g, win in enumerate(POOL_WINDOWS):
        cols = slice(g * POOL_GROUP_DIM, (g + 1) * POOL_GROUP_DIM)
        u = ext_ref[pl.ds(POOL_HALO, tb), cols]
        window_sum = u
        for back in range(1, win):
            window_sum = window_sum + ext_ref[pl.ds(POOL_HALO - back, tb), cols]
        count = jnp.minimum(t + 1, win).astype(F32)
        mixed = window_sum / count - u
        y = _dot(mixed.astype(BF16), w_ref[g])
        o_ref[:, cols] = (y * sc_ref[:, cols]).astype(o_ref.dtype)


def _pool(proj, w_pool, pool_scale, *, tb=512):
    n = proj.shape[0]
    tb = min(tb, n)
    halo_per_block = tb // POOL_HALO
    return pl.pallas_call(
        functools.partial(_pool_kernel, tb=tb),
        out_shape=jax.ShapeDtypeStruct((n, POOL_WIDTH), BF16),
        grid=(n // tb,),
        in_specs=[
            pl.BlockSpec((tb, POOL_WIDTH), lambda i: (i, COL_PU // POOL_WIDTH)),
            pl.BlockSpec((POOL_HALO, POOL_WIDTH),
                         lambda i: (jnp.maximum(i * halo_per_block - 1, 0), COL_PU // POOL_WIDTH)),
            pl.BlockSpec(w_pool.shape, lambda i: (0, 0, 0)),
            pl.BlockSpec((1, POOL_WIDTH), lambda i: (0, 0)),
        ],
        out_specs=pl.BlockSpec((tb, POOL_WIDTH), lambda i: (i, 0)),
        scratch_shapes=[pltpu.VMEM((POOL_HALO + tb, POOL_WIDTH), F32)],
        compiler_params=_params("parallel"),
        name="pool",
    )(proj, proj, w_pool, pool_scale)


def _merge_kernel(h_ref, a_ref, b_ref, c_ref, wg0, wg1, wg2, bg0, bg1, bg2, wb0, wb1, wb2, o_ref):
    h = h_ref[...]
    total = None
    for br_ref, wg, bg, wb in ((a_ref, wg0, bg0, wb0), (b_ref, wg1, bg1, wb1), (c_ref, wg2, bg2, wb2)):
        gate = jax.nn.sigmoid(_dot(h, wg[...]) + bg[...])
        term = gate * _dot(br_ref[...], wb[...])
        total = term if total is None else total + term
    o_ref[...] = total.astype(o_ref.dtype)


def _merge(h, o_moba, o_gla, o_pool, w_gate, b_gate, w_branch, *, tm=1024, tn=256):
    n, d = h.shape
    tm = min(tm, n)
    bw = o_moba.shape[1]
    tiles = d // tn
    row_spec = lambda width: pl.BlockSpec((tm, width), lambda i, j: (i, 0))
    gate_specs = [pl.BlockSpec((d, tn), lambda i, j, k=k: (0, k * tiles + j)) for k in range(3)]
    bias_specs = [pl.BlockSpec((1, tn), lambda i, j, k=k: (0, k * tiles + j)) for k in range(3)]
    branch_specs = [pl.BlockSpec((None, bw, tn), lambda i, j, k=k: (k, 0, j)) for k in range(3)]
    return pl.pallas_call(
        _merge_kernel,
        out_shape=jax.ShapeDtypeStruct((n, d), BF16),
        grid=(n // tm, tiles),
        in_specs=[row_spec(d), row_spec(bw), row_spec(bw), row_spec(bw)] + gate_specs + bias_specs + branch_specs,
        out_specs=pl.BlockSpec((tm, tn), lambda i, j: (i, j)),
        compiler_params=_params("parallel", "arbitrary"),
        name="merge",
    )(h, o_moba, o_gla, o_pool, w_gate, w_gate, w_gate, b_gate, b_gate, b_gate, w_branch, w_branch, w_branch)


def _matmul_residual_kernel(a_ref, w_ref, x_ref, o_ref):
    o_ref[...] = x_ref[...] + _dot(a_ref[...], w_ref[...])


def _matmul_residual(a, w, x, *, tm=1024, tn=512):
    n, k = a.shape
    d = w.shape[1]
    tm = min(tm, n)
    return pl.pallas_call(
        _matmul_residual_kernel,
        out_shape=jax.ShapeDtypeStruct((n, d), F32),
        grid=(n // tm, d // tn),
        in_specs=[
            pl.BlockSpec((tm, k), lambda i, j: (i, 0)),
            pl.BlockSpec((k, tn), lambda i, j: (0, j)),
            pl.BlockSpec((tm, tn), lambda i, j: (i, j)),
        ],
        out_specs=pl.BlockSpec((tm, tn), lambda i, j: (i, j)),
        compiler_params=_params("parallel", "arbitrary"),
        name="mix_out",
    )(a, w, x)


def _norm_matmul_kernel(x_ref, g_ref, w_ref, o_ref):
    o_ref[...] = _dot(_rms(x_ref[...], g_ref[...]).astype(BF16), w_ref[...]).astype(o_ref.dtype)


def _norm_matmul(x, g, w):
    n = x.shape[0]
    return pl.pallas_call(
        _norm_matmul_kernel,
        out_shape=jax.ShapeDtypeStruct((n, w.shape[1]), BF16),
        name="mem_kv",
    )(x, g, w)


def _xattn_kernel(x_ref, g_ref, wq_ref, kv_ref, wo_ref, gm_ref, wr_ref, br_ref, xo_ref, h2_ref, lg_ref):
    x = x_ref[...]
    h = _rms(x, g_ref[...]).astype(BF16)
    q = (_dot(h, wq_ref[...]) * (XATTN_HEAD_DIM ** -0.5)).astype(BF16)
    heads = []
    for hd in range(XATTN_HEADS):
        kc = slice(hd * XATTN_HEAD_DIM, (hd + 1) * XATTN_HEAD_DIM)
        vc = slice(XATTN_WIDTH + hd * XATTN_HEAD_DIM, XATTN_WIDTH + (hd + 1) * XATTN_HEAD_DIM)
        s = _dot_nt(q[:, kc], kv_ref[:, kc])
        p = jnp.exp(s - jnp.max(s, axis=1, keepdims=True))
        p = p / jnp.sum(p, axis=1, keepdims=True)
        heads.append(_dot(p.astype(BF16), kv_ref[:, vc]).astype(BF16))
    o = jnp.concatenate(heads, axis=1)
    xn = x + _dot(o, wo_ref[...])
    xo_ref[...] = xn
    h2 = _rms(xn, gm_ref[...])
    h2_ref[...] = h2.astype(BF16)
    h_hi, h_mid, h_lo = _split3(h2)
    w_hi, w_mid, w_lo = wr_ref[0], wr_ref[1], wr_ref[2]
    lg = (_dot(h_hi, w_hi) + (_dot(h_hi, w_mid) + _dot(h_mid, w_hi))
          + (_dot(h_hi, w_lo) + _dot(h_mid, w_mid) + _dot(h_lo, w_hi)))
    lg_ref[...] = lg + br_ref[...]


def _xattn(x, g, wq, kv, wo, g_moe, w_route3, b_route, *, tm=512):
    n, d = x.shape
    tm = min(tm, n)
    full = lambda a: pl.BlockSpec(a.shape, lambda i: (0,) * a.ndim)
    rows = lambda width: pl.BlockSpec((tm, width), lambda i: (i, 0))
    return pl.pallas_call(
        _xattn_kernel,
        out_shape=(jax.ShapeDtypeStruct((n, d), F32), jax.ShapeDtypeStruct((n, d), BF16),
                   jax.ShapeDtypeStruct((n, LANES), F32)),
        grid=(n // tm,),
        in_specs=[rows(d), full(g), full(wq), full(kv), full(wo), full(g_moe), full(w_route3), full(b_route)],
        out_specs=(rows(d), rows(d), rows(LANES)),
        compiler_params=_params("parallel"),
        name="xattn_route",
    )(x, g, wq, kv, wo, g_moe, w_route3, b_route)


def _moe_kernel(be_ref, nused_ref, x_ref, w1_ref, w3_ref, w2_ref, rw_ref, o_ref, w1b, w3b, w2b):
    b = pl.program_id(0)

    @pl.when(b < nused_ref[0])
    def _():
        prev = be_ref[jnp.maximum(b - 1, 0)]

        @pl.when((b == 0) | (be_ref[b] != prev))
        def _():
            w1b[...] = w1_ref[...].astype(BF16)
            w3b[...] = w3_ref[...].astype(BF16)
            w2b[...] = w2_ref[...].astype(BF16)

        x = x_ref[...]
        hidden = jax.nn.silu(_dot(x, w1b[...])) * _dot(x, w3b[...])
        o_ref[...] = _dot(hidden.astype(BF16), w2b[...]) * rw_ref[...]

    @pl.when(b >= nused_ref[0])
    def _():
        o_ref[...] = jnp.zeros_like(o_ref)


def _moe_experts(block_expert, n_used, x_rows, w1, w3, w2, row_w):
    p, d = x_rows.shape
    rb = MOE_ROW_BLOCK
    dff = w1.shape[2]
    grid_spec = pltpu.PrefetchScalarGridSpec(
        num_scalar_prefetch=2,
        grid=(p // rb,),
        in_specs=[
            pl.BlockSpec((rb, d), lambda b, be, nu: (b, 0)),
            pl.BlockSpec((None, d, dff), lambda b, be, nu: (be[b], 0, 0)),
            pl.BlockSpec((None, d, dff), lambda b, be, nu: (be[b], 0, 0)),
            pl.BlockSpec((None, dff, d), lambda b, be, nu: (be[b], 0, 0)),
            pl.BlockSpec((rb, 1), lambda b, be, nu: (b, 0)),
        ],
        out_specs=pl.BlockSpec((rb, d), lambda b, be, nu: (b, 0)),
        scratch_shapes=[pltpu.VMEM((d, dff), BF16), pltpu.VMEM((d, dff), BF16), pltpu.VMEM((dff, d), BF16)],
    )
    return pl.pallas_call(
        _moe_kernel,
        out_shape=jax.ShapeDtypeStruct((p, d), F32),
        grid_spec=grid_spec,
        compiler_params=_params("arbitrary"),
        name="moe_experts",
    )(block_expert, n_used, x_rows, w1, w3, w2, row_w)


def _route(logits):
    n = logits.shape[0]
    e_count, rb = MOE_EXPERTS, MOE_ROW_BLOCK
    g_prob = jax.nn.softmax(logits[:, :MOE_GROUPS], axis=-1)
    g_w, g_idx = lax.top_k(g_prob, 1)
    e_logits = logits[:, MOE_GROUPS:MOE_GROUPS + e_count].reshape(n, MOE_GROUPS, MOE_EXPERTS_PER_GROUP)
    e_logits = jnp.take_along_axis(e_logits, g_idx[:, :, None], axis=1)[:, 0]
    e_w, e_local = lax.top_k(jax.nn.softmax(e_logits, axis=-1), MOE_TOPK)
    weights = g_w * e_w / jnp.sum(e_w, axis=-1, keepdims=True)
    expert = g_idx * MOE_EXPERTS_PER_GROUP + e_local
    a = n * MOE_TOPK
    e_flat = expert.reshape(a).astype(jnp.int32)
    tok_flat = jnp.repeat(jnp.arange(n, dtype=jnp.int32), MOE_TOPK)
    w_flat = weights.reshape(a)
    order = jnp.argsort(e_flat)
    e_sorted = e_flat[order]
    counts = jnp.zeros((e_count,), jnp.int32).at[e_flat].add(1)
    starts = jnp.cumsum(counts) - counts
    padded = (counts + rb - 1) // rb * rb
    pad_ends = jnp.cumsum(padded)
    pad_starts = pad_ends - padded
    dest = pad_starts[e_sorted] + (jnp.arange(a, dtype=jnp.int32) - starts[e_sorted])
    n_blocks = (a + e_count * (rb - 1) + rb - 1) // rb
    p = n_blocks * rb
    row_tok = jnp.zeros((p,), jnp.int32).at[dest].set(tok_flat[order])
    row_w = jnp.zeros((p,), F32).at[dest].set(w_flat[order])
    block_expert = jnp.minimum(jnp.searchsorted(pad_ends, jnp.arange(n_blocks) * rb, side='right'),
                               e_count - 1).astype(jnp.int32)
    n_used = (pad_ends[-1:] // rb).astype(jnp.int32)
    slot = jnp.zeros((a,), jnp.int32).at[order].set(dest).reshape(n, MOE_TOPK)
    return row_tok, row_w, block_expert, n_used, slot


def _final_norm_kernel(x_ref, g_ref, o_ref):
    o_ref[...] = _rms(x_ref[...], g_ref[...])


def _final_norm(x, g, *, tm=1024):
    n, d = x.shape
    tm = min(tm, n)
    return pl.pallas_call(
        _final_norm_kernel,
        out_shape=jax.ShapeDtypeStruct((n, d), F32),
        grid=(n // tm,),
        in_specs=[pl.BlockSpec((tm, d), lambda i: (i, 0)), pl.BlockSpec((1, d), lambda i: (0, 0))],
        out_specs=pl.BlockSpec((tm, d), lambda i: (i, 0)),
        compiler_params=_params("parallel"),
        name="final_norm",
    )(x, g)


def _rope_tables(positions):
    inv = 1.0 / (ROPE_THETA ** (jnp.arange(0, MOBA_HEAD_DIM, 2, dtype=F32) / MOBA_HEAD_DIM))
    ang = positions.astype(F32)[:, None] * inv
    cos, sin = jnp.cos(ang), jnp.sin(ang)
    return jnp.concatenate([cos, cos], axis=-1), jnp.concatenate([-sin, sin], axis=-1)


def _reorder_w_in(w_in):
    moba_gla = 3 * MOBA_WIDTH + 2 * GLA_K_WIDTH + 2 * GLA_V_WIDTH
    g_low = w_in[:, moba_gla:moba_gla + GLA_RANK]
    pool = w_in[:, moba_gla + GLA_RANK:]
    g_low = jnp.pad(g_low, ((0, 0), (0, LANES - GLA_RANK)))
    return jnp.concatenate([w_in[:, :moba_gla], pool], axis=1).astype(BF16), g_low.astype(BF16)


def _layer(x, mem, cos_f, sin_f, p):
    row = lambda v: v.reshape(1, -1)
    w_main, w_low = _reorder_w_in(p["w_in"])
    proj, h = _inproj(x, row(p["norm_mix"]), w_main, cos_f, sin_f)
    o_moba = _moba(proj)
    w_dec = jnp.pad(p["w_gla_decay"], ((0, LANES - GLA_RANK), (0, 0))).astype(BF16)
    o_gla = _gla(proj, h, w_low, w_dec, row(p["b_gla_decay"]), row(p["gla_norm"]))
    o_pool = _pool(proj, p["w_pool"].astype(BF16), row(p["pool_scale"]))
    merged = _merge(h, o_moba, o_gla, o_pool, p["w_gate"].astype(BF16), row(p["b_gate"]),
                    p["w_branch"].astype(BF16))
    x = _matmul_residual(merged, p["w_mix_out"].astype(BF16), x)

    kv = _norm_matmul(mem, row(p["norm_mem"]), p["w_xkv"].astype(BF16))
    w_route = jnp.concatenate([p["w_route_group"], p["w_route_expert"]], axis=1)
    n_route = w_route.shape[1]
    w_route = jnp.pad(w_route, ((0, 0), (0, LANES - n_route)))
    w_route3 = jnp.stack(_split3(w_route))
    b_route = jnp.pad(jnp.concatenate([p["b_route_group"], p["b_route_expert"]]), (0, LANES - n_route))
    x, h2, logits = _xattn(x, row(p["norm_xattn"]), p["w_xq"].astype(BF16), kv, p["w_xo"].astype(BF16),
                           row(p["norm_moe"]), w_route3, row(b_route))

    row_tok, row_w, block_expert, n_used, slot = _route(logits)
    x_rows = jnp.take(h2, row_tok, axis=0)
    y_rows = _moe_experts(block_expert, n_used, x_rows, p["w_exp_gate"], p["w_exp_up"], p["w_exp_down"],
                          row_w.reshape(-1, 1))
    y = jnp.take(y_rows, slot[:, 0], axis=0) + jnp.take(y_rows, slot[:, 1], axis=0)
    return x + y


def kernel(x, mem, positions, norm_mix, w_in, w_gla_decay, b_gla_decay, gla_norm, w_pool, pool_scale, w_branch, w_gate, b_gate, w_mix_out, norm_xattn, norm_mem, w_xq, w_xkv, w_xo, norm_moe, w_route_group, b_route_group, w_route_expert, b_route_expert, w_exp_gate, w_exp_up, w_exp_down, norm_final):
    batch, seq, d = x.shape
    assert batch == 1, "kernels are written for a single sequence"
    layers = dict(norm_mix=norm_mix, w_in=w_in, w_gla_decay=w_gla_decay, b_gla_decay=b_gla_decay,
                  gla_norm=gla_norm, w_pool=w_pool, pool_scale=pool_scale, w_branch=w_branch, w_gate=w_gate,
                  b_gate=b_gate, w_mix_out=w_mix_out, norm_xattn=norm_xattn, norm_mem=norm_mem, w_xq=w_xq,
                  w_xkv=w_xkv, w_xo=w_xo, norm_moe=norm_moe, w_route_group=w_route_group,
                  b_route_group=b_route_group, w_route_expert=w_route_expert, b_route_expert=b_route_expert,
                  w_exp_gate=w_exp_gate, w_exp_up=w_exp_up, w_exp_down=w_exp_down)
    cos_f, sin_f = _rope_tables(positions[0])
    xs = x[0]
    mem2 = mem[0]
    for l in range(norm_mix.shape[0]):
        xs = _layer(xs, mem2, cos_f, sin_f, {k: v[l] for k, v in layers.items()})
    return _final_norm(xs, norm_final.reshape(1, -1)).reshape(batch, seq, d)
```

```python
import functools

import jax
import jax.numpy as jnp
from jax import lax
from jax.experimental import pallas as pl
from jax.experimental.pallas import tpu as pltpu

F32 = jnp.float32
BF16 = jnp.bfloat16

EPS = 1e-6
NEG_INF = -1e30

MOBA_HEADS = 8
MOBA_HEAD_DIM = 128
MOBA_WIDTH = MOBA_HEADS * MOBA_HEAD_DIM
MOBA_BLOCK = 256
MOBA_TOPK = 3
MOBA_KV_BLOCKS = 2
MOBA_HEADS_PER_STEP = 4
ROPE_THETA = 10000.0

GLA_HEADS = 4
GLA_DK = 128
GLA_DV = 256
GLA_K_WIDTH = GLA_HEADS * GLA_DK
GLA_V_WIDTH = GLA_HEADS * GLA_DV
GLA_RANK = 16
GLA_TAU = 16.0
GLA_CHUNK = 64

POOL_WINDOWS = (2, 4, 8, 16)
POOL_GROUP_DIM = 256
POOL_WIDTH = len(POOL_WINDOWS) * POOL_GROUP_DIM
POOL_HALO = 16

XATTN_HEADS = 4
XATTN_HEAD_DIM = 128
XATTN_WIDTH = XATTN_HEADS * XATTN_HEAD_DIM

MOE_GROUPS = 8
MOE_EXPERTS_PER_GROUP = 8
MOE_EXPERTS = MOE_GROUPS * MOE_EXPERTS_PER_GROUP
MOE_TOPK = 2
MOE_ROW_BLOCK = 128

LANES = 128

COL_MQ = 0
COL_MK = COL_MQ + MOBA_WIDTH
COL_MV = COL_MK + MOBA_WIDTH
COL_GQ = COL_MV + MOBA_WIDTH
COL_GK = COL_GQ + GLA_K_WIDTH
COL_GV = COL_GK + GLA_K_WIDTH
COL_GR = COL_GV + GLA_V_WIDTH
COL_PU = COL_GR + GLA_V_WIDTH
PROJ_WIDTH = COL_PU + POOL_WIDTH
W_IN_LOW = COL_PU
W_IN_POOL = COL_PU + GLA_RANK

ROUTE_E1, ROUTE_E2, ROUTE_W1, ROUTE_W2, ROUTE_RANK1, ROUTE_RANK2 = range(6)

VMEM_LIMIT = 56 * 1024 * 1024


def _params(*semantics):
    return pltpu.CompilerParams(dimension_semantics=semantics, vmem_limit_bytes=VMEM_LIMIT)


def _rms(x, g):
    return x * lax.rsqrt(jnp.mean(x * x, axis=-1, keepdims=True) + EPS) * g


def _dot(a, b):
    return jnp.dot(a, b, preferred_element_type=F32)


def _dot_nt(a, b):
    return lax.dot_general(a, b, (((1,), (1,)), ((), ())), preferred_element_type=F32)


def _dot_tn(a, b):
    return lax.dot_general(a, b, (((0,), (0,)), ((), ())), preferred_element_type=F32)


def _split3(x):
    hi = x.astype(BF16)
    r1 = x - hi.astype(F32)
    mid = r1.astype(BF16)
    lo = (r1 - mid.astype(F32)).astype(BF16)
    return hi, mid, lo


def _layer_vec(v, l):
    return v[l].reshape(1, -1)


def _inproj_kernel(x_ref, g_ref, w_ref, wp_ref, cos_ref, sin_ref, o_ref, h_ref, *, tn, n_main):
    j = pl.program_id(1)

    @pl.when(j == 0)
    def _():
        h_ref[...] = _rms(x_ref[...], g_ref[...]).astype(BF16)

    n_rope = (2 * MOBA_WIDTH) // tn
    n_q = MOBA_WIDTH // tn

    @pl.when(j < n_rope)
    def _():
        acc = _dot(h_ref[...], w_ref[...].astype(BF16))
        scale = jnp.where(j < n_q, MOBA_HEAD_DIM ** -0.5, 1.0).astype(F32)
        cos = cos_ref[...] * scale
        sin = sin_ref[...] * scale
        for hh in range(tn // MOBA_HEAD_DIM):
            cols = slice(hh * MOBA_HEAD_DIM, (hh + 1) * MOBA_HEAD_DIM)
            a = acc[:, cols]
            rot = pltpu.roll(a, MOBA_HEAD_DIM // 2, axis=1)
            o_ref[:, cols] = (a * cos + rot * sin).astype(o_ref.dtype)

    @pl.when((j >= n_rope) & (j < n_main))
    def _():
        o_ref[...] = _dot(h_ref[...], w_ref[...].astype(BF16)).astype(o_ref.dtype)

    @pl.when(j >= n_main)
    def _():
        o_ref[...] = _dot(h_ref[...], wp_ref[...].astype(BF16)).astype(o_ref.dtype)


def _inproj(x, g, w_in, w_pool_in, l, cos_f, sin_f, *, tm=1024, tn=512):
    n, d = x.shape
    tm = min(tm, n)
    n_main = COL_PU // tn
    n_tiles = PROJ_WIDTH // tn
    return pl.pallas_call(
        functools.partial(_inproj_kernel, tn=tn, n_main=n_main),
        out_shape=(jax.ShapeDtypeStruct((n, PROJ_WIDTH), BF16), jax.ShapeDtypeStruct((n, d), BF16)),
        grid=(n // tm, n_tiles),
        in_specs=[
            pl.BlockSpec((tm, d), lambda i, j: (i, 0)),
            pl.BlockSpec((1, d), lambda i, j: (0, 0)),
            pl.BlockSpec((None, d, tn), lambda i, j: (l, 0, jnp.minimum(j, n_main - 1))),
            pl.BlockSpec((d, tn), lambda i, j: (0, jnp.maximum(j - n_main, 0))),
            pl.BlockSpec((tm, MOBA_HEAD_DIM), lambda i, j: (i, 0)),
            pl.BlockSpec((tm, MOBA_HEAD_DIM), lambda i, j: (i, 0)),
        ],
        out_specs=(
            pl.BlockSpec((tm, tn), lambda i, j: (i, j)),
            pl.BlockSpec((tm, d), lambda i, j: (i, 0)),
        ),
        compiler_params=_params("parallel", "arbitrary"),
        name="inproj",
    )(x, g, w_in, w_pool_in, cos_f, sin_f)


def _moba_kernel(q_ref, k_ref, v_ref, o_ref, kmean_ref, *, n_blocks):
    qi = pl.program_id(1)
    blk = MOBA_BLOCK
    hd = MOBA_HEAD_DIM
    tile = MOBA_KV_BLOCKS * blk
    blk_shift = blk.bit_length() - 1
    head_cols = [slice(hh * hd, (hh + 1) * hd) for hh in range(MOBA_HEADS_PER_STEP)]

    @pl.when(qi == 0)
    def _():
        kmean_ref[...] = jnp.zeros_like(kmean_ref)

        def fill(b, carry):
            kb = k_ref[pl.ds(pl.multiple_of(b * blk, blk), blk), :].astype(F32)
            kmean_ref[pl.ds(b, 1), :] = jnp.mean(kb, axis=0, keepdims=True)
            return carry

        lax.fori_loop(0, n_blocks, fill, 0)

    start = pl.multiple_of(qi * blk, blk)
    lane = lax.broadcasted_iota(jnp.int32, (blk, LANES), 1)
    lane_f = lane.astype(F32)
    row = lax.broadcasted_iota(jnp.int32, (blk, blk), 0)
    col = lax.broadcasted_iota(jnp.int32, (blk, blk), 1)
    ones_own = jnp.ones((blk, hd), BF16)
    ones_tile = jnp.ones((tile, hd), BF16)

    q_augs, carry = [], []
    for cols in head_cols:
        q = q_ref[:, cols]
        kmean_hi, kmean_mid, _ = _split3(kmean_ref[:, cols])
        gate = _dot_nt(q, kmean_hi) + _dot_nt(q, kmean_mid)
        gate = jnp.where(lane < qi, gate, NEG_INF)
        sel = jnp.zeros(gate.shape, jnp.bool_)
        for _ in range(MOBA_TOPK):
            top = jnp.max(gate, axis=1, keepdims=True)
            idx = jnp.min(jnp.where(gate == top, lane_f, float(LANES)), axis=1, keepdims=True)
            hit = lane_f == idx
            sel = sel | (hit & (top > 0.5 * NEG_INF))
            gate = jnp.where(hit, NEG_INF, gate)
        q_augs.append(jnp.concatenate([q, jnp.where(sel, 0.0, NEG_INF).astype(BF16)], axis=1))

        s = _dot_nt(q, k_ref[pl.ds(start, blk), cols])
        s = jnp.where(col <= row, s, NEG_INF)
        m = jnp.max(s, axis=1, keepdims=True)
        p = jnp.exp(s - m).astype(BF16)
        acc = _dot(p, jnp.concatenate([v_ref[pl.ds(start, blk), cols], ones_own], axis=1))
        carry += [m, acc]

    tile_row = lax.broadcasted_iota(jnp.int32, (tile, hd), 0)
    tile_lane = lax.broadcasted_iota(jnp.int32, (tile, hd), 1)

    def body(t, carry):
        off = pl.multiple_of(t * tile, tile)
        block_hot = (tile_lane == t * MOBA_KV_BLOCKS + (tile_row >> blk_shift)).astype(BF16)
        out = []
        for hh, cols in enumerate(head_cols):
            m, acc = carry[2 * hh], carry[2 * hh + 1]
            k_aug = jnp.concatenate([k_ref[pl.ds(off, tile), cols], block_hot], axis=1)
            s = _dot_nt(q_augs[hh], k_aug)
            m_new = jnp.maximum(m, jnp.max(s, axis=1, keepdims=True))
            alpha = jnp.exp(m - m_new)
            p = jnp.exp(s - m_new).astype(BF16)
            v_aug = jnp.concatenate([v_ref[pl.ds(off, tile), cols], ones_tile], axis=1)
            out += [m_new, alpha * acc + _dot(p, v_aug)]
        return tuple(out)

    n_tiles = (qi + MOBA_KV_BLOCKS - 1) // MOBA_KV_BLOCKS
    carry = lax.fori_loop(0, n_tiles, body, tuple(carry))
    for hh, cols in enumerate(head_cols):
        acc = carry[2 * hh + 1]
        o_ref[:, cols] = (acc[:, :hd] / acc[:, hd:]).astype(o_ref.dtype)


def _moba(proj):
    n = proj.shape[0]
    n_blocks = n // MOBA_BLOCK
    assert n_blocks % MOBA_KV_BLOCKS == 0 and n_blocks <= LANES
    width = MOBA_HEADS_PER_STEP * MOBA_HEAD_DIM
    return pl.pallas_call(
        functools.partial(_moba_kernel, n_blocks=n_blocks),
        out_shape=jax.ShapeDtypeStruct((n, MOBA_WIDTH), BF16),
        grid=(MOBA_HEADS // MOBA_HEADS_PER_STEP, n_blocks),
        in_specs=[
            pl.BlockSpec((MOBA_BLOCK, width), lambda h, i: (i, COL_MQ // width + h)),
            pl.BlockSpec((n, width), lambda h, i: (0, COL_MK // width + h), pipeline_mode=pl.Buffered(1)),
            pl.BlockSpec((n, width), lambda h, i: (0, COL_MV // width + h), pipeline_mode=pl.Buffered(1)),
        ],
        out_specs=pl.BlockSpec((MOBA_BLOCK, width), lambda h, i: (i, h)),
        scratch_shapes=[pltpu.VMEM((LANES, width), F32)],
        compiler_params=_params("parallel", "arbitrary"),
        name="moba",
    )(proj, proj, proj)


def _gla_kernel(q_ref, k_ref, v_ref, r_ref, h_ref, wlow_ref, wdec_ref, bdec_ref, gn_ref, o_ref, state_ref, *, tb):
    i = pl.program_id(0)
    c = GLA_CHUNK
    n_chunks = tb // c
    chunk_shift = c.bit_length() - 1

    @pl.when(i == 0)
    def _():
        state_ref[...] = jnp.zeros_like(state_ref)

    g_low = _dot(h_ref[...], wlow_ref[...]).astype(BF16)
    z = _dot(g_low, wdec_ref[...]) + bdec_ref[...]
    log_a = jax.nn.log_sigmoid(z) / GLA_TAU

    row = lax.broadcasted_iota(jnp.int32, (tb, tb), 0)
    col = lax.broadcasted_iota(jnp.int32, (tb, tb), 1)
    same_chunk = (row >> chunk_shift) == (col >> chunk_shift)
    causal = same_chunk & (col <= row)
    tri = causal.astype(BF16)
    ones = same_chunk.astype(BF16)
    parts = _split3(log_a)
    b = sum(_dot(tri, part) for part in parts)
    b_last = sum(_dot(ones, part) for part in parts)

    q_dec = (q_ref[...].astype(F32) * (GLA_DK ** -0.5) * jnp.exp(b)).astype(BF16)
    kf = k_ref[...].astype(F32)
    k_inv = (kf * jnp.exp(-b)).astype(BF16)
    k_end = (kf * jnp.exp(b_last - b)).astype(BF16)
    decay = jnp.exp(b_last)

    for h in range(GLA_HEADS):
        kc = slice(h * GLA_DK, (h + 1) * GLA_DK)
        vc = slice(h * GLA_DV, (h + 1) * GLA_DV)
        v = v_ref[:, vc]
        attn = jnp.where(causal, _dot_nt(q_dec[:, kc], k_inv[:, kc]), 0.0)
        o = _dot(attn.astype(BF16), v)
        state = state_ref[h]
        inter = []
        for ci in range(n_chunks):
            rows = slice(ci * c, (ci + 1) * c)
            inter.append(_dot_nt(q_dec[rows, kc], state.astype(BF16)))
            update = _dot_tn(v[rows, :], k_end[rows, kc])
            state = decay[ci * c:ci * c + 1, kc] * state + update
        state_ref[h] = state
        o = o + jnp.concatenate(inter, axis=0)
        o = _rms(o, gn_ref[...])
        o_ref[:, vc] = (o * jax.nn.silu(r_ref[:, vc].astype(F32))).astype(o_ref.dtype)


def _gla(proj, h, w_low, w_dec, b_dec, gla_norm, *, tb=512):
    n, d = h.shape
    tb = min(tb, n)
    return pl.pallas_call(
        functools.partial(_gla_kernel, tb=tb),
        out_shape=jax.ShapeDtypeStruct((n, GLA_V_WIDTH), BF16),
        grid=(n // tb,),
        in_specs=[
            pl.BlockSpec((tb, GLA_K_WIDTH), lambda i: (i, COL_GQ // GLA_K_WIDTH)),
            pl.BlockSpec((tb, GLA_K_WIDTH), lambda i: (i, COL_GK // GLA_K_WIDTH)),
            pl.BlockSpec((tb, GLA_V_WIDTH), lambda i: (i, COL_GV // GLA_V_WIDTH)),
            pl.BlockSpec((tb, GLA_V_WIDTH), lambda i: (i, COL_GR // GLA_V_WIDTH)),
            pl.BlockSpec((tb, d), lambda i: (i, 0)),
            pl.BlockSpec((d, LANES), lambda i: (0, 0)),
            pl.BlockSpec((LANES, GLA_K_WIDTH), lambda i: (0, 0)),
            pl.BlockSpec((1, GLA_K_WIDTH), lambda i: (0, 0)),
            pl.BlockSpec((1, GLA_DV), lambda i: (0, 0)),
        ],
        out_specs=pl.BlockSpec((tb, GLA_V_WIDTH), lambda i: (i, 0)),
        scratch_shapes=[pltpu.VMEM((GLA_HEADS, GLA_DV, GLA_DK), F32)],
        compiler_params=_params("arbitrary"),
        name="gla",
    )(proj, proj, proj, proj, h, w_low, w_dec, b_dec, gla_norm)


def _pool_kernel(u_ref, halo_ref, w_ref, sc_ref, o_ref, ext_ref, *, tb):
    i = pl.program_id(0)
    ext_ref[pl.ds(POOL_HALO, tb), :] = u_ref[...].astype(F32)
    ext_ref[pl.ds(0, POOL_HALO), :] = jnp.where(i == 0, 0.0, halo_ref[...].astype(F32))
    t = i * tb + lax.broadcasted_iota(jnp.int32, (tb, 1), 0)
    for g, win in enumerate(POOL_WINDOWS):
        cols = slice(g * POOL_GROUP_DIM, (g + 1) * POOL_GROUP_DIM)
        u = ext_ref[pl.ds(POOL_HALO, tb), cols]
        window_sum = u
        for back in range(1, win):
            window_sum = window_sum + ext_ref[pl.ds(POOL_HALO - back, tb), cols]
        count = jnp.minimum(t + 1, win).astype(F32)
        mixed = window_sum / count - u
        y = _dot(mixed.astype(BF16), w_ref[g])
        o_ref[:, cols] = (y * sc_ref[:, cols]).astype(o_ref.dtype)


def _pool(proj, w_pool, pool_scale, *, tb=512):
    n = proj.shape[0]
    tb = min(tb, n)
    halo_per_block = tb // POOL_HALO
    return pl.pallas_call(
        functools.partial(_pool_kernel, tb=tb),
        out_shape=jax.ShapeDtypeStruct((n, POOL_WIDTH), BF16),
        grid=(n // tb,),
        in_specs=[
            pl.BlockSpec((tb, POOL_WIDTH), lambda i: (i, COL_PU // POOL_WIDTH)),
            pl.BlockSpec((POOL_HALO, POOL_WIDTH),
                         lambda i: (jnp.maximum(i * halo_per_block - 1, 0), COL_PU // POOL_WIDTH)),
            pl.BlockSpec(w_pool.shape, lambda i: (0, 0, 0)),
            pl.BlockSpec((1, POOL_WIDTH), lambda i: (0, 0)),
        ],
        out_specs=pl.BlockSpec((tb, POOL_WIDTH), lambda i: (i, 0)),
        scratch_shapes=[pltpu.VMEM((POOL_HALO + tb, POOL_WIDTH), F32)],
        compiler_params=_params("parallel"),
        name="pool",
    )(proj, proj, w_pool, pool_scale)


def _merge_kernel(h_ref, a_ref, b_ref, c_ref, wg0, wg1, wg2, bg0, bg1, bg2, wb0, wb1, wb2, o_ref):
    h = h_ref[...]
    total = None
    for br_ref, wg, bg, wb in ((a_ref, wg0, bg0, wb0), (b_ref, wg1, bg1, wb1), (c_ref, wg2, bg2, wb2)):
        gate = jax.nn.sigmoid(_dot(h, wg[...].astype(BF16)) + bg[...])
        term = gate * _dot(br_ref[...], wb[...].astype(BF16))
        total = term if total is None else total + term
    o_ref[...] = total.astype(o_ref.dtype)


def _merge(h, o_moba, o_gla, o_pool, w_gate, b_gate, w_branch, l, *, tm=1024, tn=256):
    n, d = h.shape
    tm = min(tm, n)
    bw = o_moba.shape[1]
    tiles = d // tn
    row_spec = lambda width: pl.BlockSpec((tm, width), lambda i, j: (i, 0))
    gate_specs = [pl.BlockSpec((None, d, tn), lambda i, j, k=k: (l, 0, k * tiles + j)) for k in range(3)]
    bias_specs = [pl.BlockSpec((1, tn), lambda i, j, k=k: (0, k * tiles + j)) for k in range(3)]
    branch_specs = [pl.BlockSpec((None, None, bw, tn), lambda i, j, k=k: (l, k, 0, j)) for k in range(3)]
    return pl.pallas_call(
        _merge_kernel,
        out_shape=jax.ShapeDtypeStruct((n, d), BF16),
        grid=(n // tm, tiles),
        in_specs=[row_spec(d), row_spec(bw), row_spec(bw), row_spec(bw)] + gate_specs + bias_specs + branch_specs,
        out_specs=pl.BlockSpec((tm, tn), lambda i, j: (i, j)),
        compiler_params=_params("parallel", "arbitrary"),
        name="merge",
    )(h, o_moba, o_gla, o_pool, w_gate, w_gate, w_gate, b_gate, b_gate, b_gate, w_branch, w_branch, w_branch)


def _matmul_residual_kernel(a_ref, w_ref, x_ref, o_ref):
    o_ref[...] = x_ref[...] + _dot(a_ref[...], w_ref[...].astype(BF16))


def _matmul_residual(a, w, l, x, *, tm=1024, tn=512):
    n, k = a.shape
    d = w.shape[-1]
    tm = min(tm, n)
    return pl.pallas_call(
        _matmul_residual_kernel,
        out_shape=jax.ShapeDtypeStruct((n, d), F32),
        grid=(n // tm, d // tn),
        in_specs=[
            pl.BlockSpec((tm, k), lambda i, j: (i, 0)),
            pl.BlockSpec((None, k, tn), lambda i, j: (l, 0, j)),
            pl.BlockSpec((tm, tn), lambda i, j: (i, j)),
        ],
        out_specs=pl.BlockSpec((tm, tn), lambda i, j: (i, j)),
        compiler_params=_params("parallel", "arbitrary"),
        name="mix_out",
    )(a, w, x)


def _norm_matmul_kernel(x_ref, g_ref, w_ref, o_ref):
    o_ref[...] = _dot(_rms(x_ref[...], g_ref[...]).astype(BF16), w_ref[...].astype(BF16)).astype(o_ref.dtype)


def _mem_kv(mem, g, w_xkv, l):
    n, d = mem.shape
    width = w_xkv.shape[-1]
    return pl.pallas_call(
        _norm_matmul_kernel,
        out_shape=jax.ShapeDtypeStruct((n, width), BF16),
        grid=(1,),
        in_specs=[
            pl.BlockSpec((n, d), lambda i: (0, 0)),
            pl.BlockSpec((1, d), lambda i: (0, 0)),
            pl.BlockSpec((None, d, width), lambda i: (l, 0, 0)),
        ],
        out_specs=pl.BlockSpec((n, width), lambda i: (0, 0)),
        compiler_params=_params("arbitrary"),
        name="mem_kv",
    )(mem, g, w_xkv)


def _xattn_kernel(x_ref, g_ref, wq_ref, kv_ref, wo_ref, gm_ref, wr_ref, br_ref,
                  xo_ref, h2_ref, route_ref, counts_ref, wq_b, wo_b, counts_acc, *, tm):
    i = pl.program_id(0)

    @pl.when(i == 0)
    def _():
        wq_b[...] = wq_ref[...].astype(BF16)
        wo_b[...] = wo_ref[...].astype(BF16)
        counts_acc[...] = jnp.zeros_like(counts_acc)

    x = x_ref[...]
    h = _rms(x, g_ref[...]).astype(BF16)
    q = (_dot(h, wq_b[...]) * (XATTN_HEAD_DIM ** -0.5)).astype(BF16)
    heads = []
    for hd in range(XATTN_HEADS):
        kc = slice(hd * XATTN_HEAD_DIM, (hd + 1) * XATTN_HEAD_DIM)
        vc = slice(XATTN_WIDTH + hd * XATTN_HEAD_DIM, XATTN_WIDTH + (hd + 1) * XATTN_HEAD_DIM)
        s = _dot_nt(q[:, kc], kv_ref[:, kc])
        p = jnp.exp(s - jnp.max(s, axis=1, keepdims=True))
        p = p / jnp.sum(p, axis=1, keepdims=True)
        heads.append(_dot(p.astype(BF16), kv_ref[:, vc]).astype(BF16))
    o = jnp.concatenate(heads, axis=1)
    xn = x + _dot(o, wo_b[...])
    xo_ref[...] = xn
    h2 = _rms(xn, gm_ref[...])
    h2_ref[...] = h2

    h_hi, h_mid, h_lo = _split3(h2)
    w_hi, w_mid, w_lo = wr_ref[0], wr_ref[1], wr_ref[2]
    lg = (_dot(h_hi, w_hi) + (_dot(h_hi, w_mid) + _dot(h_mid, w_hi))
          + (_dot(h_hi, w_lo) + _dot(h_mid, w_mid) + _dot(h_lo, w_hi))) + br_ref[...]

    lane = lax.broadcasted_iota(jnp.int32, lg.shape, 1)
    lane_f = lane.astype(F32)
    big = float(LANES)

    def top1(v):
        top = jnp.max(v, axis=1, keepdims=True)
        return top, jnp.min(jnp.where(v == top, lane_f, big), axis=1, keepdims=True)

    g_logit = jnp.where(lane < MOE_GROUPS, lg, NEG_INF)
    g_top, g_idx = top1(g_logit)
    g_w = 1.0 / jnp.sum(jnp.exp(g_logit - g_top), axis=1, keepdims=True)
    first = MOE_GROUPS + MOE_EXPERTS_PER_GROUP * g_idx
    e_logit = jnp.where((lane_f >= first) & (lane_f < first + MOE_EXPERTS_PER_GROUP), lg, NEG_INF)
    e_top1, lane1 = top1(e_logit)
    e_top2, lane2 = top1(jnp.where(lane_f == lane1, NEG_INF, e_logit))
    z = jnp.sum(jnp.exp(e_logit - e_top1), axis=1, keepdims=True)
    p1 = 1.0 / z
    p2 = jnp.exp(e_top2 - e_top1) / z
    w1 = g_w * p1 / (p1 + p2)
    w2 = g_w * p2 / (p1 + p2)

    two_hot = (lane_f == lane1) | (lane_f == lane2)
    row = lax.broadcasted_iota(jnp.int32, (tm, tm), 0)
    col = lax.broadcasted_iota(jnp.int32, (tm, tm), 1)
    before = _dot((col < row).astype(BF16), two_hot.astype(BF16)) + counts_acc[...]
    rank1 = jnp.sum(jnp.where(lane_f == lane1, before, 0.0), axis=1, keepdims=True)
    rank2 = jnp.sum(jnp.where(lane_f == lane2, before, 0.0), axis=1, keepdims=True)
    counts_acc[...] += jnp.sum(two_hot.astype(F32), axis=0, keepdims=True)
    counts_ref[...] = counts_acc[...]

    record = jnp.zeros(lg.shape, F32)
    for slot, val in ((ROUTE_E1, lane1 - MOE_GROUPS), (ROUTE_E2, lane2 - MOE_GROUPS), (ROUTE_W1, w1),
                      (ROUTE_W2, w2), (ROUTE_RANK1, rank1), (ROUTE_RANK2, rank2)):
        record = jnp.where(lane == slot, val, record)
    route_ref[...] = record


def _xattn(x, g, wq, kv, wo, l, g_moe, w_route3, b_route, *, tm=512):
    n, d = x.shape
    tm = min(tm, n)
    full = lambda a: pl.BlockSpec(a.shape, lambda i: (0,) * a.ndim)
    layer = lambda a: pl.BlockSpec((None,) + a.shape[1:], lambda i: (l,) + (0,) * (a.ndim - 1))
    rows = lambda width: pl.BlockSpec((tm, width), lambda i: (i, 0))
    return pl.pallas_call(
        functools.partial(_xattn_kernel, tm=tm),
        out_shape=(jax.ShapeDtypeStruct((n, d), F32), jax.ShapeDtypeStruct((n, d), F32),
                   jax.ShapeDtypeStruct((n, LANES), F32), jax.ShapeDtypeStruct((1, LANES), F32)),
        grid=(n // tm,),
        in_specs=[rows(d), full(g), layer(wq), full(kv), layer(wo), full(g_moe), full(w_route3), full(b_route)],
        out_specs=(rows(d), rows(d), rows(LANES), pl.BlockSpec((1, LANES), lambda i: (0, 0))),
        scratch_shapes=[pltpu.VMEM(wq.shape[1:], BF16), pltpu.VMEM(wo.shape[1:], BF16),
                        pltpu.VMEM((1, LANES), F32)],
        compiler_params=_params("arbitrary"),
        name="xattn_route",
    )(x, g, wq, kv, wo, g_moe, w_route3, b_route)


def _moe_kernel(be_ref, nused_ref, x_ref, w1_ref, w3_ref, w2_ref, o_ref, w1b, w3b, w2b):
    b = pl.program_id(0)

    @pl.when(b < nused_ref[0])
    def _():
        prev = be_ref[jnp.maximum(b - 1, 0)]

        @pl.when((b == 0) | (be_ref[b] != prev))
        def _():
            w1b[...] = w1_ref[...].astype(BF16)
            w3b[...] = w3_ref[...].astype(BF16)
            w2b[...] = w2_ref[...].astype(BF16)

        x = x_ref[...].astype(BF16)
        hidden = jax.nn.silu(_dot(x, w1b[...])) * _dot(x, w3b[...])
        o_ref[...] = _dot(hidden.astype(BF16), w2b[...])

    @pl.when(b >= nused_ref[0])
    def _():
        o_ref[...] = jnp.zeros_like(o_ref)


def _moe_experts(block_expert, n_used, x_rows, w1, w3, w2, l):
    p, d = x_rows.shape
    rb = MOE_ROW_BLOCK
    dff = w1.shape[-1]
    grid_spec = pltpu.PrefetchScalarGridSpec(
        num_scalar_prefetch=2,
        grid=(p // rb,),
        in_specs=[
            pl.BlockSpec((rb, d), lambda b, be, nu: (b, 0)),
            pl.BlockSpec((None, None, d, dff), lambda b, be, nu: (l, be[b], 0, 0)),
            pl.BlockSpec((None, None, d, dff), lambda b, be, nu: (l, be[b], 0, 0)),
            pl.BlockSpec((None, None, dff, d), lambda b, be, nu: (l, be[b], 0, 0)),
        ],
        out_specs=pl.BlockSpec((rb, d), lambda b, be, nu: (b, 0)),
        scratch_shapes=[pltpu.VMEM((d, dff), BF16), pltpu.VMEM((d, dff), BF16), pltpu.VMEM((dff, d), BF16)],
    )
    return pl.pallas_call(
        _moe_kernel,
        out_shape=jax.ShapeDtypeStruct((p, d), F32),
        grid_spec=grid_spec,
        compiler_params=_params("arbitrary"),
        name="moe_experts",
    )(block_expert, n_used, x_rows, w1, w3, w2)


def _dispatch_layout(route, counts):
    n = route.shape[0]
    rb = MOE_ROW_BLOCK
    e1 = route[:, ROUTE_E1].astype(jnp.int32)
    e2 = route[:, ROUTE_E2].astype(jnp.int32)
    counts = counts[0, MOE_GROUPS:MOE_GROUPS + MOE_EXPERTS].astype(jnp.int32)
    padded = (counts + rb - 1) // rb * rb
    pad_ends = jnp.cumsum(padded)
    pad_starts = pad_ends - padded
    dest1 = pad_starts[e1] + route[:, ROUTE_RANK1].astype(jnp.int32)
    dest2 = pad_starts[e2] + route[:, ROUTE_RANK2].astype(jnp.int32)
    n_blocks = (n * MOE_TOPK + MOE_EXPERTS * (rb - 1) + rb - 1) // rb
    tok = jnp.arange(n, dtype=jnp.int32)
    row_tok = jnp.zeros((n_blocks * rb,), jnp.int32).at[jnp.concatenate([dest1, dest2])].set(
        jnp.concatenate([tok, tok]), mode="promise_in_bounds", unique_indices=True)
    block_start = jnp.arange(n_blocks, dtype=jnp.int32) * rb
    block_expert = jnp.minimum(jnp.sum(pad_ends[None, :] <= block_start[:, None], axis=1), MOE_EXPERTS - 1)
    n_used = pad_ends[-1:] // rb
    return row_tok, block_expert.astype(jnp.int32), n_used.astype(jnp.int32), dest1, dest2


def _final_norm_kernel(x_ref, g_ref, o_ref):
    o_ref[...] = _rms(x_ref[...], g_ref[...])


def _final_norm(x, g, *, tm=1024):
    n, d = x.shape
    tm = min(tm, n)
    return pl.pallas_call(
        _final_norm_kernel,
        out_shape=jax.ShapeDtypeStruct((n, d), F32),
        grid=(n // tm,),
        in_specs=[pl.BlockSpec((tm, d), lambda i: (i, 0)), pl.BlockSpec((1, d), lambda i: (0, 0))],
        out_specs=pl.BlockSpec((tm, d), lambda i: (i, 0)),
        compiler_params=_params("parallel"),
        name="final_norm",
    )(x, g)


def _rope_tables(positions):
    inv = 1.0 / (ROPE_THETA ** (jnp.arange(0, MOBA_HEAD_DIM, 2, dtype=F32) / MOBA_HEAD_DIM))
    ang = positions.astype(F32)[:, None] * inv
    cos, sin = jnp.cos(ang), jnp.sin(ang)
    return jnp.concatenate([cos, cos], axis=-1), jnp.concatenate([-sin, sin], axis=-1)


def _layer(x, mem, cos_f, sin_f, p, l):
    w_in = p["w_in"]
    w_pool_in = w_in[l, :, W_IN_POOL:]
    w_low = jnp.pad(w_in[l, :, W_IN_LOW:W_IN_POOL], ((0, 0), (0, LANES - GLA_RANK))).astype(BF16)
    proj, h = _inproj(x, _layer_vec(p["norm_mix"], l), w_in, w_pool_in, l, cos_f, sin_f)
    o_moba = _moba(proj)
    w_dec = jnp.pad(p["w_gla_decay"][l], ((0, LANES - GLA_RANK), (0, 0))).astype(BF16)
    o_gla = _gla(proj, h, w_low, w_dec, _layer_vec(p["b_gla_decay"], l), _layer_vec(p["gla_norm"], l))
    o_pool = _pool(proj, p["w_pool"][l].astype(BF16), _layer_vec(p["pool_scale"], l))
    merged = _merge(h, o_moba, o_gla, o_pool, p["w_gate"], _layer_vec(p["b_gate"], l), p["w_branch"], l)
    x = _matmul_residual(merged, p["w_mix_out"], l, x)

    kv = _mem_kv(mem, _layer_vec(p["norm_mem"], l), p["w_xkv"], l)
    w_route = jnp.concatenate([p["w_route_group"][l], p["w_route_expert"][l]], axis=1)
    n_route = w_route.shape[1]
    w_route3 = jnp.stack(_split3(jnp.pad(w_route, ((0, 0), (0, LANES - n_route)))))
    b_route = jnp.pad(jnp.concatenate([p["b_route_group"][l], p["b_route_expert"][l]]), (0, LANES - n_route))
    x, h2, route, counts = _xattn(x, _layer_vec(p["norm_xattn"], l), p["w_xq"], kv, p["w_xo"], l,
                                  _layer_vec(p["norm_moe"], l), w_route3, b_route.reshape(1, -1))

    row_tok, block_expert, n_used, dest1, dest2 = _dispatch_layout(route, counts)
    x_rows = h2.at[row_tok].get(mode="promise_in_bounds")
    y_rows = _moe_experts(block_expert, n_used, x_rows, p["w_exp_gate"], p["w_exp_up"], p["w_exp_down"], l)
    y1 = y_rows.at[dest1].get(mode="promise_in_bounds")
    y2 = y_rows.at[dest2].get(mode="promise_in_bounds")
    return x + (route[:, ROUTE_W1:ROUTE_W1 + 1] * y1 + route[:, ROUTE_W2:ROUTE_W2 + 1] * y2)


def kernel(x, mem, positions, norm_mix, w_in, w_gla_decay, b_gla_decay, gla_norm, w_pool, pool_scale, w_branch, w_gate, b_gate, w_mix_out, norm_xattn, norm_mem, w_xq, w_xkv, w_xo, norm_moe, w_route_group, b_route_group, w_route_expert, b_route_expert, w_exp_gate, w_exp_up, w_exp_down, norm_final):
    batch, seq, d = x.shape
    assert batch == 1, "kernels are written for a single sequence"
    params = dict(norm_mix=norm_mix, w_in=w_in, w_gla_decay=w_gla_decay, b_gla_decay=b_gla_decay,
                  gla_norm=gla_norm, w_pool=w_pool, pool_scale=pool_scale, w_branch=w_branch, w_gate=w_gate,
                  b_gate=b_gate, w_mix_out=w_mix_out, norm_xattn=norm_xattn, norm_mem=norm_mem, w_xq=w_xq,
                  w_xkv=w_xkv, w_xo=w_xo, norm_moe=norm_moe, w_route_group=w_route_group,
                  b_route_group=b_route_group, w_route_expert=w_route_expert, b_route_expert=b_route_expert,
                  w_exp_gate=w_exp_gate, w_exp_up=w_exp_up, w_exp_down=w_exp_down)
    cos_f, sin_f = _rope_tables(positions[0])
    xs = x[0]
    for l in range(norm_mix.shape[0]):
        xs = _layer(xs, mem[0], cos_f, sin_f, params, l)
    return _final_norm(xs, norm_final.reshape(1, -1)).reshape(batch, seq, d)
```

```python
import functools

import jax
import jax.numpy as jnp
from jax import lax
from jax.experimental import pallas as pl
from jax.experimental.pallas import tpu as pltpu

F32 = jnp.float32
BF16 = jnp.bfloat16

EPS = 1e-6
NEG_INF = -1e30

MOBA_HEADS = 8
MOBA_HEAD_DIM = 128
MOBA_WIDTH = MOBA_HEADS * MOBA_HEAD_DIM
MOBA_BLOCK = 256
MOBA_TOPK = 3
MOBA_KV_BLOCKS = 2
MOBA_HEADS_PER_STEP = 4
ROPE_THETA = 10000.0

GLA_HEADS = 4
GLA_DK = 128
GLA_DV = 256
GLA_K_WIDTH = GLA_HEADS * GLA_DK
GLA_V_WIDTH = GLA_HEADS * GLA_DV
GLA_RANK = 16
GLA_TAU = 16.0
GLA_CHUNK = 64

POOL_WINDOWS = (2, 4, 8, 16)
POOL_GROUP_DIM = 256
POOL_WIDTH = len(POOL_WINDOWS) * POOL_GROUP_DIM
POOL_HALO = 16

XATTN_HEADS = 4
XATTN_HEAD_DIM = 128
XATTN_WIDTH = XATTN_HEADS * XATTN_HEAD_DIM

MOE_GROUPS = 8
MOE_EXPERTS_PER_GROUP = 8
MOE_EXPERTS = MOE_GROUPS * MOE_EXPERTS_PER_GROUP
MOE_TOPK = 2
MOE_ROW_BLOCK = 128

LANES = 128

COL_MQ = 0
COL_MK = COL_MQ + MOBA_WIDTH
COL_MV = COL_MK + MOBA_WIDTH
COL_GQ = COL_MV + MOBA_WIDTH
COL_GK = COL_GQ + GLA_K_WIDTH
COL_GV = COL_GK + GLA_K_WIDTH
COL_GR = COL_GV + GLA_V_WIDTH
COL_PU = COL_GR + GLA_V_WIDTH
PROJ_WIDTH = COL_PU + POOL_WIDTH
W_IN_LOW = COL_PU
W_IN_POOL = COL_PU + GLA_RANK

ROUTE_E1, ROUTE_E2, ROUTE_W1, ROUTE_W2, ROUTE_RANK1, ROUTE_RANK2 = range(6)

VMEM_LIMIT = 56 * 1024 * 1024


def _params(*semantics):
    return pltpu.CompilerParams(dimension_semantics=semantics, vmem_limit_bytes=VMEM_LIMIT)


def _rms(x, g):
    return x * lax.rsqrt(jnp.mean(x * x, axis=-1, keepdims=True) + EPS) * g


def _dot(a, b):
    return jnp.dot(a, b, preferred_element_type=F32)


def _dot_nt(a, b):
    return lax.dot_general(a, b, (((1,), (1,)), ((), ())), preferred_element_type=F32)


def _dot_tn(a, b):
    return lax.dot_general(a, b, (((0,), (0,)), ((), ())), preferred_element_type=F32)


def _split3(x):
    hi = x.astype(BF16)
    r1 = x - hi.astype(F32)
    mid = r1.astype(BF16)
    lo = (r1 - mid.astype(F32)).astype(BF16)
    return hi, mid, lo


def _layer_vec(v, l):
    return v[l].reshape(1, -1)


def _inproj_kernel(x_ref, g_ref, w_ref, wp_ref, cos_ref, sin_ref, o_ref, h_ref, *, tn, n_main):
    j = pl.program_id(1)

    @pl.when(j == 0)
    def _():
        h_ref[...] = _rms(x_ref[...], g_ref[...]).astype(BF16)

    n_rope = (2 * MOBA_WIDTH) // tn
    n_q = MOBA_WIDTH // tn

    @pl.when(j < n_rope)
    def _():
        acc = _dot(h_ref[...], w_ref[...])
        scale = jnp.where(j < n_q, MOBA_HEAD_DIM ** -0.5, 1.0).astype(F32)
        cos = cos_ref[...] * scale
        sin = sin_ref[...] * scale
        for hh in range(tn // MOBA_HEAD_DIM):
            cols = slice(hh * MOBA_HEAD_DIM, (hh + 1) * MOBA_HEAD_DIM)
            a = acc[:, cols]
            rot = pltpu.roll(a, MOBA_HEAD_DIM // 2, axis=1)
            o_ref[:, cols] = (a * cos + rot * sin).astype(o_ref.dtype)

    @pl.when((j >= n_rope) & (j < n_main))
    def _():
        o_ref[...] = _dot(h_ref[...], w_ref[...]).astype(o_ref.dtype)

    @pl.when(j >= n_main)
    def _():
        o_ref[...] = _dot(h_ref[...], wp_ref[...]).astype(o_ref.dtype)


def _inproj(x, g, w_in, w_pool_in, l, cos_f, sin_f, *, tm=1024, tn=512):
    n, d = x.shape
    tm = min(tm, n)
    n_main = COL_PU // tn
    n_tiles = PROJ_WIDTH // tn
    return pl.pallas_call(
        functools.partial(_inproj_kernel, tn=tn, n_main=n_main),
        out_shape=(jax.ShapeDtypeStruct((n, PROJ_WIDTH), BF16), jax.ShapeDtypeStruct((n, d), BF16)),
        grid=(n // tm, n_tiles),
        in_specs=[
            pl.BlockSpec((tm, d), lambda i, j: (i, 0)),
            pl.BlockSpec((1, d), lambda i, j: (0, 0)),
            pl.BlockSpec((None, d, tn), lambda i, j: (l, 0, jnp.minimum(j, n_main - 1))),
            pl.BlockSpec((d, tn), lambda i, j: (0, jnp.maximum(j - n_main, 0))),
            pl.BlockSpec((tm, MOBA_HEAD_DIM), lambda i, j: (i, 0)),
            pl.BlockSpec((tm, MOBA_HEAD_DIM), lambda i, j: (i, 0)),
        ],
        out_specs=(
            pl.BlockSpec((tm, tn), lambda i, j: (i, j)),
            pl.BlockSpec((tm, d), lambda i, j: (i, 0)),
        ),
        compiler_params=_params("parallel", "arbitrary"),
        name="inproj",
    )(x, g, w_in, w_pool_in, cos_f, sin_f)


def _moba_kernel(q_ref, k_ref, v_ref, o_ref, kmean_ref, *, n_blocks):
    qi = pl.program_id(1)
    blk = MOBA_BLOCK
    hd = MOBA_HEAD_DIM
    tile = MOBA_KV_BLOCKS * blk
    blk_shift = blk.bit_length() - 1
    head_cols = [slice(hh * hd, (hh + 1) * hd) for hh in range(MOBA_HEADS_PER_STEP)]

    @pl.when(qi == 0)
    def _():
        kmean_ref[...] = jnp.zeros_like(kmean_ref)

        def fill(b, carry):
            kb = k_ref[pl.ds(pl.multiple_of(b * blk, blk), blk), :].astype(F32)
            kmean_ref[pl.ds(b, 1), :] = jnp.mean(kb, axis=0, keepdims=True)
            return carry

        lax.fori_loop(0, n_blocks, fill, 0)

    start = pl.multiple_of(qi * blk, blk)
    lane = lax.broadcasted_iota(jnp.int32, (blk, LANES), 1)
    lane_f = lane.astype(F32)
    row = lax.broadcasted_iota(jnp.int32, (blk, blk), 0)
    col = lax.broadcasted_iota(jnp.int32, (blk, blk), 1)
    ones_own = jnp.ones((blk, hd), BF16)
    ones_tile = jnp.ones((tile, hd), BF16)

    q_augs, carry = [], []
    for cols in head_cols:
        q = q_ref[:, cols]
        kmean_hi, kmean_mid, _ = _split3(kmean_ref[:, cols])
        gate = _dot_nt(q, kmean_hi) + _dot_nt(q, kmean_mid)
        gate = jnp.where(lane < qi, gate, NEG_INF)
        sel = jnp.zeros(gate.shape, jnp.bool_)
        for _ in range(MOBA_TOPK):
            top = jnp.max(gate, axis=1, keepdims=True)
            idx = jnp.min(jnp.where(gate == top, lane_f, float(LANES)), axis=1, keepdims=True)
            hit = lane_f == idx
            sel = sel | (hit & (top > 0.5 * NEG_INF))
            gate = jnp.where(hit, NEG_INF, gate)
        q_augs.append(jnp.concatenate([q, jnp.where(sel, 0.0, NEG_INF).astype(BF16)], axis=1))

        s = _dot_nt(q, k_ref[pl.ds(start, blk), cols])
        s = jnp.where(col <= row, s, NEG_INF)
        m = jnp.max(s, axis=1, keepdims=True)
        p = jnp.exp(s - m).astype(BF16)
        acc = _dot(p, jnp.concatenate([v_ref[pl.ds(start, blk), cols], ones_own], axis=1))
        carry += [m, acc]

    tile_row = lax.broadcasted_iota(jnp.int32, (tile, hd), 0)
    tile_lane = lax.broadcasted_iota(jnp.int32, (tile, hd), 1)

    def body(t, carry):
        off = pl.multiple_of(t * tile, tile)
        block_hot = (tile_lane == t * MOBA_KV_BLOCKS + (tile_row >> blk_shift)).astype(BF16)
        out = []
        for hh, cols in enumerate(head_cols):
            m, acc = carry[2 * hh], carry[2 * hh + 1]
            k_aug = jnp.concatenate([k_ref[pl.ds(off, tile), cols], block_hot], axis=1)
            s = _dot_nt(q_augs[hh], k_aug)
            m_new = jnp.maximum(m, jnp.max(s, axis=1, keepdims=True))
            alpha = jnp.exp(m - m_new)
            p = jnp.exp(s - m_new).astype(BF16)
            v_aug = jnp.concatenate([v_ref[pl.ds(off, tile), cols], ones_tile], axis=1)
            out += [m_new, alpha * acc + _dot(p, v_aug)]
        return tuple(out)

    n_tiles = (qi + MOBA_KV_BLOCKS - 1) // MOBA_KV_BLOCKS
    carry = lax.fori_loop(0, n_tiles, body, tuple(carry))
    for hh, cols in enumerate(head_cols):
        acc = carry[2 * hh + 1]
        o_ref[:, cols] = (acc[:, :hd] / acc[:, hd:]).astype(o_ref.dtype)


def _moba(proj):
    n = proj.shape[0]
    n_blocks = n // MOBA_BLOCK
    assert n_blocks % MOBA_KV_BLOCKS == 0 and n_blocks <= LANES
    width = MOBA_HEADS_PER_STEP * MOBA_HEAD_DIM
    return pl.pallas_call(
        functools.partial(_moba_kernel, n_blocks=n_blocks),
        out_shape=jax.ShapeDtypeStruct((n, MOBA_WIDTH), BF16),
        grid=(MOBA_HEADS // MOBA_HEADS_PER_STEP, n_blocks),
        in_specs=[
            pl.BlockSpec((MOBA_BLOCK, width), lambda h, i: (i, COL_MQ // width + h)),
            pl.BlockSpec((n, width), lambda h, i: (0, COL_MK // width + h), pipeline_mode=pl.Buffered(1)),
            pl.BlockSpec((n, width), lambda h, i: (0, COL_MV // width + h), pipeline_mode=pl.Buffered(1)),
        ],
        out_specs=pl.BlockSpec((MOBA_BLOCK, width), lambda h, i: (i, h)),
        scratch_shapes=[pltpu.VMEM((LANES, width), F32)],
        compiler_params=_params("parallel", "arbitrary"),
        name="moba",
    )(proj, proj, proj)


def _gla_kernel(q_ref, k_ref, v_ref, r_ref, h_ref, wlow_ref, wdec_ref, bdec_ref, gn_ref, o_ref, state_ref, *, tb):
    i = pl.program_id(0)
    c = GLA_CHUNK
    n_chunks = tb // c
    chunk_shift = c.bit_length() - 1

    @pl.when(i == 0)
    def _():
        state_ref[...] = jnp.zeros_like(state_ref)

    g_low = _dot(h_ref[...], wlow_ref[...]).astype(BF16)
    z = _dot(g_low, wdec_ref[...]) + bdec_ref[...]
    log_a = jax.nn.log_sigmoid(z) / GLA_TAU

    row = lax.broadcasted_iota(jnp.int32, (tb, tb), 0)
    col = lax.broadcasted_iota(jnp.int32, (tb, tb), 1)
    same_chunk = (row >> chunk_shift) == (col >> chunk_shift)
    causal = same_chunk & (col <= row)
    tri = causal.astype(BF16)
    ones = same_chunk.astype(BF16)
    parts = _split3(log_a)
    b = sum(_dot(tri, part) for part in parts)
    b_last = sum(_dot(ones, part) for part in parts)

    q_dec = (q_ref[...].astype(F32) * (GLA_DK ** -0.5) * jnp.exp(b)).astype(BF16)
    kf = k_ref[...].astype(F32)
    k_inv = (kf * jnp.exp(-b)).astype(BF16)
    k_end = (kf * jnp.exp(b_last - b)).astype(BF16)
    decay = jnp.exp(b_last)

    for h in range(GLA_HEADS):
        kc = slice(h * GLA_DK, (h + 1) * GLA_DK)
        vc = slice(h * GLA_DV, (h + 1) * GLA_DV)
        v = v_ref[:, vc]
        attn = jnp.where(causal, _dot_nt(q_dec[:, kc], k_inv[:, kc]), 0.0)
        o = _dot(attn.astype(BF16), v)
        state = state_ref[h]
        inter = []
        for ci in range(n_chunks):
            rows = slice(ci * c, (ci + 1) * c)
            inter.append(_dot_nt(q_dec[rows, kc], state.astype(BF16)))
            update = _dot_tn(v[rows, :], k_end[rows, kc])
            state = decay[ci * c:ci * c + 1, kc] * state + update
        state_ref[h] = state
        o = o + jnp.concatenate(inter, axis=0)
        o = _rms(o, gn_ref[...])
        o_ref[:, vc] = (o * jax.nn.silu(r_ref[:, vc].astype(F32))).astype(o_ref.dtype)


def _gla(proj, h, w_low, w_dec, b_dec, gla_norm, *, tb=512):
    n, d = h.shape
    tb = min(tb, n)
    return pl.pallas_call(
        functools.partial(_gla_kernel, tb=tb),
        out_shape=jax.ShapeDtypeStruct((n, GLA_V_WIDTH), BF16),
        grid=(n // tb,),
        in_specs=[
            pl.BlockSpec((tb, GLA_K_WIDTH), lambda i: (i, COL_GQ // GLA_K_WIDTH)),
            pl.BlockSpec((tb, GLA_K_WIDTH), lambda i: (i, COL_GK // GLA_K_WIDTH)),
            pl.BlockSpec((tb, GLA_V_WIDTH), lambda i: (i, COL_GV // GLA_V_WIDTH)),
            pl.BlockSpec((tb, GLA_V_WIDTH), lambda i: (i, COL_GR // GLA_V_WIDTH)),
            pl.BlockSpec((tb, d), lambda i: (i, 0)),
            pl.BlockSpec((d, LANES), lambda i: (0, 0)),
            pl.BlockSpec((LANES, GLA_K_WIDTH), lambda i: (0, 0)),
            pl.BlockSpec((1, GLA_K_WIDTH), lambda i: (0, 0)),
            pl.BlockSpec((1, GLA_DV), lambda i: (0, 0)),
        ],
        out_specs=pl.BlockSpec((tb, GLA_V_WIDTH), lambda i: (i, 0)),
        scratch_shapes=[pltpu.VMEM((GLA_HEADS, GLA_DV, GLA_DK), F32)],
        compiler_params=_params("arbitrary"),
        name="gla",
    )(proj, proj, proj, proj, h, w_low, w_dec, b_dec, gla_norm)


def _pool_kernel(u_ref, halo_ref, w_ref, sc_ref, o_ref, ext_ref, *, tb):
    i = pl.program_id(0)
    ext_ref[pl.ds(POOL_HALO, tb), :] = u_ref[...].astype(F32)
    ext_ref[pl.ds(0, POOL_HALO), :] = jnp.where(i == 0, 0.0, halo_ref[...].astype(F32))
    t = i * tb + lax.broadcasted_iota(jnp.int32, (tb, 1), 0)
    for g, win in enumerate(POOL_WINDOWS):
        cols = slice(g * POOL_GROUP_DIM, (g + 1) * POOL_GROUP_DIM)
        u = ext_ref[pl.ds(POOL_HALO, tb), cols]
        window_sum = u
        for back in range(1, win):
            window_sum = window_sum + ext_ref[pl.ds(POOL_HALO - back, tb), cols]
        count = jnp.minimum(t + 1, win).astype(F32)
        mixed = window_sum / count - u
        y = _dot(mixed.astype(BF16), w_ref[g])
        o_ref[:, cols] = (y * sc_ref[:, cols]).astype(o_ref.dtype)


def _pool(proj, w_pool, pool_scale, *, tb=512):
    n = proj.shape[0]
    tb = min(tb, n)
    halo_per_block = tb // POOL_HALO
    return pl.pallas_call(
        functools.partial(_pool_kernel, tb=tb),
        out_shape=jax.ShapeDtypeStruct((n, POOL_WIDTH), BF16),
        grid=(n // tb,),
        in_specs=[
            pl.BlockSpec((tb, POOL_WIDTH), lambda i: (i, COL_PU // POOL_WIDTH)),
            pl.BlockSpec((POOL_HALO, POOL_WIDTH),
                         lambda i: (jnp.maximum(i * halo_per_block - 1, 0), COL_PU // POOL_WIDTH)),
            pl.BlockSpec(w_pool.shape, lambda i: (0, 0, 0)),
            pl.BlockSpec((1, POOL_WIDTH), lambda i: (0, 0)),
        ],
        out_specs=pl.BlockSpec((tb, POOL_WIDTH), lambda i: (i, 0)),
        scratch_shapes=[pltpu.VMEM((POOL_HALO + tb, POOL_WIDTH), F32)],
        compiler_params=_params("parallel"),
        name="pool",
    )(proj, proj, w_pool, pool_scale)


def _merge_kernel(h_ref, a_ref, b_ref, c_ref, wg0, wg1, wg2, bg0, bg1, bg2, wb0, wb1, wb2, o_ref):
    h = h_ref[...]
    total = None
    for br_ref, wg, bg, wb in ((a_ref, wg0, bg0, wb0), (b_ref, wg1, bg1, wb1), (c_ref, wg2, bg2, wb2)):
        gate = jax.nn.sigmoid(_dot(h, wg[...].astype(BF16)) + bg[...])
        term = gate * _dot(br_ref[...], wb[...].astype(BF16))
        total = term if total is None else total + term
    o_ref[...] = total.astype(o_ref.dtype)


def _merge(h, o_moba, o_gla, o_pool, w_gate, b_gate, w_branch, l, *, tm=1024, tn=256):
    n, d = h.shape
    tm = min(tm, n)
    bw = o_moba.shape[1]
    tiles = d // tn
    row_spec = lambda width: pl.BlockSpec((tm, width), lambda i, j: (i, 0))
    gate_specs = [pl.BlockSpec((None, d, tn), lambda i, j, k=k: (l, 0, k * tiles + j)) for k in range(3)]
    bias_specs = [pl.BlockSpec((1, tn), lambda i, j, k=k: (0, k * tiles + j)) for k in range(3)]
    branch_specs = [pl.BlockSpec((None, None, bw, tn), lambda i, j, k=k: (l, k, 0, j)) for k in range(3)]
    return pl.pallas_call(
        _merge_kernel,
        out_shape=jax.ShapeDtypeStruct((n, d), BF16),
        grid=(n // tm, tiles),
        in_specs=[row_spec(d), row_spec(bw), row_spec(bw), row_spec(bw)] + gate_specs + bias_specs + branch_specs,
        out_specs=pl.BlockSpec((tm, tn), lambda i, j: (i, j)),
        compiler_params=_params("parallel", "arbitrary"),
        name="merge",
    )(h, o_moba, o_gla, o_pool, w_gate, w_gate, w_gate, b_gate, b_gate, b_gate, w_branch, w_branch, w_branch)


def _matmul_residual_kernel(a_ref, w_ref, x_ref, o_ref):
    o_ref[...] = x_ref[...] + _dot(a_ref[...], w_ref[...].astype(BF16))


def _matmul_residual(a, w, l, x, *, tm=1024, tn=512):
    n, k = a.shape
    d = w.shape[-1]
    tm = min(tm, n)
    return pl.pallas_call(
        _matmul_residual_kernel,
        out_shape=jax.ShapeDtypeStruct((n, d), F32),
        grid=(n // tm, d // tn),
        in_specs=[
            pl.BlockSpec((tm, k), lambda i, j: (i, 0)),
            pl.BlockSpec((None, k, tn), lambda i, j: (l, 0, j)),
            pl.BlockSpec((tm, tn), lambda i, j: (i, j)),
        ],
        out_specs=pl.BlockSpec((tm, tn), lambda i, j: (i, j)),
        compiler_params=_params("parallel", "arbitrary"),
        name="mix_out",
    )(a, w, x)


def _norm_matmul_kernel(x_ref, g_ref, w_ref, o_ref):
    o_ref[...] = _dot(_rms(x_ref[...], g_ref[...]).astype(BF16), w_ref[...].astype(BF16)).astype(o_ref.dtype)


def _mem_kv(mem, g, w_xkv, l):
    n, d = mem.shape
    width = w_xkv.shape[-1]
    return pl.pallas_call(
        _norm_matmul_kernel,
        out_shape=jax.ShapeDtypeStruct((n, width), BF16),
        grid=(1,),
        in_specs=[
            pl.BlockSpec((n, d), lambda i: (0, 0)),
            pl.BlockSpec((1, d), lambda i: (0, 0)),
            pl.BlockSpec((None, d, width), lambda i: (l, 0, 0)),
        ],
        out_specs=pl.BlockSpec((n, width), lambda i: (0, 0)),
        compiler_params=_params("arbitrary"),
        name="mem_kv",
    )(mem, g, w_xkv)


def _xattn_kernel(x_ref, g_ref, wq_ref, kv_ref, wo_ref, gm_ref, wr_ref, br_ref,
                  xo_ref, h2_ref, route_ref, counts_ref, wq_b, wo_b, counts_acc, *, tm):
    i = pl.program_id(0)

    @pl.when(i == 0)
    def _():
        wq_b[...] = wq_ref[...].astype(BF16)
        wo_b[...] = wo_ref[...].astype(BF16)
        counts_acc[...] = jnp.zeros_like(counts_acc)

    x = x_ref[...]
    h = _rms(x, g_ref[...]).astype(BF16)
    q = (_dot(h, wq_b[...]) * (XATTN_HEAD_DIM ** -0.5)).astype(BF16)
    heads = []
    for hd in range(XATTN_HEADS):
        kc = slice(hd * XATTN_HEAD_DIM, (hd + 1) * XATTN_HEAD_DIM)
        vc = slice(XATTN_WIDTH + hd * XATTN_HEAD_DIM, XATTN_WIDTH + (hd + 1) * XATTN_HEAD_DIM)
        s = _dot_nt(q[:, kc], kv_ref[:, kc])
        p = jnp.exp(s - jnp.max(s, axis=1, keepdims=True))
        p = p / jnp.sum(p, axis=1, keepdims=True)
        heads.append(_dot(p.astype(BF16), kv_ref[:, vc]).astype(BF16))
    o = jnp.concatenate(heads, axis=1)
    xn = x + _dot(o, wo_b[...])
    xo_ref[...] = xn
    h2 = _rms(xn, gm_ref[...])
    h2_ref[...] = h2

    h_hi, h_mid, h_lo = _split3(h2)
    w_hi, w_mid, w_lo = wr_ref[0], wr_ref[1], wr_ref[2]
    lg = (_dot(h_hi, w_hi) + (_dot(h_hi, w_mid) + _dot(h_mid, w_hi))
          + (_dot(h_hi, w_lo) + _dot(h_mid, w_mid) + _dot(h_lo, w_hi))) + br_ref[...]

    lane = lax.broadcasted_iota(jnp.int32, lg.shape, 1)
    lane_f = lane.astype(F32)
    big = float(LANES)

    def top1(v):
        top = jnp.max(v, axis=1, keepdims=True)
        return top, jnp.min(jnp.where(v == top, lane_f, big), axis=1, keepdims=True)

    g_logit = jnp.where(lane < MOE_GROUPS, lg, NEG_INF)
    g_top, g_idx = top1(g_logit)
    g_w = 1.0 / jnp.sum(jnp.exp(g_logit - g_top), axis=1, keepdims=True)
    first = MOE_GROUPS + MOE_EXPERTS_PER_GROUP * g_idx
    e_logit = jnp.where((lane_f >= first) & (lane_f < first + MOE_EXPERTS_PER_GROUP), lg, NEG_INF)
    e_top1, lane1 = top1(e_logit)
    e_top2, lane2 = top1(jnp.where(lane_f == lane1, NEG_INF, e_logit))
    z = jnp.sum(jnp.exp(e_logit - e_top1), axis=1, keepdims=True)
    p1 = 1.0 / z
    p2 = jnp.exp(e_top2 - e_top1) / z
    w1 = g_w * p1 / (p1 + p2)
    w2 = g_w * p2 / (p1 + p2)

    two_hot = (lane_f == lane1) | (lane_f == lane2)
    row = lax.broadcasted_iota(jnp.int32, (tm, tm), 0)
    col = lax.broadcasted_iota(jnp.int32, (tm, tm), 1)
    before = _dot((col < row).astype(BF16), two_hot.astype(BF16)) + counts_acc[...]
    rank1 = jnp.sum(jnp.where(lane_f == lane1, before, 0.0), axis=1, keepdims=True)
    rank2 = jnp.sum(jnp.where(lane_f == lane2, before, 0.0), axis=1, keepdims=True)
    counts_acc[...] += jnp.sum(two_hot.astype(F32), axis=0, keepdims=True)
    counts_ref[...] = counts_acc[...]

    record = jnp.zeros(lg.shape, F32)
    for slot, val in ((ROUTE_E1, lane1 - MOE_GROUPS), (ROUTE_E2, lane2 - MOE_GROUPS), (ROUTE_W1, w1),
                      (ROUTE_W2, w2), (ROUTE_RANK1, rank1), (ROUTE_RANK2, rank2)):
        record = jnp.where(lane == slot, val, record)
    route_ref[...] = record


def _xattn(x, g, wq, kv, wo, l, g_moe, w_route3, b_route, *, tm=512):
    n, d = x.shape
    tm = min(tm, n)
    full = lambda a: pl.BlockSpec(a.shape, lambda i: (0,) * a.ndim)
    layer = lambda a: pl.BlockSpec((None,) + a.shape[1:], lambda i: (l,) + (0,) * (a.ndim - 1))
    rows = lambda width: pl.BlockSpec((tm, width), lambda i: (i, 0))
    return pl.pallas_call(
        functools.partial(_xattn_kernel, tm=tm),
        out_shape=(jax.ShapeDtypeStruct((n, d), F32), jax.ShapeDtypeStruct((n, d), F32),
                   jax.ShapeDtypeStruct((n, LANES), F32), jax.ShapeDtypeStruct((1, LANES), F32)),
        grid=(n // tm,),
        in_specs=[rows(d), full(g), layer(wq), full(kv), layer(wo), full(g_moe), full(w_route3), full(b_route)],
        out_specs=(rows(d), rows(d), rows(LANES), pl.BlockSpec((1, LANES), lambda i: (0, 0))),
        scratch_shapes=[pltpu.VMEM(wq.shape[1:], BF16), pltpu.VMEM(wo.shape[1:], BF16),
                        pltpu.VMEM((1, LANES), F32)],
        compiler_params=_params("arbitrary"),
        name="xattn_route",
    )(x, g, wq, kv, wo, g_moe, w_route3, b_route)


def _moe_kernel(start_ref, count_ref, x_hbm, w1_ref, w3_ref, w2_ref, y_hbm,
                w1b, w3b, w2b, xbuf, ybuf, xsem, ysem):
    e = pl.program_id(0)
    rb = MOE_ROW_BLOCK
    first = start_ref[e]
    n_blocks = count_ref[e]

    def rows(c):
        return pl.ds(pl.multiple_of((first + c) * rb, rb), rb)

    def x_copy(c, slot):
        return pltpu.make_async_copy(x_hbm.at[rows(c), :], xbuf.at[slot], xsem.at[slot])

    def y_copy(c, slot):
        return pltpu.make_async_copy(ybuf.at[slot], y_hbm.at[rows(c), :], ysem.at[slot])

    @pl.when(n_blocks > 0)
    def _():
        x_copy(0, 0).start()

    w1b[...] = w1_ref[...].astype(BF16)
    w3b[...] = w3_ref[...].astype(BF16)
    w2b[...] = w2_ref[...].astype(BF16)

    def body(c, carry):
        slot = c % 2
        x_copy(c, slot).wait()

        @pl.when(c + 1 < n_blocks)
        def _():
            x_copy(c + 1, 1 - slot).start()

        @pl.when(c >= 2)
        def _():
            y_copy(c - 2, slot).wait()

        x = xbuf[slot].astype(BF16)
        hidden = jax.nn.silu(_dot(x, w1b[...])) * _dot(x, w3b[...])
        ybuf[slot] = _dot(hidden.astype(BF16), w2b[...])
        y_copy(c, slot).start()
        return carry

    lax.fori_loop(0, n_blocks, body, 0)

    @pl.when(n_blocks >= 2)
    def _():
        y_copy(n_blocks - 2, n_blocks % 2).wait()

    @pl.when(n_blocks >= 1)
    def _():
        y_copy(n_blocks - 1, (n_blocks - 1) % 2).wait()

    @pl.when(e == pl.num_programs(0) - 1)
    def _():
        ybuf[0] = jnp.zeros(ybuf.shape[1:], ybuf.dtype)

        def zero_copy(c):
            return y_copy(c, 0)

        tail = y_hbm.shape[0] // rb - first

        def start(c, carry):
            zero_copy(c).start()
            return carry

        def wait(c, carry):
            zero_copy(c).wait()
            return carry

        lax.fori_loop(n_blocks, tail, start, 0)
        lax.fori_loop(n_blocks, tail, wait, 0)


def _moe_experts(block_start, block_count, x_rows, w1, w3, w2, l):
    p, d = x_rows.shape
    rb = MOE_ROW_BLOCK
    n_experts, dff = w1.shape[1], w1.shape[-1]
    grid_spec = pltpu.PrefetchScalarGridSpec(
        num_scalar_prefetch=2,
        grid=(n_experts,),
        in_specs=[
            pl.BlockSpec(memory_space=pl.ANY),
            pl.BlockSpec((None, None, d, dff), lambda e, s, c: (l, e, 0, 0)),
            pl.BlockSpec((None, None, d, dff), lambda e, s, c: (l, e, 0, 0)),
            pl.BlockSpec((None, None, dff, d), lambda e, s, c: (l, e, 0, 0)),
        ],
        out_specs=pl.BlockSpec(memory_space=pl.ANY),
        scratch_shapes=[
            pltpu.VMEM((d, dff), BF16), pltpu.VMEM((d, dff), BF16), pltpu.VMEM((dff, d), BF16),
            pltpu.VMEM((2, rb, d), F32), pltpu.VMEM((2, rb, d), F32),
            pltpu.SemaphoreType.DMA((2,)), pltpu.SemaphoreType.DMA((2,)),
        ],
    )
    return pl.pallas_call(
        _moe_kernel,
        out_shape=jax.ShapeDtypeStruct((p, d), F32),
        grid_spec=grid_spec,
        compiler_params=_params("arbitrary"),
        name="moe_experts",
    )(block_start, block_count, x_rows, w1, w3, w2)


DISPATCH_DEST1, DISPATCH_DEST2 = 0, 1
META_BLOCK_START, META_BLOCK_COUNT = 0, 1


def _dispatch_kernel(route_ref, counts_ref, dest_ref, meta_ref):
    rb = float(MOE_ROW_BLOCK)
    counts = jnp.broadcast_to(counts_ref[...], (8, LANES))
    padded = jnp.floor((counts + (rb - 1.0)) * (1.0 / rb)) * rb
    src = lax.broadcasted_iota(jnp.int32, (LANES, LANES), 0)
    dst = lax.broadcasted_iota(jnp.int32, (LANES, LANES), 1)
    earlier = (src < dst).astype(BF16)
    seg_start = sum(_dot(part, earlier) for part in _split3(padded))[0:1]

    route = route_ref[...]
    lane = lax.broadcasted_iota(jnp.int32, route.shape, 1)
    lane_f = lane.astype(F32)

    def field(slot):
        return jnp.sum(jnp.where(lane == slot, route, 0.0), axis=1, keepdims=True)

    def dest(e_slot, rank_slot):
        hit = lane_f == field(e_slot) + float(MOE_GROUPS)
        return jnp.sum(jnp.where(hit, seg_start, 0.0), axis=1, keepdims=True) + field(rank_slot)

    dest1 = dest(ROUTE_E1, ROUTE_RANK1)
    dest2 = dest(ROUTE_E2, ROUTE_RANK2)
    dest_ref[...] = jnp.where(lane == DISPATCH_DEST1, dest1, jnp.where(lane == DISPATCH_DEST2, dest2, 0.0))

    sub = lax.broadcasted_iota(jnp.int32, (8, LANES), 0)
    meta_ref[...] = jnp.where(sub == META_BLOCK_START, jnp.broadcast_to(seg_start, (8, LANES)) * (1.0 / rb),
                              jnp.where(sub == META_BLOCK_COUNT, padded * (1.0 / rb), 0.0))


def _dispatch(route, counts, *, tm=1024):
    n = route.shape[0]
    tm = min(tm, n)
    return pl.pallas_call(
        _dispatch_kernel,
        out_shape=(jax.ShapeDtypeStruct((n, LANES), F32), jax.ShapeDtypeStruct((8, LANES), F32)),
        grid=(n // tm,),
        in_specs=[pl.BlockSpec((tm, LANES), lambda i: (i, 0)), pl.BlockSpec((1, LANES), lambda i: (0, 0))],
        out_specs=(pl.BlockSpec((tm, LANES), lambda i: (i, 0)), pl.BlockSpec((8, LANES), lambda i: (0, 0))),
        compiler_params=_params("arbitrary"),
        name="moe_dispatch",
    )(route, counts)


def _dispatch_layout(route, counts):
    n = route.shape[0]
    rb = MOE_ROW_BLOCK
    dest, meta = _dispatch(route, counts)
    dest1 = dest[:, DISPATCH_DEST1].astype(jnp.int32)
    dest2 = dest[:, DISPATCH_DEST2].astype(jnp.int32)
    experts = slice(MOE_GROUPS, MOE_GROUPS + MOE_EXPERTS)
    block_start = meta[META_BLOCK_START, experts].astype(jnp.int32)
    block_count = meta[META_BLOCK_COUNT, experts].astype(jnp.int32)
    n_rows = (n * MOE_TOPK + MOE_EXPERTS * (rb - 1) + rb - 1) // rb * rb
    tok = jnp.arange(n, dtype=jnp.int32)
    row_tok = (jnp.arange(n_rows, dtype=jnp.int32) % n).at[jnp.concatenate([dest1, dest2])].set(
        jnp.concatenate([tok, tok]), mode="promise_in_bounds", unique_indices=True)
    return row_tok, block_start, block_count, dest1, dest2


def _final_norm_kernel(x_ref, g_ref, o_ref):
    o_ref[...] = _rms(x_ref[...], g_ref[...])


def _final_norm(x, g, *, tm=1024):
    n, d = x.shape
    tm = min(tm, n)
    return pl.pallas_call(
        _final_norm_kernel,
        out_shape=jax.ShapeDtypeStruct((n, d), F32),
        grid=(n // tm,),
        in_specs=[pl.BlockSpec((tm, d), lambda i: (i, 0)), pl.BlockSpec((1, d), lambda i: (0, 0))],
        out_specs=pl.BlockSpec((tm, d), lambda i: (i, 0)),
        compiler_params=_params("parallel"),
        name="final_norm",
    )(x, g)


def _rope_tables(positions):
    inv = 1.0 / (ROPE_THETA ** (jnp.arange(0, MOBA_HEAD_DIM, 2, dtype=F32) / MOBA_HEAD_DIM))
    ang = positions.astype(F32)[:, None] * inv
    cos, sin = jnp.cos(ang), jnp.sin(ang)
    return jnp.concatenate([cos, cos], axis=-1), jnp.concatenate([-sin, sin], axis=-1)


def _layer(x, mem, cos_f, sin_f, p, l):
    w_in = p["w_in"]
    w_pool_in = w_in[l, :, W_IN_POOL:]
    w_low = jnp.pad(w_in[l, :, W_IN_LOW:W_IN_POOL], ((0, 0), (0, LANES - GLA_RANK)))
    proj, h = _inproj(x, _layer_vec(p["norm_mix"], l), w_in, w_pool_in, l, cos_f, sin_f)
    o_moba = _moba(proj)
    w_dec = jnp.pad(p["w_gla_decay"][l], ((0, LANES - GLA_RANK), (0, 0))).astype(BF16)
    o_gla = _gla(proj, h, w_low, w_dec, _layer_vec(p["b_gla_decay"], l), _layer_vec(p["gla_norm"], l))
    o_pool = _pool(proj, p["w_pool"][l].astype(BF16), _layer_vec(p["pool_scale"], l))
    merged = _merge(h, o_moba, o_gla, o_pool, p["w_gate"], _layer_vec(p["b_gate"], l), p["w_branch"], l)
    x = _matmul_residual(merged, p["w_mix_out"], l, x)

    kv = _mem_kv(mem, _layer_vec(p["norm_mem"], l), p["w_xkv"], l)
    w_route = jnp.concatenate([p["w_route_group"][l], p["w_route_expert"][l]], axis=1)
    n_route = w_route.shape[1]
    w_route3 = jnp.stack(_split3(jnp.pad(w_route, ((0, 0), (0, LANES - n_route)))))
    b_route = jnp.pad(jnp.concatenate([p["b_route_group"][l], p["b_route_expert"][l]]), (0, LANES - n_route))
    x, h2, route, counts = _xattn(x, _layer_vec(p["norm_xattn"], l), p["w_xq"], kv, p["w_xo"], l,
                                  _layer_vec(p["norm_moe"], l), w_route3, b_route.reshape(1, -1))

    row_tok, block_start, block_count, dest1, dest2 = _dispatch_layout(route, counts)
    x_rows = h2.at[row_tok].get(mode="promise_in_bounds")
    y_rows = _moe_experts(block_start, block_count, x_rows, p["w_exp_gate"], p["w_exp_up"], p["w_exp_down"], l)
    y1 = y_rows.at[dest1].get(mode="promise_in_bounds")
    y2 = y_rows.at[dest2].get(mode="promise_in_bounds")
    return x + (route[:, ROUTE_W1:ROUTE_W1 + 1] * y1 + route[:, ROUTE_W2:ROUTE_W2 + 1] * y2)


def kernel(x, mem, positions, norm_mix, w_in, w_gla_decay, b_gla_decay, gla_norm, w_pool, pool_scale, w_branch, w_gate, b_gate, w_mix_out, norm_xattn, norm_mem, w_xq, w_xkv, w_xo, norm_moe, w_route_group, b_route_group, w_route_expert, b_route_expert, w_exp_gate, w_exp_up, w_exp_down, norm_final):
    batch, seq, d = x.shape
    assert batch == 1, "kernels are written for a single sequence"
    params = dict(norm_mix=norm_mix, w_in=w_in.astype(BF16), w_gla_decay=w_gla_decay, b_gla_decay=b_gla_decay,
                  gla_norm=gla_norm, w_pool=w_pool, pool_scale=pool_scale, w_branch=w_branch, w_gate=w_gate,
                  b_gate=b_gate, w_mix_out=w_mix_out, norm_xattn=norm_xattn, norm_mem=norm_mem, w_xq=w_xq,
                  w_xkv=w_xkv, w_xo=w_xo, norm_moe=norm_moe, w_route_group=w_route_group,
                  b_route_group=b_route_group, w_route_expert=w_route_expert, b_route_expert=b_route_expert,
                  w_exp_gate=w_exp_gate, w_exp_up=w_exp_up, w_exp_down=w_exp_down)
    cos_f, sin_f = _rope_tables(positions[0])
    xs = x[0]
    for l in range(norm_mix.shape[0]):
        xs = _layer(xs, mem[0], cos_f, sin_f, params, l)
    return _final_norm(xs, norm_final.reshape(1, -1)).reshape(batch, seq, d)
```

```python
import functools

import jax
import jax.numpy as jnp
from jax import lax
from jax.experimental import pallas as pl
from jax.experimental.pallas import tpu as pltpu

F32 = jnp.float32
BF16 = jnp.bfloat16

EPS = 1e-6
NEG_INF = -1e30

MOBA_HEADS = 8
MOBA_HEAD_DIM = 128
MOBA_WIDTH = MOBA_HEADS * MOBA_HEAD_DIM
MOBA_BLOCK = 256
MOBA_TOPK = 3
MOBA_KV_BLOCKS = 2
MOBA_HEADS_PER_STEP = 4
ROPE_THETA = 10000.0
MOBA_Q_SCALE = MOBA_HEAD_DIM ** -0.5 * 1.4426950408889634

GLA_HEADS = 4
GLA_DK = 128
GLA_DV = 256
GLA_K_WIDTH = GLA_HEADS * GLA_DK
GLA_V_WIDTH = GLA_HEADS * GLA_DV
GLA_RANK = 16
GLA_TAU = 16.0
GLA_CHUNK = 64

POOL_WINDOWS = (2, 4, 8, 16)
POOL_GROUP_DIM = 256
POOL_WIDTH = len(POOL_WINDOWS) * POOL_GROUP_DIM
POOL_HALO = 16

XATTN_HEADS = 4
XATTN_HEAD_DIM = 128
XATTN_WIDTH = XATTN_HEADS * XATTN_HEAD_DIM

MOE_GROUPS = 8
MOE_EXPERTS_PER_GROUP = 8
MOE_EXPERTS = MOE_GROUPS * MOE_EXPERTS_PER_GROUP
MOE_TOPK = 2
MOE_ROW_BLOCK = 128
MOE_X_SLOTS = 4
MOE_Y_SLOTS = 4

LANES = 128

COL_MQ = 0
COL_MK = COL_MQ + MOBA_WIDTH
COL_MV = COL_MK + MOBA_WIDTH
COL_GQ = COL_MV + MOBA_WIDTH
COL_GK = COL_GQ + GLA_K_WIDTH
COL_GV = COL_GK + GLA_K_WIDTH
COL_GR = COL_GV + GLA_V_WIDTH
COL_PU = COL_GR + GLA_V_WIDTH
PROJ_WIDTH = COL_PU + POOL_WIDTH
W_IN_LOW = COL_PU
W_IN_POOL = COL_PU + GLA_RANK

ROUTE_E1, ROUTE_E2, ROUTE_W1, ROUTE_W2, ROUTE_RANK1, ROUTE_RANK2 = range(6)

VMEM_LIMIT = 56 * 1024 * 1024


def _params(*semantics):
    return pltpu.CompilerParams(dimension_semantics=semantics, vmem_limit_bytes=VMEM_LIMIT)


def _rms(x, g):
    return x * lax.rsqrt(jnp.mean(x * x, axis=-1, keepdims=True) + EPS) * g


def _dot(a, b):
    return jnp.dot(a, b, preferred_element_type=F32)


def _dot_nt(a, b):
    return lax.dot_general(a, b, (((1,), (1,)), ((), ())), preferred_element_type=F32)


def _dot_tn(a, b):
    return lax.dot_general(a, b, (((0,), (0,)), ((), ())), preferred_element_type=F32)


def _split3(x):
    hi = x.astype(BF16)
    r1 = x - hi.astype(F32)
    mid = r1.astype(BF16)
    lo = (r1 - mid.astype(F32)).astype(BF16)
    return hi, mid, lo


def _layer_vec(v, l):
    return v[l].reshape(1, -1)


def _inproj_kernel(x_ref, g_ref, w_ref, wp_ref, cos_ref, sin_ref, o_ref, h_ref, *, tn, n_main):
    j = pl.program_id(1)

    @pl.when(j == 0)
    def _():
        h_ref[...] = _rms(x_ref[...], g_ref[...]).astype(BF16)

    n_rope = (2 * MOBA_WIDTH) // tn
    n_q = MOBA_WIDTH // tn

    @pl.when(j < n_rope)
    def _():
        acc = _dot(h_ref[...], w_ref[...])
        scale = jnp.where(j < n_q, MOBA_Q_SCALE, 1.0).astype(F32)
        cos = cos_ref[...] * scale
        sin = sin_ref[...] * scale
        for hh in range(tn // MOBA_HEAD_DIM):
            cols = slice(hh * MOBA_HEAD_DIM, (hh + 1) * MOBA_HEAD_DIM)
            a = acc[:, cols]
            rot = pltpu.roll(a, MOBA_HEAD_DIM // 2, axis=1)
            o_ref[:, cols] = (a * cos + rot * sin).astype(o_ref.dtype)

    @pl.when((j >= n_rope) & (j < n_main))
    def _():
        o_ref[...] = _dot(h_ref[...], w_ref[...]).astype(o_ref.dtype)

    @pl.when(j >= n_main)
    def _():
        o_ref[...] = _dot(h_ref[...], wp_ref[...]).astype(o_ref.dtype)


def _inproj(x, g, w_in, w_pool_in, l, cos_f, sin_f, *, tm=1024, tn=512):
    n, d = x.shape
    tm = min(tm, n)
    n_main = COL_PU // tn
    n_tiles = PROJ_WIDTH // tn
    return pl.pallas_call(
        functools.partial(_inproj_kernel, tn=tn, n_main=n_main),
        out_shape=(jax.ShapeDtypeStruct((n, PROJ_WIDTH), BF16), jax.ShapeDtypeStruct((n, d), BF16)),
        grid=(n // tm, n_tiles),
        in_specs=[
            pl.BlockSpec((tm, d), lambda i, j: (i, 0)),
            pl.BlockSpec((1, d), lambda i, j: (0, 0)),
            pl.BlockSpec((None, d, tn), lambda i, j: (l, 0, jnp.minimum(j, n_main - 1))),
            pl.BlockSpec((d, tn), lambda i, j: (0, jnp.maximum(j - n_main, 0))),
            pl.BlockSpec((tm, MOBA_HEAD_DIM), lambda i, j: (i, 0)),
            pl.BlockSpec((tm, MOBA_HEAD_DIM), lambda i, j: (i, 0)),
        ],
        out_specs=(
            pl.BlockSpec((tm, tn), lambda i, j: (i, j)),
            pl.BlockSpec((tm, d), lambda i, j: (i, 0)),
        ),
        compiler_params=_params("parallel", "arbitrary"),
        name="inproj",
    )(x, g, w_in, w_pool_in, cos_f, sin_f)


def _moba_kernel(q_ref, k_ref, v_ref, o_ref, kmean_ref, *, n_blocks):
    qi = pl.program_id(1)
    blk = MOBA_BLOCK
    hd = MOBA_HEAD_DIM
    tile = MOBA_KV_BLOCKS * blk
    blk_shift = blk.bit_length() - 1
    head_cols = [slice(hh * hd, (hh + 1) * hd) for hh in range(MOBA_HEADS_PER_STEP)]

    @pl.when(qi == 0)
    def _():
        kmean_ref[...] = jnp.zeros_like(kmean_ref)

        def fill(b, carry):
            kb = k_ref[pl.ds(pl.multiple_of(b * blk, blk), blk), :].astype(F32)
            kmean_ref[pl.ds(b, 1), :] = jnp.mean(kb, axis=0, keepdims=True)
            return carry

        lax.fori_loop(0, n_blocks, fill, 0)

    start = pl.multiple_of(qi * blk, blk)
    lane = lax.broadcasted_iota(jnp.int32, (blk, LANES), 1)
    lane_f = lane.astype(F32)
    row = lax.broadcasted_iota(jnp.int32, (blk, blk), 0)
    col = lax.broadcasted_iota(jnp.int32, (blk, blk), 1)
    ones_own = jnp.ones((blk, hd), BF16)
    ones_tile = jnp.ones((tile, hd), BF16)

    q_augs, carry = [], []
    for cols in head_cols:
        q = q_ref[:, cols]
        kmean_hi, kmean_mid, _ = _split3(kmean_ref[:, cols])
        gate = _dot_nt(q, kmean_hi) + _dot_nt(q, kmean_mid)
        gate = jnp.where(lane < qi, gate, NEG_INF)
        sel = jnp.zeros(gate.shape, jnp.bool_)
        for _ in range(MOBA_TOPK):
            top = jnp.max(gate, axis=1, keepdims=True)
            idx = jnp.min(jnp.where(gate == top, lane_f, float(LANES)), axis=1, keepdims=True)
            hit = lane_f == idx
            sel = sel | (hit & (top > 0.5 * NEG_INF))
            gate = jnp.where(hit, NEG_INF, gate)
        q_augs.append(jnp.concatenate([q, jnp.where(sel, 0.0, NEG_INF).astype(BF16)], axis=1))

        s = _dot_nt(q, k_ref[pl.ds(start, blk), cols])
        s = jnp.where(col <= row, s, NEG_INF)
        m = jnp.max(s, axis=1, keepdims=True)
        p = jnp.exp2(s - m).astype(BF16)
        acc = _dot(p, jnp.concatenate([v_ref[pl.ds(start, blk), cols], ones_own], axis=1))
        carry += [m, acc]

    tile_row = lax.broadcasted_iota(jnp.int32, (tile, hd), 0)
    tile_lane = lax.broadcasted_iota(jnp.int32, (tile, hd), 1)

    def body(t, carry):
        off = pl.multiple_of(t * tile, tile)
        block_hot = (tile_lane == t * MOBA_KV_BLOCKS + (tile_row >> blk_shift)).astype(BF16)
        out = []
        for hh, cols in enumerate(head_cols):
            m, acc = carry[2 * hh], carry[2 * hh + 1]
            k_aug = jnp.concatenate([k_ref[pl.ds(off, tile), cols], block_hot], axis=1)
            s = _dot_nt(q_augs[hh], k_aug)
            m_new = jnp.maximum(m, jnp.max(s, axis=1, keepdims=True))
            alpha = jnp.exp2(m - m_new)
            p = jnp.exp2(s - m_new).astype(BF16)
            v_aug = jnp.concatenate([v_ref[pl.ds(off, tile), cols], ones_tile], axis=1)
            out += [m_new, alpha * acc + _dot(p, v_aug)]
        return tuple(out)

    n_tiles = (qi + MOBA_KV_BLOCKS - 1) // MOBA_KV_BLOCKS
    carry = lax.fori_loop(0, n_tiles, body, tuple(carry))
    for hh, cols in enumerate(head_cols):
        acc = carry[2 * hh + 1]
        o_ref[:, cols] = (acc[:, :hd] / acc[:, hd:]).astype(o_ref.dtype)


def _moba(proj):
    n = proj.shape[0]
    n_blocks = n // MOBA_BLOCK
    assert n_blocks % MOBA_KV_BLOCKS == 0 and n_blocks <= LANES
    width = MOBA_HEADS_PER_STEP * MOBA_HEAD_DIM
    return pl.pallas_call(
        functools.partial(_moba_kernel, n_blocks=n_blocks),
        out_shape=jax.ShapeDtypeStruct((n, MOBA_WIDTH), BF16),
        grid=(MOBA_HEADS // MOBA_HEADS_PER_STEP, n_blocks),
        in_specs=[
            pl.BlockSpec((MOBA_BLOCK, width), lambda h, i: (i, COL_MQ // width + h)),
            pl.BlockSpec((n, width), lambda h, i: (0, COL_MK // width + h), pipeline_mode=pl.Buffered(1)),
            pl.BlockSpec((n, width), lambda h, i: (0, COL_MV // width + h), pipeline_mode=pl.Buffered(1)),
        ],
        out_specs=pl.BlockSpec((MOBA_BLOCK, width), lambda h, i: (i, h)),
        scratch_shapes=[pltpu.VMEM((LANES, width), F32)],
        compiler_params=_params("parallel", "arbitrary"),
        name="moba",
    )(proj, proj, proj)


def _gla_kernel(q_ref, k_ref, v_ref, r_ref, h_ref, wlow_ref, wdec_ref, bdec_ref, gn_ref, o_ref, state_ref, *, tb):
    i = pl.program_id(0)
    c = GLA_CHUNK
    n_chunks = tb // c
    chunk_shift = c.bit_length() - 1

    @pl.when(i == 0)
    def _():
        state_ref[...] = jnp.zeros_like(state_ref)

    g_low = _dot(h_ref[...], wlow_ref[...]).astype(BF16)
    z = _dot(g_low, wdec_ref[...]) + bdec_ref[...]
    log_a = jax.nn.log_sigmoid(z) / GLA_TAU

    row = lax.broadcasted_iota(jnp.int32, (tb, tb), 0)
    col = lax.broadcasted_iota(jnp.int32, (tb, tb), 1)
    same_chunk = (row >> chunk_shift) == (col >> chunk_shift)
    causal = same_chunk & (col <= row)
    tri = causal.astype(BF16)
    ones = same_chunk.astype(BF16)
    parts = _split3(log_a)
    b = sum(_dot(tri, part) for part in parts)
    b_last = sum(_dot(ones, part) for part in parts)

    q_dec = (q_ref[...].astype(F32) * (GLA_DK ** -0.5) * jnp.exp(b)).astype(BF16)
    kf = k_ref[...].astype(F32)
    k_inv = (kf * jnp.exp(-b)).astype(BF16)
    k_end = (kf * jnp.exp(b_last - b)).astype(BF16)
    decay = jnp.exp(b_last)

    for h in range(GLA_HEADS):
        kc = slice(h * GLA_DK, (h + 1) * GLA_DK)
        vc = slice(h * GLA_DV, (h + 1) * GLA_DV)
        v = v_ref[:, vc]
        attn = jnp.where(causal, _dot_nt(q_dec[:, kc], k_inv[:, kc]), 0.0)
        o = _dot(attn.astype(BF16), v)
        state = state_ref[h]
        inter = []
        for ci in range(n_chunks):
            rows = slice(ci * c, (ci + 1) * c)
            inter.append(_dot_nt(q_dec[rows, kc], state.astype(BF16)))
            update = _dot_tn(v[rows, :], k_end[rows, kc])
            state = decay[ci * c:ci * c + 1, kc] * state + update
        state_ref[h] = state
        o = o + jnp.concatenate(inter, axis=0)
        o = _rms(o, gn_ref[...])
        o_ref[:, vc] = (o * jax.nn.silu(r_ref[:, vc].astype(F32))).astype(o_ref.dtype)


def _gla(proj, h, w_low, w_dec, b_dec, gla_norm, *, tb=512):
    n, d = h.shape
    tb = min(tb, n)
    return pl.pallas_call(
        functools.partial(_gla_kernel, tb=tb),
        out_shape=jax.ShapeDtypeStruct((n, GLA_V_WIDTH), BF16),
        grid=(n // tb,),
        in_specs=[
            pl.BlockSpec((tb, GLA_K_WIDTH), lambda i: (i, COL_GQ // GLA_K_WIDTH)),
            pl.BlockSpec((tb, GLA_K_WIDTH), lambda i: (i, COL_GK // GLA_K_WIDTH)),
            pl.BlockSpec((tb, GLA_V_WIDTH), lambda i: (i, COL_GV // GLA_V_WIDTH)),
            pl.BlockSpec((tb, GLA_V_WIDTH), lambda i: (i, COL_GR // GLA_V_WIDTH)),
            pl.BlockSpec((tb, d), lambda i: (i, 0)),
            pl.BlockSpec((d, LANES), lambda i: (0, 0)),
            pl.BlockSpec((LANES, GLA_K_WIDTH), lambda i: (0, 0)),
            pl.BlockSpec((1, GLA_K_WIDTH), lambda i: (0, 0)),
            pl.BlockSpec((1, GLA_DV), lambda i: (0, 0)),
        ],
        out_specs=pl.BlockSpec((tb, GLA_V_WIDTH), lambda i: (i, 0)),
        scratch_shapes=[pltpu.VMEM((GLA_HEADS, GLA_DV, GLA_DK), F32)],
        compiler_params=_params("arbitrary"),
        name="gla",
    )(proj, proj, proj, proj, h, w_low, w_dec, b_dec, gla_norm)


def _pool_kernel(u_ref, halo_ref, w_ref, sc_ref, o_ref, ext_ref, *, tb):
    i = pl.program_id(0)
    ext_ref[pl.ds(POOL_HALO, tb), :] = u_ref[...].astype(F32)
    ext_ref[pl.ds(0, POOL_HALO), :] = jnp.where(i == 0, 0.0, halo_ref[...].astype(F32))
    t = i * tb + lax.broadcasted_iota(jnp.int32, (tb, 1), 0)
    for g, win in enumerate(POOL_WINDOWS):
        cols = slice(g * POOL_GROUP_DIM, (g + 1) * POOL_GROUP_DIM)
        u = ext_ref[pl.ds(POOL_HALO, tb), cols]
        window_sum = u
        for back in range(1, win):
            window_sum = window_sum + ext_ref[pl.ds(POOL_HALO - back, tb), cols]
        count = jnp.minimum(t + 1, win).astype(F32)
        mixed = window_sum / count - u
        y = _dot(mixed.astype(BF16), w_ref[g])
        o_ref[:, cols] = (y * sc_ref[:, cols]).astype(o_ref.dtype)


def _pool(proj, w_pool, pool_scale, *, tb=512):
    n = proj.shape[0]
    tb = min(tb, n)
    halo_per_block = tb // POOL_HALO
    return pl.pallas_call(
        functools.partial(_pool_kernel, tb=tb),
        out_shape=jax.ShapeDtypeStruct((n, POOL_WIDTH), BF16),
        grid=(n // tb,),
        in_specs=[
            pl.BlockSpec((tb, POOL_WIDTH), lambda i: (i, COL_PU // POOL_WIDTH)),
            pl.BlockSpec((POOL_HALO, POOL_WIDTH),
                         lambda i: (jnp.maximum(i * halo_per_block - 1, 0), COL_PU // POOL_WIDTH)),
            pl.BlockSpec(w_pool.shape, lambda i: (0, 0, 0)),
            pl.BlockSpec((1, POOL_WIDTH), lambda i: (0, 0)),
        ],
        out_specs=pl.BlockSpec((tb, POOL_WIDTH), lambda i: (i, 0)),
        scratch_shapes=[pltpu.VMEM((POOL_HALO + tb, POOL_WIDTH), F32)],
        compiler_params=_params("parallel"),
        name="pool",
    )(proj, proj, w_pool, pool_scale)


def _merge_kernel(h_ref, a_ref, b_ref, c_ref, wg0, wg1, wg2, bg0, bg1, bg2, wb0, wb1, wb2, o_ref):
    h = h_ref[...]
    total = None
    for br_ref, wg, bg, wb in ((a_ref, wg0, bg0, wb0), (b_ref, wg1, bg1, wb1), (c_ref, wg2, bg2, wb2)):
        gate = jax.nn.sigmoid(_dot(h, wg[...].astype(BF16)) + bg[...])
        term = gate * _dot(br_ref[...], wb[...].astype(BF16))
        total = term if total is None else total + term
    o_ref[...] = total.astype(o_ref.dtype)


def _merge(h, o_moba, o_gla, o_pool, w_gate, b_gate, w_branch, l, *, tm=1024, tn=256):
    n, d = h.shape
    tm = min(tm, n)
    bw = o_moba.shape[1]
    tiles = d // tn
    row_spec = lambda width: pl.BlockSpec((tm, width), lambda i, j: (i, 0))
    gate_specs = [pl.BlockSpec((None, d, tn), lambda i, j, k=k: (l, 0, k * tiles + j)) for k in range(3)]
    bias_specs = [pl.BlockSpec((1, tn), lambda i, j, k=k: (0, k * tiles + j)) for k in range(3)]
    branch_specs = [pl.BlockSpec((None, None, bw, tn), lambda i, j, k=k: (l, k, 0, j)) for k in range(3)]
    return pl.pallas_call(
        _merge_kernel,
        out_shape=jax.ShapeDtypeStruct((n, d), BF16),
        grid=(n // tm, tiles),
        in_specs=[row_spec(d), row_spec(bw), row_spec(bw), row_spec(bw)] + gate_specs + bias_specs + branch_specs,
        out_specs=pl.BlockSpec((tm, tn), lambda i, j: (i, j)),
        compiler_params=_params("parallel", "arbitrary"),
        name="merge",
    )(h, o_moba, o_gla, o_pool, w_gate, w_gate, w_gate, b_gate, b_gate, b_gate, w_branch, w_branch, w_branch)


def _matmul_residual_kernel(a_ref, w_ref, x_ref, o_ref):
    o_ref[...] = x_ref[...] + _dot(a_ref[...], w_ref[...].astype(BF16))


def _matmul_residual(a, w, l, x, *, tm=1024, tn=512):
    n, k = a.shape
    d = w.shape[-1]
    tm = min(tm, n)
    return pl.pallas_call(
        _matmul_residual_kernel,
        out_shape=jax.ShapeDtypeStruct((n, d), F32),
        grid=(n // tm, d // tn),
        in_specs=[
            pl.BlockSpec((tm, k), lambda i, j: (i, 0)),
            pl.BlockSpec((None, k, tn), lambda i, j: (l, 0, j)),
            pl.BlockSpec((tm, tn), lambda i, j: (i, j)),
        ],
        out_specs=pl.BlockSpec((tm, tn), lambda i, j: (i, j)),
        compiler_params=_params("parallel", "arbitrary"),
        name="mix_out",
    )(a, w, x)


def _norm_matmul_kernel(x_ref, g_ref, w_ref, o_ref):
    o_ref[...] = _dot(_rms(x_ref[...], g_ref[...]).astype(BF16), w_ref[...].astype(BF16)).astype(o_ref.dtype)


def _mem_kv(mem, g, w_xkv, l):
    n, d = mem.shape
    width = w_xkv.shape[-1]
    return pl.pallas_call(
        _norm_matmul_kernel,
        out_shape=jax.ShapeDtypeStruct((n, width), BF16),
        grid=(1,),
        in_specs=[
            pl.BlockSpec((n, d), lambda i: (0, 0)),
            pl.BlockSpec((1, d), lambda i: (0, 0)),
            pl.BlockSpec((None, d, width), lambda i: (l, 0, 0)),
        ],
        out_specs=pl.BlockSpec((n, width), lambda i: (0, 0)),
        compiler_params=_params("arbitrary"),
        name="mem_kv",
    )(mem, g, w_xkv)


def _xattn_kernel(x_ref, g_ref, wq_ref, kv_ref, wo_ref, gm_ref, wr_ref, br_ref,
                  xo_ref, h2_ref, route_ref, counts_ref, wq_b, wo_b, counts_acc, *, tm):
    i = pl.program_id(0)

    @pl.when(i == 0)
    def _():
        wq_b[...] = wq_ref[...].astype(BF16)
        wo_b[...] = wo_ref[...].astype(BF16)
        counts_acc[...] = jnp.zeros_like(counts_acc)

    x = x_ref[...]
    h = _rms(x, g_ref[...]).astype(BF16)
    q = (_dot(h, wq_b[...]) * (XATTN_HEAD_DIM ** -0.5)).astype(BF16)
    heads = []
    for hd in range(XATTN_HEADS):
        kc = slice(hd * XATTN_HEAD_DIM, (hd + 1) * XATTN_HEAD_DIM)
        vc = slice(XATTN_WIDTH + hd * XATTN_HEAD_DIM, XATTN_WIDTH + (hd + 1) * XATTN_HEAD_DIM)
        s = _dot_nt(q[:, kc], kv_ref[:, kc])
        p = jnp.exp(s - jnp.max(s, axis=1, keepdims=True))
        p = p / jnp.sum(p, axis=1, keepdims=True)
        heads.append(_dot(p.astype(BF16), kv_ref[:, vc]).astype(BF16))
    o = jnp.concatenate(heads, axis=1)
    xn = x + _dot(o, wo_b[...])
    xo_ref[...] = xn
    h2 = _rms(xn, gm_ref[...])
    h2_ref[...] = h2

    h_hi, h_mid, h_lo = _split3(h2)
    w_hi, w_mid, w_lo = wr_ref[0], wr_ref[1], wr_ref[2]
    lg = (_dot(h_hi, w_hi) + (_dot(h_hi, w_mid) + _dot(h_mid, w_hi))
          + (_dot(h_hi, w_lo) + _dot(h_mid, w_mid) + _dot(h_lo, w_hi))) + br_ref[...]

    lane = lax.broadcasted_iota(jnp.int32, lg.shape, 1)
    lane_f = lane.astype(F32)
    big = float(LANES)

    def top1(v):
        top = jnp.max(v, axis=1, keepdims=True)
        return top, jnp.min(jnp.where(v == top, lane_f, big), axis=1, keepdims=True)

    g_logit = jnp.where(lane < MOE_GROUPS, lg, NEG_INF)
    g_top, g_idx = top1(g_logit)
    g_w = 1.0 / jnp.sum(jnp.exp(g_logit - g_top), axis=1, keepdims=True)
    first = MOE_GROUPS + MOE_EXPERTS_PER_GROUP * g_idx
    e_logit = jnp.where((lane_f >= first) & (lane_f < first + MOE_EXPERTS_PER_GROUP), lg, NEG_INF)
    e_top1, lane1 = top1(e_logit)
    e_top2, lane2 = top1(jnp.where(lane_f == lane1, NEG_INF, e_logit))
    z = jnp.sum(jnp.exp(e_logit - e_top1), axis=1, keepdims=True)
    p1 = 1.0 / z
    p2 = jnp.exp(e_top2 - e_top1) / z
    w1 = g_w * p1 / (p1 + p2)
    w2 = g_w * p2 / (p1 + p2)

    two_hot = (lane_f == lane1) | (lane_f == lane2)
    row = lax.broadcasted_iota(jnp.int32, (tm, tm), 0)
    col = lax.broadcasted_iota(jnp.int32, (tm, tm), 1)
    before = _dot((col < row).astype(BF16), two_hot.astype(BF16)) + counts_acc[...]
    rank1 = jnp.sum(jnp.where(lane_f == lane1, before, 0.0), axis=1, keepdims=True)
    rank2 = jnp.sum(jnp.where(lane_f == lane2, before, 0.0), axis=1, keepdims=True)
    counts_acc[...] += jnp.sum(two_hot.astype(F32), axis=0, keepdims=True)
    counts_ref[...] = counts_acc[...]

    record = jnp.zeros(lg.shape, F32)
    for slot, val in ((ROUTE_E1, lane1 - MOE_GROUPS), (ROUTE_E2, lane2 - MOE_GROUPS), (ROUTE_W1, w1),
                      (ROUTE_W2, w2), (ROUTE_RANK1, rank1), (ROUTE_RANK2, rank2)):
        record = jnp.where(lane == slot, val, record)
    route_ref[...] = record


def _xattn(x, g, wq, kv, wo, l, g_moe, w_route3, b_route, *, tm=512):
    n, d = x.shape
    tm = min(tm, n)
    full = lambda a: pl.BlockSpec(a.shape, lambda i: (0,) * a.ndim)
    layer = lambda a: pl.BlockSpec((None,) + a.shape[1:], lambda i: (l,) + (0,) * (a.ndim - 1))
    rows = lambda width: pl.BlockSpec((tm, width), lambda i: (i, 0))
    return pl.pallas_call(
        functools.partial(_xattn_kernel, tm=tm),
        out_shape=(jax.ShapeDtypeStruct((n, d), F32), jax.ShapeDtypeStruct((n, d), F32),
                   jax.ShapeDtypeStruct((n, LANES), F32), jax.ShapeDtypeStruct((1, LANES), F32)),
        grid=(n // tm,),
        in_specs=[rows(d), full(g), layer(wq), full(kv), layer(wo), full(g_moe), full(w_route3), full(b_route)],
        out_specs=(rows(d), rows(d), rows(LANES), pl.BlockSpec((1, LANES), lambda i: (0, 0))),
        scratch_shapes=[pltpu.VMEM(wq.shape[1:], BF16), pltpu.VMEM(wo.shape[1:], BF16),
                        pltpu.VMEM((1, LANES), F32)],
        compiler_params=_params("arbitrary"),
        name="xattn_route",
    )(x, g, wq, kv, wo, g_moe, w_route3, b_route)


def _moe_kernel(start_ref, count_ref, x_hbm, w1_hbm, w3_hbm, w2_hbm, y_hbm,
                w1f, w3f, w2f, w1b, w3b, w2b, xbuf, ybuf, wsem, xsem, ysem, *, layer):
    e = pl.program_id(0)
    n_experts = pl.num_programs(0)
    rb = MOE_ROW_BLOCK
    par = e % 2
    first = start_ref[e]
    n_blocks = count_ref[e]

    def rows(block):
        return pl.ds(pl.multiple_of(block * rb, rb), rb)

    def w_copies(expert, slot):
        return (pltpu.make_async_copy(w1_hbm.at[layer, expert], w1f.at[slot], wsem.at[0, slot]),
                pltpu.make_async_copy(w3_hbm.at[layer, expert], w3f.at[slot], wsem.at[1, slot]),
                pltpu.make_async_copy(w2_hbm.at[layer, expert], w2f.at[slot], wsem.at[2, slot]))

    def x_copy(block, slot, xs):
        return pltpu.make_async_copy(x_hbm.at[rows(block), :], xbuf.at[slot, xs], xsem.at[slot, xs])

    def y_copy(block, ys):
        return pltpu.make_async_copy(ybuf.at[ys], y_hbm.at[rows(block), :], ysem.at[ys])

    def prefetch(expert, slot):
        for xs in range(MOE_X_SLOTS):
            @pl.when(xs < count_ref[expert])
            def _():
                x_copy(start_ref[expert] + xs, slot, xs).start()
        for copy in w_copies(expert, slot):
            copy.start()

    @pl.when(e == 0)
    def _():
        prefetch(0, 0)

    @pl.when(e + 1 < n_experts)
    def _():
        prefetch(e + 1, 1 - par)

    for copy in w_copies(e, par):
        copy.wait()
    w1b[...] = w1f[par].astype(BF16)
    w3b[...] = w3f[par].astype(BF16)
    w2b[...] = w2f[par].astype(BF16)

    def body(c, carry):
        xs = c % MOE_X_SLOTS
        ys = c % MOE_Y_SLOTS

        @pl.when(c < MOE_X_SLOTS)
        def _():
            x_copy(first + c, par, xs).wait()

        @pl.when(c >= MOE_X_SLOTS)
        def _():
            copy = x_copy(first + c, par, xs)
            copy.start()
            copy.wait()

        @pl.when(c >= MOE_Y_SLOTS)
        def _():
            y_copy(first + c - MOE_Y_SLOTS, ys).wait()

        x = xbuf[par, xs].astype(BF16)
        hidden = jax.nn.silu(_dot(x, w1b[...])) * _dot(x, w3b[...])
        ybuf[ys] = _dot(hidden.astype(BF16), w2b[...])
        y_copy(first + c, ys).start()
        return carry

    lax.fori_loop(0, n_blocks, body, 0)

    for back in range(1, MOE_Y_SLOTS + 1):
        @pl.when(n_blocks >= back)
        def _():
            y_copy(first + n_blocks - back, (n_blocks - back) % MOE_Y_SLOTS).wait()

    @pl.when(e == n_experts - 1)
    def _():
        ybuf[0] = jnp.zeros(ybuf.shape[1:], ybuf.dtype)
        used = first + n_blocks
        total = y_hbm.shape[0] // rb

        def start(block, carry):
            y_copy(block, 0).start()
            return carry

        def wait(block, carry):
            y_copy(block, 0).wait()
            return carry

        lax.fori_loop(used, total, start, 0)
        lax.fori_loop(used, total, wait, 0)


def _moe_experts(block_start, block_count, x_rows, w1, w3, w2, l):
    p, d = x_rows.shape
    rb = MOE_ROW_BLOCK
    n_experts, dff = w1.shape[1], w1.shape[-1]
    hbm = pl.BlockSpec(memory_space=pl.ANY)
    grid_spec = pltpu.PrefetchScalarGridSpec(
        num_scalar_prefetch=2,
        grid=(n_experts,),
        in_specs=[hbm, hbm, hbm, hbm],
        out_specs=hbm,
        scratch_shapes=[
            pltpu.VMEM((2, d, dff), F32), pltpu.VMEM((2, d, dff), F32), pltpu.VMEM((2, dff, d), F32),
            pltpu.VMEM((d, dff), BF16), pltpu.VMEM((d, dff), BF16), pltpu.VMEM((dff, d), BF16),
            pltpu.VMEM((2, MOE_X_SLOTS, rb, d), F32), pltpu.VMEM((MOE_Y_SLOTS, rb, d), F32),
            pltpu.SemaphoreType.DMA((3, 2)), pltpu.SemaphoreType.DMA((2, MOE_X_SLOTS)),
            pltpu.SemaphoreType.DMA((MOE_Y_SLOTS,)),
        ],
    )
    return pl.pallas_call(
        functools.partial(_moe_kernel, layer=l),
        out_shape=jax.ShapeDtypeStruct((p, d), F32),
        grid_spec=grid_spec,
        compiler_params=_params("arbitrary"),
        name="moe_experts",
    )(block_start, block_count, x_rows, w1, w3, w2)


DISPATCH_DEST1, DISPATCH_DEST2 = 0, 1
META_BLOCK_START, META_BLOCK_COUNT = 0, 1


def _dispatch_kernel(route_ref, counts_ref, dest_ref, meta_ref):
    rb = float(MOE_ROW_BLOCK)
    counts = jnp.broadcast_to(counts_ref[...], (8, LANES))
    padded = jnp.floor((counts + (rb - 1.0)) * (1.0 / rb)) * rb
    src = lax.broadcasted_iota(jnp.int32, (LANES, LANES), 0)
    dst = lax.broadcasted_iota(jnp.int32, (LANES, LANES), 1)
    earlier = (src < dst).astype(BF16)
    seg_start = sum(_dot(part, earlier) for part in _split3(padded))[0:1]

    route = route_ref[...]
    lane = lax.broadcasted_iota(jnp.int32, route.shape, 1)
    lane_f = lane.astype(F32)

    def field(slot):
        return jnp.sum(jnp.where(lane == slot, route, 0.0), axis=1, keepdims=True)

    def dest(e_slot, rank_slot):
        hit = lane_f == field(e_slot) + float(MOE_GROUPS)
        return jnp.sum(jnp.where(hit, seg_start, 0.0), axis=1, keepdims=True) + field(rank_slot)

    dest1 = dest(ROUTE_E1, ROUTE_RANK1)
    dest2 = dest(ROUTE_E2, ROUTE_RANK2)
    dest_ref[...] = jnp.where(lane == DISPATCH_DEST1, dest1, jnp.where(lane == DISPATCH_DEST2, dest2, 0.0))

    sub = lax.broadcasted_iota(jnp.int32, (8, LANES), 0)
    meta_ref[...] = jnp.where(sub == META_BLOCK_START, jnp.broadcast_to(seg_start, (8, LANES)) * (1.0 / rb),
                              jnp.where(sub == META_BLOCK_COUNT, padded * (1.0 / rb), 0.0))


def _dispatch(route, counts, *, tm=1024):
    n = route.shape[0]
    tm = min(tm, n)
    return pl.pallas_call(
        _dispatch_kernel,
        out_shape=(jax.ShapeDtypeStruct((n, LANES), F32), jax.ShapeDtypeStruct((8, LANES), F32)),
        grid=(n // tm,),
        in_specs=[pl.BlockSpec((tm, LANES), lambda i: (i, 0)), pl.BlockSpec((1, LANES), lambda i: (0, 0))],
        out_specs=(pl.BlockSpec((tm, LANES), lambda i: (i, 0)), pl.BlockSpec((8, LANES), lambda i: (0, 0))),
        compiler_params=_params("arbitrary"),
        name="moe_dispatch",
    )(route, counts)


def _dispatch_layout(route, counts):
    n = route.shape[0]
    rb = MOE_ROW_BLOCK
    dest, meta = _dispatch(route, counts)
    dest1 = dest[:, DISPATCH_DEST1].astype(jnp.int32)
    dest2 = dest[:, DISPATCH_DEST2].astype(jnp.int32)
    experts = slice(MOE_GROUPS, MOE_GROUPS + MOE_EXPERTS)
    block_start = meta[META_BLOCK_START, experts].astype(jnp.int32)
    block_count = meta[META_BLOCK_COUNT, experts].astype(jnp.int32)
    n_rows = (n * MOE_TOPK + MOE_EXPERTS * (rb - 1) + rb - 1) // rb * rb
    tok = jnp.arange(n, dtype=jnp.int32)
    row_tok = (jnp.arange(n_rows, dtype=jnp.int32) % n).at[jnp.concatenate([dest1, dest2])].set(
        jnp.concatenate([tok, tok]), mode="promise_in_bounds", unique_indices=True)
    return row_tok, block_start, block_count, dest1, dest2


def _final_norm_kernel(x_ref, g_ref, o_ref):
    o_ref[...] = _rms(x_ref[...], g_ref[...])


def _final_norm(x, g, *, tm=1024):
    n, d = x.shape
    tm = min(tm, n)
    return pl.pallas_call(
        _final_norm_kernel,
        out_shape=jax.ShapeDtypeStruct((n, d), F32),
        grid=(n // tm,),
        in_specs=[pl.BlockSpec((tm, d), lambda i: (i, 0)), pl.BlockSpec((1, d), lambda i: (0, 0))],
        out_specs=pl.BlockSpec((tm, d), lambda i: (i, 0)),
        compiler_params=_params("parallel"),
        name="final_norm",
    )(x, g)


def _rope_tables(positions):
    inv = 1.0 / (ROPE_THETA ** (jnp.arange(0, MOBA_HEAD_DIM, 2, dtype=F32) / MOBA_HEAD_DIM))
    ang = positions.astype(F32)[:, None] * inv
    cos, sin = jnp.cos(ang), jnp.sin(ang)
    return jnp.concatenate([cos, cos], axis=-1), jnp.concatenate([-sin, sin], axis=-1)


def _layer(x, mem, cos_f, sin_f, p, l):
    w_in = p["w_in"]
    w_pool_in = w_in[l, :, W_IN_POOL:]
    w_low = jnp.pad(w_in[l, :, W_IN_LOW:W_IN_POOL], ((0, 0), (0, LANES - GLA_RANK)))
    proj, h = _inproj(x, _layer_vec(p["norm_mix"], l), w_in, w_pool_in, l, cos_f, sin_f)
    o_moba = _moba(proj)
    w_dec = jnp.pad(p["w_gla_decay"][l], ((0, LANES - GLA_RANK), (0, 0))).astype(BF16)
    o_gla = _gla(proj, h, w_low, w_dec, _layer_vec(p["b_gla_decay"], l), _layer_vec(p["gla_norm"], l))
    o_pool = _pool(proj, p["w_pool"][l].astype(BF16), _layer_vec(p["pool_scale"], l))
    merged = _merge(h, o_moba, o_gla, o_pool, p["w_gate"], _layer_vec(p["b_gate"], l), p["w_branch"], l)
    x = _matmul_residual(merged, p["w_mix_out"], l, x)

    kv = _mem_kv(mem, _layer_vec(p["norm_mem"], l), p["w_xkv"], l)
    w_route = jnp.concatenate([p["w_route_group"][l], p["w_route_expert"][l]], axis=1)
    n_route = w_route.shape[1]
    w_route3 = jnp.stack(_split3(jnp.pad(w_route, ((0, 0), (0, LANES - n_route)))))
    b_route = jnp.pad(jnp.concatenate([p["b_route_group"][l], p["b_route_expert"][l]]), (0, LANES - n_route))
    x, h2, route, counts = _xattn(x, _layer_vec(p["norm_xattn"], l), p["w_xq"], kv, p["w_xo"], l,
                                  _layer_vec(p["norm_moe"], l), w_route3, b_route.reshape(1, -1))

    row_tok, block_start, block_count, dest1, dest2 = _dispatch_layout(route, counts)
    x_rows = h2.at[row_tok].get(mode="promise_in_bounds")
    y_rows = _moe_experts(block_start, block_count, x_rows, p["w_exp_gate"], p["w_exp_up"], p["w_exp_down"], l)
    y1 = y_rows.at[dest1].get(mode="promise_in_bounds")
    y2 = y_rows.at[dest2].get(mode="promise_in_bounds")
    return x + (route[:, ROUTE_W1:ROUTE_W1 + 1] * y1 + route[:, ROUTE_W2:ROUTE_W2 + 1] * y2)


def kernel(x, mem, positions, norm_mix, w_in, w_gla_decay, b_gla_decay, gla_norm, w_pool, pool_scale, w_branch, w_gate, b_gate, w_mix_out, norm_xattn, norm_mem, w_xq, w_xkv, w_xo, norm_moe, w_route_group, b_route_group, w_route_expert, b_route_expert, w_exp_gate, w_exp_up, w_exp_down, norm_final):
    batch, seq, d = x.shape
    assert batch == 1, "kernels are written for a single sequence"
    params = dict(norm_mix=norm_mix, w_in=w_in.astype(BF16), w_gla_decay=w_gla_decay, b_gla_decay=b_gla_decay,
                  gla_norm=gla_norm, w_pool=w_pool, pool_scale=pool_scale, w_branch=w_branch, w_gate=w_gate,
                  b_gate=b_gate, w_mix_out=w_mix_out, norm_xattn=norm_xattn, norm_mem=norm_mem, w_xq=w_xq,
                  w_xkv=w_xkv, w_xo=w_xo, norm_moe=norm_moe, w_route_group=w_route_group,
                  b_route_group=b_route_group, w_route_expert=w_route_expert, b_route_expert=b_route_expert,
                  w_exp_gate=w_exp_gate, w_exp_up=w_exp_up, w_exp_down=w_exp_down)
    cos_f, sin_f = _rope_tables(positions[0])
    xs = x[0]
    for l in range(norm_mix.shape[0]):
        xs = _layer(xs, mem[0], cos_f, sin_f, params, l)
    return _final_norm(xs, norm_final.reshape(1, -1)).reshape(batch, seq, d)
```

```python
import functools

import jax
import jax.numpy as jnp
from jax import lax
from jax.experimental import pallas as pl
from jax.experimental.pallas import tpu as pltpu

F32 = jnp.float32
BF16 = jnp.bfloat16

EPS = 1e-6
NEG_INF = -1e30

MOBA_HEADS = 8
MOBA_HEAD_DIM = 128
MOBA_WIDTH = MOBA_HEADS * MOBA_HEAD_DIM
MOBA_BLOCK = 256
MOBA_TOPK = 3
MOBA_KV_BLOCKS = 4
MOBA_HEADS_PER_STEP = 4
ROPE_THETA = 10000.0
MOBA_Q_SCALE = MOBA_HEAD_DIM ** -0.5 * 1.4426950408889634

GLA_HEADS = 4
GLA_DK = 128
GLA_DV = 256
GLA_K_WIDTH = GLA_HEADS * GLA_DK
GLA_V_WIDTH = GLA_HEADS * GLA_DV
GLA_RANK = 16
GLA_TAU = 16.0
GLA_CHUNK = 64

POOL_WINDOWS = (2, 4, 8, 16)
POOL_GROUP_DIM = 256
POOL_WIDTH = len(POOL_WINDOWS) * POOL_GROUP_DIM
POOL_HALO = 16

XATTN_HEADS = 4
XATTN_HEAD_DIM = 128
XATTN_WIDTH = XATTN_HEADS * XATTN_HEAD_DIM

MOE_GROUPS = 8
MOE_EXPERTS_PER_GROUP = 8
MOE_EXPERTS = MOE_GROUPS * MOE_EXPERTS_PER_GROUP
MOE_TOPK = 2
MOE_ROW_BLOCK = 128
MOE_X_SLOTS = 4
MOE_Y_SLOTS = 4

LANES = 128

COL_MQ = 0
COL_MK = COL_MQ + MOBA_WIDTH
COL_MV = COL_MK + MOBA_WIDTH
COL_GQ = COL_MV + MOBA_WIDTH
COL_GK = COL_GQ + GLA_K_WIDTH
COL_GV = COL_GK + GLA_K_WIDTH
COL_GR = COL_GV + GLA_V_WIDTH
COL_PU = COL_GR + GLA_V_WIDTH
PROJ_WIDTH = COL_PU + POOL_WIDTH
W_IN_LOW = COL_PU
W_IN_POOL = COL_PU + GLA_RANK

ROUTE_E1, ROUTE_E2, ROUTE_W1, ROUTE_W2, ROUTE_RANK1, ROUTE_RANK2 = range(6)

VMEM_LIMIT = 56 * 1024 * 1024


def _params(*semantics):
    return pltpu.CompilerParams(dimension_semantics=semantics, vmem_limit_bytes=VMEM_LIMIT)


def _rms(x, g):
    return x * lax.rsqrt(jnp.mean(x * x, axis=-1, keepdims=True) + EPS) * g


def _dot(a, b):
    return jnp.dot(a, b, preferred_element_type=F32)


def _dot_nt(a, b):
    return lax.dot_general(a, b, (((1,), (1,)), ((), ())), preferred_element_type=F32)


def _dot_tn(a, b):
    return lax.dot_general(a, b, (((0,), (0,)), ((), ())), preferred_element_type=F32)


def _split_bf16(x, terms):
    parts = []
    for _ in range(terms):
        part = x.astype(BF16)
        parts.append(part)
        x = x - part.astype(F32)
    return parts


def _layer_vec(v, l):
    return v[l].reshape(1, -1)


def _inproj_kernel(x_ref, g_ref, w_ref, wp_ref, cos_ref, sin_ref, o_ref, h_ref, *, tn, n_main):
    j = pl.program_id(1)

    @pl.when(j == 0)
    def _():
        h_ref[...] = _rms(x_ref[...], g_ref[...]).astype(BF16)

    n_rope = (2 * MOBA_WIDTH) // tn
    n_q = MOBA_WIDTH // tn

    @pl.when(j < n_rope)
    def _():
        acc = _dot(h_ref[...], w_ref[...])
        scale = jnp.where(j < n_q, MOBA_Q_SCALE, 1.0).astype(F32)
        cos = cos_ref[...] * scale
        sin = sin_ref[...] * scale
        for hh in range(tn // MOBA_HEAD_DIM):
            cols = slice(hh * MOBA_HEAD_DIM, (hh + 1) * MOBA_HEAD_DIM)
            a = acc[:, cols]
            rot = pltpu.roll(a, MOBA_HEAD_DIM // 2, axis=1)
            o_ref[:, cols] = (a * cos + rot * sin).astype(o_ref.dtype)

    @pl.when((j >= n_rope) & (j < n_main))
    def _():
        o_ref[...] = _dot(h_ref[...], w_ref[...]).astype(o_ref.dtype)

    @pl.when(j >= n_main)
    def _():
        o_ref[...] = _dot(h_ref[...], wp_ref[...]).astype(o_ref.dtype)


def _inproj(x, g, w_in, w_pool_in, l, cos_f, sin_f, *, tm=1024, tn=1024):
    n, d = x.shape
    tm = min(tm, n)
    n_main = COL_PU // tn
    n_tiles = PROJ_WIDTH // tn
    return pl.pallas_call(
        functools.partial(_inproj_kernel, tn=tn, n_main=n_main),
        out_shape=(jax.ShapeDtypeStruct((n, PROJ_WIDTH), BF16), jax.ShapeDtypeStruct((n, d), BF16)),
        grid=(n // tm, n_tiles),
        in_specs=[
            pl.BlockSpec((tm, d), lambda i, j: (i, 0)),
            pl.BlockSpec((1, d), lambda i, j: (0, 0)),
            pl.BlockSpec((None, d, tn), lambda i, j: (l, 0, jnp.minimum(j, n_main - 1))),
            pl.BlockSpec((d, tn), lambda i, j: (0, jnp.maximum(j - n_main, 0))),
            pl.BlockSpec((tm, MOBA_HEAD_DIM), lambda i, j: (i, 0)),
            pl.BlockSpec((tm, MOBA_HEAD_DIM), lambda i, j: (i, 0)),
        ],
        out_specs=(
            pl.BlockSpec((tm, tn), lambda i, j: (i, j)),
            pl.BlockSpec((tm, d), lambda i, j: (i, 0)),
        ),
        compiler_params=_params("parallel", "arbitrary"),
        name="inproj",
    )(x, g, w_in, w_pool_in, cos_f, sin_f)


def _moba_kernel(q_ref, k_ref, v_ref, o_ref, kmean_ref, *, n_blocks):
    qi = pl.program_id(1)
    blk = MOBA_BLOCK
    hd = MOBA_HEAD_DIM
    tile = MOBA_KV_BLOCKS * blk
    blk_shift = blk.bit_length() - 1
    head_cols = [slice(hh * hd, (hh + 1) * hd) for hh in range(MOBA_HEADS_PER_STEP)]

    @pl.when(qi == 0)
    def _():
        kmean_ref[...] = jnp.zeros_like(kmean_ref)

        def fill(b, carry):
            kb = k_ref[pl.ds(pl.multiple_of(b * blk, blk), blk), :].astype(F32)
            kmean_ref[pl.ds(b, 1), :] = jnp.mean(kb, axis=0, keepdims=True)
            return carry

        lax.fori_loop(0, n_blocks, fill, 0)

    start = pl.multiple_of(qi * blk, blk)
    lane = lax.broadcasted_iota(jnp.int32, (blk, LANES), 1)
    lane_f = lane.astype(F32)
    row = lax.broadcasted_iota(jnp.int32, (blk, blk), 0)
    col = lax.broadcasted_iota(jnp.int32, (blk, blk), 1)
    ones_own = jnp.ones((blk, hd), BF16)
    ones_tile = jnp.ones((tile, hd), BF16)

    q_augs, carry = [], []
    for cols in head_cols:
        q = q_ref[:, cols]
        kmean_hi, kmean_mid = _split_bf16(kmean_ref[:, cols], 2)
        gate = _dot_nt(q, kmean_hi) + _dot_nt(q, kmean_mid)
        gate = jnp.where(lane < qi, gate, NEG_INF)
        sel = jnp.zeros(gate.shape, jnp.bool_)
        for _ in range(MOBA_TOPK):
            top = jnp.max(gate, axis=1, keepdims=True)
            idx = jnp.min(jnp.where(gate == top, lane_f, float(LANES)), axis=1, keepdims=True)
            hit = lane_f == idx
            sel = sel | (hit & (top > 0.5 * NEG_INF))
            gate = jnp.where(hit, NEG_INF, gate)
        q_augs.append(jnp.concatenate([q, jnp.where(sel, 0.0, NEG_INF).astype(BF16)], axis=1))

        s = _dot_nt(q, k_ref[pl.ds(start, blk), cols])
        s = jnp.where(col <= row, s, NEG_INF)
        m = jnp.max(s, axis=1, keepdims=True)
        p = jnp.exp2(s - m).astype(BF16)
        acc = _dot(p, jnp.concatenate([v_ref[pl.ds(start, blk), cols], ones_own], axis=1))
        carry += [m, acc]

    tile_row = lax.broadcasted_iota(jnp.int32, (tile, hd), 0)
    tile_lane = lax.broadcasted_iota(jnp.int32, (tile, hd), 1)

    def body(t, carry):
        off = pl.multiple_of(t * tile, tile)
        block_hot = (tile_lane == t * MOBA_KV_BLOCKS + (tile_row >> blk_shift)).astype(BF16)
        out = []
        for hh, cols in enumerate(head_cols):
            m, acc = carry[2 * hh], carry[2 * hh + 1]
            k_aug = jnp.concatenate([k_ref[pl.ds(off, tile), cols], block_hot], axis=1)
            s = _dot_nt(q_augs[hh], k_aug)
            m_new = jnp.maximum(m, jnp.max(s, axis=1, keepdims=True))
            alpha = jnp.exp2(m - m_new)
            p = jnp.exp2(s - m_new).astype(BF16)
            v_aug = jnp.concatenate([v_ref[pl.ds(off, tile), cols], ones_tile], axis=1)
            out += [m_new, alpha * acc + _dot(p, v_aug)]
        return tuple(out)

    n_tiles = (qi + MOBA_KV_BLOCKS - 1) // MOBA_KV_BLOCKS
    carry = lax.fori_loop(0, n_tiles, body, tuple(carry))
    for hh, cols in enumerate(head_cols):
        acc = carry[2 * hh + 1]
        o_ref[:, cols] = (acc[:, :hd] / acc[:, hd:]).astype(o_ref.dtype)


def _moba(proj):
    n = proj.shape[0]
    n_blocks = n // MOBA_BLOCK
    assert n_blocks % MOBA_KV_BLOCKS == 0 and n_blocks <= LANES
    width = MOBA_HEADS_PER_STEP * MOBA_HEAD_DIM
    return pl.pallas_call(
        functools.partial(_moba_kernel, n_blocks=n_blocks),
        out_shape=jax.ShapeDtypeStruct((n, MOBA_WIDTH), BF16),
        grid=(MOBA_HEADS // MOBA_HEADS_PER_STEP, n_blocks),
        in_specs=[
            pl.BlockSpec((MOBA_BLOCK, width), lambda h, i: (i, COL_MQ // width + h)),
            pl.BlockSpec((n, width), lambda h, i: (0, COL_MK // width + h), pipeline_mode=pl.Buffered(1)),
            pl.BlockSpec((n, width), lambda h, i: (0, COL_MV // width + h), pipeline_mode=pl.Buffered(1)),
        ],
        out_specs=pl.BlockSpec((MOBA_BLOCK, width), lambda h, i: (i, h)),
        scratch_shapes=[pltpu.VMEM((LANES, width), F32)],
        compiler_params=_params("parallel", "arbitrary"),
        name="moba",
    )(proj, proj, proj)


def _gla_kernel(q_ref, k_ref, v_ref, r_ref, h_ref, wlow_ref, wdec_ref, bdec_ref, gn_ref, o_ref, state_ref, *, tb):
    i = pl.program_id(0)
    c = GLA_CHUNK
    n_chunks = tb // c
    chunk_shift = c.bit_length() - 1

    @pl.when(i == 0)
    def _():
        state_ref[...] = jnp.zeros_like(state_ref)

    g_low = _dot(h_ref[...], wlow_ref[...]).astype(BF16)
    z = _dot(g_low, wdec_ref[...]) + bdec_ref[...]
    log_a = jax.nn.log_sigmoid(z) / GLA_TAU

    row = lax.broadcasted_iota(jnp.int32, (tb, tb), 0)
    col = lax.broadcasted_iota(jnp.int32, (tb, tb), 1)
    causal = ((row >> chunk_shift) == (col >> chunk_shift)) & (col <= row)
    tri = (lax.broadcasted_iota(jnp.int32, (c, c), 1) <= lax.broadcasted_iota(jnp.int32, (c, c), 0)).astype(BF16)
    parts = _split_bf16(log_a, 3)
    b_chunks, last_chunks = [], []
    for ci in range(n_chunks):
        rows = slice(ci * c, (ci + 1) * c)
        b_c = sum(_dot(tri, part[rows, :]) for part in parts)
        b_chunks.append(b_c)
        last_chunks.append(jnp.broadcast_to(b_c[c - 1:c, :], b_c.shape))
    b = jnp.concatenate(b_chunks, axis=0)
    b_last = jnp.concatenate(last_chunks, axis=0)

    q_dec = (q_ref[...].astype(F32) * (GLA_DK ** -0.5) * jnp.exp(b)).astype(BF16)
    kf = k_ref[...].astype(F32)
    k_inv = (kf * jnp.exp(-b)).astype(BF16)
    k_end = (kf * jnp.exp(b_last - b)).astype(BF16)
    decay = jnp.exp(b_last)

    for h in range(GLA_HEADS):
        kc = slice(h * GLA_DK, (h + 1) * GLA_DK)
        vc = slice(h * GLA_DV, (h + 1) * GLA_DV)
        v = v_ref[:, vc]
        attn = jnp.where(causal, _dot_nt(q_dec[:, kc], k_inv[:, kc]), 0.0)
        o = _dot(attn.astype(BF16), v)
        state = state_ref[h]
        inter = []
        for ci in range(n_chunks):
            rows = slice(ci * c, (ci + 1) * c)
            inter.append(_dot_nt(q_dec[rows, kc], state.astype(BF16)))
            update = _dot_tn(v[rows, :], k_end[rows, kc])
            state = decay[ci * c:ci * c + 1, kc] * state + update
        state_ref[h] = state
        o = o + jnp.concatenate(inter, axis=0)
        o = _rms(o, gn_ref[...])
        o_ref[:, vc] = (o * jax.nn.silu(r_ref[:, vc].astype(F32))).astype(o_ref.dtype)


def _gla(proj, h, w_low, w_dec, b_dec, gla_norm, *, tb=512):
    n, d = h.shape
    tb = min(tb, n)
    return pl.pallas_call(
        functools.partial(_gla_kernel, tb=tb),
        out_shape=jax.ShapeDtypeStruct((n, GLA_V_WIDTH), BF16),
        grid=(n // tb,),
        in_specs=[
            pl.BlockSpec((tb, GLA_K_WIDTH), lambda i: (i, COL_GQ // GLA_K_WIDTH)),
            pl.BlockSpec((tb, GLA_K_WIDTH), lambda i: (i, COL_GK // GLA_K_WIDTH)),
            pl.BlockSpec((tb, GLA_V_WIDTH), lambda i: (i, COL_GV // GLA_V_WIDTH)),
            pl.BlockSpec((tb, GLA_V_WIDTH), lambda i: (i, COL_GR // GLA_V_WIDTH)),
            pl.BlockSpec((tb, d), lambda i: (i, 0)),
            pl.BlockSpec((d, LANES), lambda i: (0, 0)),
            pl.BlockSpec((LANES, GLA_K_WIDTH), lambda i: (0, 0)),
            pl.BlockSpec((1, GLA_K_WIDTH), lambda i: (0, 0)),
            pl.BlockSpec((1, GLA_DV), lambda i: (0, 0)),
        ],
        out_specs=pl.BlockSpec((tb, GLA_V_WIDTH), lambda i: (i, 0)),
        scratch_shapes=[pltpu.VMEM((GLA_HEADS, GLA_DV, GLA_DK), F32)],
        compiler_params=_params("arbitrary"),
        name="gla",
    )(proj, proj, proj, proj, h, w_low, w_dec, b_dec, gla_norm)


def _pool_kernel(u_ref, halo_ref, w_ref, sc_ref, o_ref, ext_ref, *, tb):
    i = pl.program_id(0)
    ext_ref[pl.ds(POOL_HALO, tb), :] = u_ref[...].astype(F32)
    ext_ref[pl.ds(0, POOL_HALO), :] = jnp.where(i == 0, 0.0, halo_ref[...].astype(F32))
    t = i * tb + lax.broadcasted_iota(jnp.int32, (tb, 1), 0)
    for g, win in enumerate(POOL_WINDOWS):
        cols = slice(g * POOL_GROUP_DIM, (g + 1) * POOL_GROUP_DIM)
        u = ext_ref[pl.ds(POOL_HALO, tb), cols]
        window_sum = u
        for back in range(1, win):
            window_sum = window_sum + ext_ref[pl.ds(POOL_HALO - back, tb), cols]
        count = jnp.minimum(t + 1, win).astype(F32)
        mixed = window_sum / count - u
        y = _dot(mixed.astype(BF16), w_ref[g])
        o_ref[:, cols] = (y * sc_ref[:, cols]).astype(o_ref.dtype)


def _pool(proj, w_pool, pool_scale, *, tb=512):
    n = proj.shape[0]
    tb = min(tb, n)
    halo_per_block = tb // POOL_HALO
    return pl.pallas_call(
        functools.partial(_pool_kernel, tb=tb),
        out_shape=jax.ShapeDtypeStruct((n, POOL_WIDTH), BF16),
        grid=(n // tb,),
        in_specs=[
            pl.BlockSpec((tb, POOL_WIDTH), lambda i: (i, COL_PU // POOL_WIDTH)),
            pl.BlockSpec((POOL_HALO, POOL_WIDTH),
                         lambda i: (jnp.maximum(i * halo_per_block - 1, 0), COL_PU // POOL_WIDTH)),
            pl.BlockSpec(w_pool.shape, lambda i: (0, 0, 0)),
            pl.BlockSpec((1, POOL_WIDTH), lambda i: (0, 0)),
        ],
        out_specs=pl.BlockSpec((tb, POOL_WIDTH), lambda i: (i, 0)),
        scratch_shapes=[pltpu.VMEM((POOL_HALO + tb, POOL_WIDTH), F32)],
        compiler_params=_params("parallel"),
        name="pool",
    )(proj, proj, w_pool, pool_scale)


def _merge_kernel(h_ref, a_ref, b_ref, c_ref, wg0, wg1, wg2, bg0, bg1, bg2, wb0, wb1, wb2, o_ref):
    h = h_ref[...]
    total = None
    for br_ref, wg, bg, wb in ((a_ref, wg0, bg0, wb0), (b_ref, wg1, bg1, wb1), (c_ref, wg2, bg2, wb2)):
        gate = jax.nn.sigmoid(_dot(h, wg[...].astype(BF16)) + bg[...])
        term = gate * _dot(br_ref[...], wb[...].astype(BF16))
        total = term if total is None else total + term
    o_ref[...] = total.astype(o_ref.dtype)


def _merge(h, o_moba, o_gla, o_pool, w_gate, b_gate, w_branch, l, *, tm=1024, tn=256):
    n, d = h.shape
    tm = min(tm, n)
    bw = o_moba.shape[1]
    tiles = d // tn
    row_spec = lambda width: pl.BlockSpec((tm, width), lambda i, j: (i, 0))
    gate_specs = [pl.BlockSpec((None, d, tn), lambda i, j, k=k: (l, 0, k * tiles + j)) for k in range(3)]
    bias_specs = [pl.BlockSpec((1, tn), lambda i, j, k=k: (0, k * tiles + j)) for k in range(3)]
    branch_specs = [pl.BlockSpec((None, None, bw, tn), lambda i, j, k=k: (l, k, 0, j)) for k in range(3)]
    return pl.pallas_call(
        _merge_kernel,
        out_shape=jax.ShapeDtypeStruct((n, d), BF16),
        grid=(n // tm, tiles),
        in_specs=[row_spec(d), row_spec(bw), row_spec(bw), row_spec(bw)] + gate_specs + bias_specs + branch_specs,
        out_specs=pl.BlockSpec((tm, tn), lambda i, j: (i, j)),
        compiler_params=_params("parallel", "arbitrary"),
        name="merge",
    )(h, o_moba, o_gla, o_pool, w_gate, w_gate, w_gate, b_gate, b_gate, b_gate, w_branch, w_branch, w_branch)


def _matmul_residual_kernel(a_ref, w_ref, x_ref, o_ref):
    o_ref[...] = x_ref[...] + _dot(a_ref[...], w_ref[...].astype(BF16))


def _matmul_residual(a, w, l, x, *, tm=1024, tn=512):
    n, k = a.shape
    d = w.shape[-1]
    tm = min(tm, n)
    return pl.pallas_call(
        _matmul_residual_kernel,
        out_shape=jax.ShapeDtypeStruct((n, d), F32),
        grid=(n // tm, d // tn),
        in_specs=[
            pl.BlockSpec((tm, k), lambda i, j: (i, 0)),
            pl.BlockSpec((None, k, tn), lambda i, j: (l, 0, j)),
            pl.BlockSpec((tm, tn), lambda i, j: (i, j)),
        ],
        out_specs=pl.BlockSpec((tm, tn), lambda i, j: (i, j)),
        compiler_params=_params("parallel", "arbitrary"),
        name="mix_out",
    )(a, w, x)


def _norm_matmul_kernel(x_ref, g_ref, w_ref, o_ref):
    o_ref[...] = _dot(_rms(x_ref[...], g_ref[...]).astype(BF16), w_ref[...].astype(BF16)).astype(o_ref.dtype)


def _mem_kv(mem, g, w_xkv, l):
    n, d = mem.shape
    width = w_xkv.shape[-1]
    return pl.pallas_call(
        _norm_matmul_kernel,
        out_shape=jax.ShapeDtypeStruct((n, width), BF16),
        grid=(1,),
        in_specs=[
            pl.BlockSpec((n, d), lambda i: (0, 0)),
            pl.BlockSpec((1, d), lambda i: (0, 0)),
            pl.BlockSpec((None, d, width), lambda i: (l, 0, 0)),
        ],
        out_specs=pl.BlockSpec((n, width), lambda i: (0, 0)),
        compiler_params=_params("arbitrary"),
        name="mem_kv",
    )(mem, g, w_xkv)


def _xattn_kernel(x_ref, g_ref, wq_ref, kv_ref, wo_ref, gm_ref, wr_ref, br_ref,
                  xo_ref, h2_ref, route_ref, counts_ref, wq_b, wo_b, counts_acc, *, tm):
    i = pl.program_id(0)

    @pl.when(i == 0)
    def _():
        wq_b[...] = wq_ref[...].astype(BF16)
        wo_b[...] = wo_ref[...].astype(BF16)
        counts_acc[...] = jnp.zeros_like(counts_acc)

    x = x_ref[...]
    h = _rms(x, g_ref[...]).astype(BF16)
    q = (_dot(h, wq_b[...]) * (XATTN_HEAD_DIM ** -0.5)).astype(BF16)
    heads = []
    for hd in range(XATTN_HEADS):
        kc = slice(hd * XATTN_HEAD_DIM, (hd + 1) * XATTN_HEAD_DIM)
        vc = slice(XATTN_WIDTH + hd * XATTN_HEAD_DIM, XATTN_WIDTH + (hd + 1) * XATTN_HEAD_DIM)
        s = _dot_nt(q[:, kc], kv_ref[:, kc])
        p = jnp.exp(s - jnp.max(s, axis=1, keepdims=True))
        p = p / jnp.sum(p, axis=1, keepdims=True)
        heads.append(_dot(p.astype(BF16), kv_ref[:, vc]).astype(BF16))
    o = jnp.concatenate(heads, axis=1)
    xn = x + _dot(o, wo_b[...])
    xo_ref[...] = xn
    h2 = _rms(xn, gm_ref[...])
    h2_ref[...] = h2

    h_hi, h_mid = _split_bf16(h2, 2)
    w_hi, w_mid = wr_ref[0], wr_ref[1]
    lg = _dot(h_hi, w_hi) + (_dot(h_hi, w_mid) + _dot(h_mid, w_hi)) + br_ref[...]

    lane = lax.broadcasted_iota(jnp.int32, lg.shape, 1)
    lane_f = lane.astype(F32)
    big = float(LANES)

    def top1(v):
        top = jnp.max(v, axis=1, keepdims=True)
        return top, jnp.min(jnp.where(v == top, lane_f, big), axis=1, keepdims=True)

    g_logit = jnp.where(lane < MOE_GROUPS, lg, NEG_INF)
    g_top, g_idx = top1(g_logit)
    g_w = 1.0 / jnp.sum(jnp.exp(g_logit - g_top), axis=1, keepdims=True)
    first = MOE_GROUPS + MOE_EXPERTS_PER_GROUP * g_idx
    e_logit = jnp.where((lane_f >= first) & (lane_f < first + MOE_EXPERTS_PER_GROUP), lg, NEG_INF)
    e_top1, lane1 = top1(e_logit)
    e_top2, lane2 = top1(jnp.where(lane_f == lane1, NEG_INF, e_logit))
    z = jnp.sum(jnp.exp(e_logit - e_top1), axis=1, keepdims=True)
    p1 = 1.0 / z
    p2 = jnp.exp(e_top2 - e_top1) / z
    w1 = g_w * p1 / (p1 + p2)
    w2 = g_w * p2 / (p1 + p2)

    two_hot = (lane_f == lane1) | (lane_f == lane2)
    row = lax.broadcasted_iota(jnp.int32, (tm, tm), 0)
    col = lax.broadcasted_iota(jnp.int32, (tm, tm), 1)
    before = _dot((col < row).astype(BF16), two_hot.astype(BF16)) + counts_acc[...]
    rank1 = jnp.sum(jnp.where(lane_f == lane1, before, 0.0), axis=1, keepdims=True)
    rank2 = jnp.sum(jnp.where(lane_f == lane2, before, 0.0), axis=1, keepdims=True)
    counts_acc[...] += jnp.sum(two_hot.astype(F32), axis=0, keepdims=True)
    counts_ref[...] = counts_acc[...]

    record = jnp.zeros(lg.shape, F32)
    for slot, val in ((ROUTE_E1, lane1 - MOE_GROUPS), (ROUTE_E2, lane2 - MOE_GROUPS), (ROUTE_W1, w1),
                      (ROUTE_W2, w2), (ROUTE_RANK1, rank1), (ROUTE_RANK2, rank2)):
        record = jnp.where(lane == slot, val, record)
    route_ref[...] = record


def _xattn(x, g, wq, kv, wo, l, g_moe, w_route_split, b_route, *, tm=512):
    n, d = x.shape
    tm = min(tm, n)
    full = lambda a: pl.BlockSpec(a.shape, lambda i: (0,) * a.ndim)
    layer = lambda a: pl.BlockSpec((None,) + a.shape[1:], lambda i: (l,) + (0,) * (a.ndim - 1))
    rows = lambda width: pl.BlockSpec((tm, width), lambda i: (i, 0))
    return pl.pallas_call(
        functools.partial(_xattn_kernel, tm=tm),
        out_shape=(jax.ShapeDtypeStruct((n, d), F32), jax.ShapeDtypeStruct((n, d), F32),
                   jax.ShapeDtypeStruct((n, LANES), F32), jax.ShapeDtypeStruct((1, LANES), F32)),
        grid=(n // tm,),
        in_specs=[rows(d), full(g), layer(wq), full(kv), layer(wo), full(g_moe), full(w_route_split), full(b_route)],
        out_specs=(rows(d), rows(d), rows(LANES), pl.BlockSpec((1, LANES), lambda i: (0, 0))),
        scratch_shapes=[pltpu.VMEM(wq.shape[1:], BF16), pltpu.VMEM(wo.shape[1:], BF16),
                        pltpu.VMEM((1, LANES), F32)],
        compiler_params=_params("arbitrary"),
        name="xattn_route",
    )(x, g, wq, kv, wo, g_moe, w_route_split, b_route)


def _moe_kernel(start_ref, count_ref, x_hbm, w1_hbm, w3_hbm, w2_hbm, y_hbm,
                w1f, w3f, w2f, w1b, w3b, w2b, xbuf, ybuf, wsem, xsem, ysem, *, layer):
    e = pl.program_id(0)
    n_experts = pl.num_programs(0)
    rb = MOE_ROW_BLOCK
    par = e % 2
    first = start_ref[e]
    n_blocks = count_ref[e]

    def rows(block):
        return pl.ds(pl.multiple_of(block * rb, rb), rb)

    def w_copies(expert, slot):
        return (pltpu.make_async_copy(w1_hbm.at[layer, expert], w1f.at[slot], wsem.at[0, slot]),
                pltpu.make_async_copy(w3_hbm.at[layer, expert], w3f.at[slot], wsem.at[1, slot]),
                pltpu.make_async_copy(w2_hbm.at[layer, expert], w2f.at[slot], wsem.at[2, slot]))

    def x_copy(block, slot, xs):
        return pltpu.make_async_copy(x_hbm.at[rows(block), :], xbuf.at[slot, xs], xsem.at[slot, xs])

    def y_copy(block, ys):
        return pltpu.make_async_copy(ybuf.at[ys], y_hbm.at[rows(block), :], ysem.at[ys])

    def prefetch(expert, slot):
        for xs in range(MOE_X_SLOTS):
            @pl.when(xs < count_ref[expert])
            def _():
                x_copy(start_ref[expert] + xs, slot, xs).start()
        for copy in w_copies(expert, slot):
            copy.start()

    @pl.when(e == 0)
    def _():
        prefetch(0, 0)

    @pl.when(e + 1 < n_experts)
    def _():
        prefetch(e + 1, 1 - par)

    for copy in w_copies(e, par):
        copy.wait()
    w1b[...] = w1f[par].astype(BF16)
    w3b[...] = w3f[par].astype(BF16)
    w2b[...] = w2f[par].astype(BF16)

    def body(c, carry):
        xs = c % MOE_X_SLOTS
        ys = c % MOE_Y_SLOTS

        @pl.when(c < MOE_X_SLOTS)
        def _():
            x_copy(first + c, par, xs).wait()

        @pl.when(c >= MOE_X_SLOTS)
        def _():
            copy = x_copy(first + c, par, xs)
            copy.start()
            copy.wait()

        @pl.when(c >= MOE_Y_SLOTS)
        def _():
            y_copy(first + c - MOE_Y_SLOTS, ys).wait()

        x = xbuf[par, xs].astype(BF16)
        hidden = jax.nn.silu(_dot(x, w1b[...])) * _dot(x, w3b[...])
        ybuf[ys] = _dot(hidden.astype(BF16), w2b[...])
        y_copy(first + c, ys).start()
        return carry

    lax.fori_loop(0, n_blocks, body, 0)

    for back in range(1, MOE_Y_SLOTS + 1):
        @pl.when(n_blocks >= back)
        def _():
            y_copy(first + n_blocks - back, (n_blocks - back) % MOE_Y_SLOTS).wait()

    @pl.when(e == n_experts - 1)
    def _():
        ybuf[0] = jnp.zeros(ybuf.shape[1:], ybuf.dtype)
        used = first + n_blocks
        total = y_hbm.shape[0] // rb

        def start(block, carry):
            y_copy(block, 0).start()
            return carry

        def wait(block, carry):
            y_copy(block, 0).wait()
            return carry

        lax.fori_loop(used, total, start, 0)
        lax.fori_loop(used, total, wait, 0)


def _moe_experts(block_start, block_count, x_rows, w1, w3, w2, l):
    p, d = x_rows.shape
    rb = MOE_ROW_BLOCK
    n_experts, dff = w1.shape[1], w1.shape[-1]
    hbm = pl.BlockSpec(memory_space=pl.ANY)
    grid_spec = pltpu.PrefetchScalarGridSpec(
        num_scalar_prefetch=2,
        grid=(n_experts,),
        in_specs=[hbm, hbm, hbm, hbm],
        out_specs=hbm,
        scratch_shapes=[
            pltpu.VMEM((2, d, dff), F32), pltpu.VMEM((2, d, dff), F32), pltpu.VMEM((2, dff, d), F32),
            pltpu.VMEM((d, dff), BF16), pltpu.VMEM((d, dff), BF16), pltpu.VMEM((dff, d), BF16),
            pltpu.VMEM((2, MOE_X_SLOTS, rb, d), F32), pltpu.VMEM((MOE_Y_SLOTS, rb, d), F32),
            pltpu.SemaphoreType.DMA((3, 2)), pltpu.SemaphoreType.DMA((2, MOE_X_SLOTS)),
            pltpu.SemaphoreType.DMA((MOE_Y_SLOTS,)),
        ],
    )
    return pl.pallas_call(
        functools.partial(_moe_kernel, layer=l),
        out_shape=jax.ShapeDtypeStruct((p, d), F32),
        grid_spec=grid_spec,
        compiler_params=_params("arbitrary"),
        name="moe_experts",
    )(block_start, block_count, x_rows, w1, w3, w2)


DISPATCH_DEST1, DISPATCH_DEST2 = 0, 1
META_BLOCK_START, META_BLOCK_COUNT = 0, 1


def _dispatch_kernel(route_ref, counts_ref, dest_ref, meta_ref):
    rb = float(MOE_ROW_BLOCK)
    counts = jnp.broadcast_to(counts_ref[...], (8, LANES))
    padded = jnp.floor((counts + (rb - 1.0)) * (1.0 / rb)) * rb
    src = lax.broadcasted_iota(jnp.int32, (LANES, LANES), 0)
    dst = lax.broadcasted_iota(jnp.int32, (LANES, LANES), 1)
    earlier = (src < dst).astype(BF16)
    seg_start = sum(_dot(part, earlier) for part in _split_bf16(padded, 3))[0:1]

    route = route_ref[...]
    lane = lax.broadcasted_iota(jnp.int32, route.shape, 1)
    lane_f = lane.astype(F32)

    def field(slot):
        return jnp.sum(jnp.where(lane == slot, route, 0.0), axis=1, keepdims=True)

    def dest(e_slot, rank_slot):
        hit = lane_f == field(e_slot) + float(MOE_GROUPS)
        return jnp.sum(jnp.where(hit, seg_start, 0.0), axis=1, keepdims=True) + field(rank_slot)

    dest1 = dest(ROUTE_E1, ROUTE_RANK1)
    dest2 = dest(ROUTE_E2, ROUTE_RANK2)
    dest_ref[...] = jnp.where(lane == DISPATCH_DEST1, dest1, jnp.where(lane == DISPATCH_DEST2, dest2, 0.0))

    sub = lax.broadcasted_iota(jnp.int32, (8, LANES), 0)
    meta_ref[...] = jnp.where(sub == META_BLOCK_START, jnp.broadcast_to(seg_start, (8, LANES)) * (1.0 / rb),
                              jnp.where(sub == META_BLOCK_COUNT, padded * (1.0 / rb), 0.0))


def _dispatch(route, counts, *, tm=1024):
    n = route.shape[0]
    tm = min(tm, n)
    return pl.pallas_call(
        _dispatch_kernel,
        out_shape=(jax.ShapeDtypeStruct((n, LANES), F32), jax.ShapeDtypeStruct((8, LANES), F32)),
        grid=(n // tm,),
        in_specs=[pl.BlockSpec((tm, LANES), lambda i: (i, 0)), pl.BlockSpec((1, LANES), lambda i: (0, 0))],
        out_specs=(pl.BlockSpec((tm, LANES), lambda i: (i, 0)), pl.BlockSpec((8, LANES), lambda i: (0, 0))),
        compiler_params=_params("arbitrary"),
        name="moe_dispatch",
    )(route, counts)


def _dispatch_layout(route, counts):
    n = route.shape[0]
    rb = MOE_ROW_BLOCK
    dest, meta = _dispatch(route, counts)
    dest1 = dest[:, DISPATCH_DEST1].astype(jnp.int32)
    dest2 = dest[:, DISPATCH_DEST2].astype(jnp.int32)
    experts = slice(MOE_GROUPS, MOE_GROUPS + MOE_EXPERTS)
    block_start = meta[META_BLOCK_START, experts].astype(jnp.int32)
    block_count = meta[META_BLOCK_COUNT, experts].astype(jnp.int32)
    n_rows = (n * MOE_TOPK + MOE_EXPERTS * (rb - 1) + rb - 1) // rb * rb
    tok = jnp.arange(n, dtype=jnp.int32)
    row_tok = (jnp.arange(n_rows, dtype=jnp.int32) % n).at[jnp.concatenate([dest1, dest2])].set(
        jnp.concatenate([tok, tok]), mode="promise_in_bounds", unique_indices=True)
    return row_tok, block_start, block_count, dest1, dest2


def _combine_kernel(x_ref, y1_ref, y2_ref, route_ref, g_ref, o_ref, *, final_norm):
    route = route_ref[...]
    w1 = route[:, ROUTE_W1:ROUTE_W1 + 1]
    w2 = route[:, ROUTE_W2:ROUTE_W2 + 1]
    out = x_ref[...] + (w1 * y1_ref[...] + w2 * y2_ref[...])
    o_ref[...] = _rms(out, g_ref[...]) if final_norm else out


def _combine(x, y1, y2, route, g, *, final_norm, tm=512):
    n, d = x.shape
    tm = min(tm, n)
    rows = lambda width: pl.BlockSpec((tm, width), lambda i: (i, 0))
    return pl.pallas_call(
        functools.partial(_combine_kernel, final_norm=final_norm),
        out_shape=jax.ShapeDtypeStruct((n, d), F32),
        grid=(n // tm,),
        in_specs=[rows(d), rows(d), rows(d), rows(LANES), pl.BlockSpec((1, d), lambda i: (0, 0))],
        out_specs=rows(d),
        compiler_params=_params("parallel"),
        name="moe_combine",
    )(x, y1, y2, route, g)


def _rope_tables(positions):
    inv = 1.0 / (ROPE_THETA ** (jnp.arange(0, MOBA_HEAD_DIM, 2, dtype=F32) / MOBA_HEAD_DIM))
    ang = positions.astype(F32)[:, None] * inv
    cos, sin = jnp.cos(ang), jnp.sin(ang)
    return jnp.concatenate([cos, cos], axis=-1), jnp.concatenate([-sin, sin], axis=-1)


def _layer(x, mem, cos_f, sin_f, p, l, norm_final):
    w_in = p["w_in"]
    w_pool_in = w_in[l, :, W_IN_POOL:]
    w_low = jnp.pad(w_in[l, :, W_IN_LOW:W_IN_POOL], ((0, 0), (0, LANES - GLA_RANK)))
    proj, h = _inproj(x, _layer_vec(p["norm_mix"], l), w_in, w_pool_in, l, cos_f, sin_f)
    o_moba = _moba(proj)
    w_dec = jnp.pad(p["w_gla_decay"][l], ((0, LANES - GLA_RANK), (0, 0))).astype(BF16)
    o_gla = _gla(proj, h, w_low, w_dec, _layer_vec(p["b_gla_decay"], l), _layer_vec(p["gla_norm"], l))
    o_pool = _pool(proj, p["w_pool"][l].astype(BF16), _layer_vec(p["pool_scale"], l))
    merged = _merge(h, o_moba, o_gla, o_pool, p["w_gate"], _layer_vec(p["b_gate"], l), p["w_branch"], l)
    x = _matmul_residual(merged, p["w_mix_out"], l, x)

    kv = _mem_kv(mem, _layer_vec(p["norm_mem"], l), p["w_xkv"], l)
    w_route = jnp.concatenate([p["w_route_group"][l], p["w_route_expert"][l]], axis=1)
    n_route = w_route.shape[1]
    w_route_split = jnp.stack(_split_bf16(jnp.pad(w_route, ((0, 0), (0, LANES - n_route))), 2))
    b_route = jnp.pad(jnp.concatenate([p["b_route_group"][l], p["b_route_expert"][l]]), (0, LANES - n_route))
    x, h2, route, counts = _xattn(x, _layer_vec(p["norm_xattn"], l), p["w_xq"], kv, p["w_xo"], l,
                                  _layer_vec(p["norm_moe"], l), w_route_split, b_route.reshape(1, -1))

    row_tok, block_start, block_count, dest1, dest2 = _dispatch_layout(route, counts)
    x_rows = h2.at[row_tok].get(mode="promise_in_bounds")
    y_rows = _moe_experts(block_start, block_count, x_rows, p["w_exp_gate"], p["w_exp_up"], p["w_exp_down"], l)
    y1 = y_rows.at[dest1].get(mode="promise_in_bounds")
    y2 = y_rows.at[dest2].get(mode="promise_in_bounds")
    return _combine(x, y1, y2, route, norm_final.reshape(1, -1), final_norm=l == p["norm_mix"].shape[0] - 1)


def kernel(x, mem, positions, norm_mix, w_in, w_gla_decay, b_gla_decay, gla_norm, w_pool, pool_scale, w_branch, w_gate, b_gate, w_mix_out, norm_xattn, norm_mem, w_xq, w_xkv, w_xo, norm_moe, w_route_group, b_route_group, w_route_expert, b_route_expert, w_exp_gate, w_exp_up, w_exp_down, norm_final):
    batch, seq, d = x.shape
    assert batch == 1, "kernels are written for a single sequence"
    params = dict(norm_mix=norm_mix, w_in=w_in.astype(BF16), w_gla_decay=w_gla_decay, b_gla_decay=b_gla_decay,
                  gla_norm=gla_norm, w_pool=w_pool, pool_scale=pool_scale, w_branch=w_branch, w_gate=w_gate,
                  b_gate=b_gate, w_mix_out=w_mix_out, norm_xattn=norm_xattn, norm_mem=norm_mem, w_xq=w_xq,
                  w_xkv=w_xkv, w_xo=w_xo, norm_moe=norm_moe, w_route_group=w_route_group,
                  b_route_group=b_route_group, w_route_expert=w_route_expert, b_route_expert=b_route_expert,
                  w_exp_gate=w_exp_gate, w_exp_up=w_exp_up, w_exp_down=w_exp_down)
    cos_f, sin_f = _rope_tables(positions[0])
    xs = x[0]
    for l in range(norm_mix.shape[0]):
        xs = _layer(xs, mem[0], cos_f, sin_f, params, l, norm_final)
    return xs.reshape(batch, seq, d)
```

```python
import functools

import jax
import jax.numpy as jnp
from jax import lax
from jax.experimental import pallas as pl
from jax.experimental.pallas import tpu as pltpu

F32 = jnp.float32
BF16 = jnp.bfloat16

EPS = 1e-6
NEG_INF = -1e30

MOBA_HEADS = 8
MOBA_HEAD_DIM = 128
MOBA_WIDTH = MOBA_HEADS * MOBA_HEAD_DIM
MOBA_BLOCK = 256
MOBA_TOPK = 3
MOBA_KV_BLOCKS = 4
MOBA_HEADS_PER_STEP = 4
ROPE_THETA = 10000.0
MOBA_Q_SCALE = MOBA_HEAD_DIM ** -0.5 * 1.4426950408889634

GLA_HEADS = 4
GLA_DK = 128
GLA_DV = 256
GLA_K_WIDTH = GLA_HEADS * GLA_DK
GLA_V_WIDTH = GLA_HEADS * GLA_DV
GLA_RANK = 16
GLA_TAU = 16.0
GLA_CHUNK = 64

POOL_WINDOWS = (2, 4, 8, 16)
POOL_GROUP_DIM = 256
POOL_WIDTH = len(POOL_WINDOWS) * POOL_GROUP_DIM
POOL_HALO = 16

XATTN_HEADS = 4
XATTN_HEAD_DIM = 128
XATTN_WIDTH = XATTN_HEADS * XATTN_HEAD_DIM

MOE_GROUPS = 8
MOE_EXPERTS_PER_GROUP = 8
MOE_EXPERTS = MOE_GROUPS * MOE_EXPERTS_PER_GROUP
MOE_TOPK = 2
MOE_ROW_BLOCK = 128
MOE_ROW_ALIGN = 8
MOE_X_SLOTS = 4
MOE_Y_SLOTS = 4

LANES = 128

COL_MQ = 0
COL_MK = COL_MQ + MOBA_WIDTH
COL_MV = COL_MK + MOBA_WIDTH
COL_GQ = COL_MV + MOBA_WIDTH
COL_GK = COL_GQ + GLA_K_WIDTH
COL_GV = COL_GK + GLA_K_WIDTH
COL_GR = COL_GV + GLA_V_WIDTH
COL_PU = COL_GR + GLA_V_WIDTH
PROJ_WIDTH = COL_PU + POOL_WIDTH
W_IN_LOW = COL_PU
W_IN_POOL = COL_PU + GLA_RANK

ROUTE_E1, ROUTE_E2, ROUTE_W1, ROUTE_W2, ROUTE_RANK1, ROUTE_RANK2 = range(6)

VMEM_LIMIT = 56 * 1024 * 1024


def _params(*semantics):
    return pltpu.CompilerParams(dimension_semantics=semantics, vmem_limit_bytes=VMEM_LIMIT)


def _rms(x, g):
    return x * lax.rsqrt(jnp.mean(x * x, axis=-1, keepdims=True) + EPS) * g


def _dot(a, b):
    return jnp.dot(a, b, preferred_element_type=F32)


def _dot_nt(a, b):
    return lax.dot_general(a, b, (((1,), (1,)), ((), ())), preferred_element_type=F32)


def _dot_tn(a, b):
    return lax.dot_general(a, b, (((0,), (0,)), ((), ())), preferred_element_type=F32)


def _split_bf16(x, terms):
    parts = []
    for _ in range(terms):
        part = x.astype(BF16)
        parts.append(part)
        x = x - part.astype(F32)
    return parts


def _layer_vec(v, l):
    return v[l].reshape(1, -1)


def _inproj_kernel(x_ref, g_ref, w_ref, wp_ref, cos_ref, sin_ref, o_ref, h_ref, *, tn, n_main):
    j = pl.program_id(1)

    @pl.when(j == 0)
    def _():
        h_ref[...] = _rms(x_ref[...], g_ref[...]).astype(BF16)

    n_rope = (2 * MOBA_WIDTH) // tn
    n_q = MOBA_WIDTH // tn

    @pl.when(j < n_rope)
    def _():
        acc = _dot(h_ref[...], w_ref[...])
        scale = jnp.where(j < n_q, MOBA_Q_SCALE, 1.0).astype(F32)
        cos = cos_ref[...] * scale
        sin = sin_ref[...] * scale
        for hh in range(tn // MOBA_HEAD_DIM):
            cols = slice(hh * MOBA_HEAD_DIM, (hh + 1) * MOBA_HEAD_DIM)
            a = acc[:, cols]
            rot = pltpu.roll(a, MOBA_HEAD_DIM // 2, axis=1)
            o_ref[:, cols] = (a * cos + rot * sin).astype(o_ref.dtype)

    @pl.when((j >= n_rope) & (j < n_main))
    def _():
        o_ref[...] = _dot(h_ref[...], w_ref[...]).astype(o_ref.dtype)

    @pl.when(j >= n_main)
    def _():
        o_ref[...] = _dot(h_ref[...], wp_ref[...]).astype(o_ref.dtype)


def _inproj(x, g, w_in, w_pool_in, l, cos_f, sin_f, *, tm=1024, tn=1024):
    n, d = x.shape
    tm = min(tm, n)
    n_main = COL_PU // tn
    n_tiles = PROJ_WIDTH // tn
    return pl.pallas_call(
        functools.partial(_inproj_kernel, tn=tn, n_main=n_main),
        out_shape=(jax.ShapeDtypeStruct((n, PROJ_WIDTH), BF16), jax.ShapeDtypeStruct((n, d), BF16)),
        grid=(n // tm, n_tiles),
        in_specs=[
            pl.BlockSpec((tm, d), lambda i, j: (i, 0)),
            pl.BlockSpec((1, d), lambda i, j: (0, 0)),
            pl.BlockSpec((None, d, tn), lambda i, j: (l, 0, jnp.minimum(j, n_main - 1))),
            pl.BlockSpec((d, tn), lambda i, j: (0, jnp.maximum(j - n_main, 0))),
            pl.BlockSpec((tm, MOBA_HEAD_DIM), lambda i, j: (i, 0)),
            pl.BlockSpec((tm, MOBA_HEAD_DIM), lambda i, j: (i, 0)),
        ],
        out_specs=(
            pl.BlockSpec((tm, tn), lambda i, j: (i, j)),
            pl.BlockSpec((tm, d), lambda i, j: (i, 0)),
        ),
        compiler_params=_params("parallel", "arbitrary"),
        name="inproj",
    )(x, g, w_in, w_pool_in, cos_f, sin_f)


def _moba_kernel(q_ref, k_ref, v_ref, o_ref, kmean_ref, *, n_blocks):
    qi = pl.program_id(1)
    blk = MOBA_BLOCK
    hd = MOBA_HEAD_DIM
    tile = MOBA_KV_BLOCKS * blk
    blk_shift = blk.bit_length() - 1
    head_cols = [slice(hh * hd, (hh + 1) * hd) for hh in range(MOBA_HEADS_PER_STEP)]

    @pl.when(qi == 0)
    def _():
        kmean_ref[...] = jnp.zeros_like(kmean_ref)

        def fill(b, carry):
            kb = k_ref[pl.ds(pl.multiple_of(b * blk, blk), blk), :].astype(F32)
            kmean_ref[pl.ds(b, 1), :] = jnp.mean(kb, axis=0, keepdims=True)
            return carry

        lax.fori_loop(0, n_blocks, fill, 0)

    start = pl.multiple_of(qi * blk, blk)
    lane = lax.broadcasted_iota(jnp.int32, (blk, LANES), 1)
    lane_f = lane.astype(F32)
    row = lax.broadcasted_iota(jnp.int32, (blk, blk), 0)
    col = lax.broadcasted_iota(jnp.int32, (blk, blk), 1)
    ones_own = jnp.ones((blk, hd), BF16)
    ones_tile = jnp.ones((tile, hd), BF16)

    q_augs, carry = [], []
    for cols in head_cols:
        q = q_ref[:, cols]
        kmean_hi, kmean_mid = _split_bf16(kmean_ref[:, cols], 2)
        gate = _dot_nt(q, kmean_hi) + _dot_nt(q, kmean_mid)
        gate = jnp.where(lane < qi, gate, NEG_INF)
        sel = jnp.zeros(gate.shape, jnp.bool_)
        for _ in range(MOBA_TOPK):
            top = jnp.max(gate, axis=1, keepdims=True)
            idx = jnp.min(jnp.where(gate == top, lane_f, float(LANES)), axis=1, keepdims=True)
            hit = lane_f == idx
            sel = sel | (hit & (top > 0.5 * NEG_INF))
            gate = jnp.where(hit, NEG_INF, gate)
        q_augs.append(jnp.concatenate([q, jnp.where(sel, 0.0, NEG_INF).astype(BF16)], axis=1))

        s = _dot_nt(q, k_ref[pl.ds(start, blk), cols])
        s = jnp.where(col <= row, s, NEG_INF)
        m = jnp.max(s, axis=1, keepdims=True)
        p = jnp.exp2(s - m).astype(BF16)
        acc = _dot(p, jnp.concatenate([v_ref[pl.ds(start, blk), cols], ones_own], axis=1))
        carry += [m, acc]

    tile_row = lax.broadcasted_iota(jnp.int32, (tile, hd), 0)
    tile_lane = lax.broadcasted_iota(jnp.int32, (tile, hd), 1)

    def body(t, carry):
        off = pl.multiple_of(t * tile, tile)
        block_hot = (tile_lane == t * MOBA_KV_BLOCKS + (tile_row >> blk_shift)).astype(BF16)
        out = []
        for hh, cols in enumerate(head_cols):
            m, acc = carry[2 * hh], carry[2 * hh + 1]
            k_aug = jnp.concatenate([k_ref[pl.ds(off, tile), cols], block_hot], axis=1)
            s = _dot_nt(q_augs[hh], k_aug)
            m_new = jnp.maximum(m, jnp.max(s, axis=1, keepdims=True))
            alpha = jnp.exp2(m - m_new)
            p = jnp.exp2(s - m_new).astype(BF16)
            v_aug = jnp.concatenate([v_ref[pl.ds(off, tile), cols], ones_tile], axis=1)
            out += [m_new, alpha * acc + _dot(p, v_aug)]
        return tuple(out)

    n_tiles = (qi + MOBA_KV_BLOCKS - 1) // MOBA_KV_BLOCKS
    carry = lax.fori_loop(0, n_tiles, body, tuple(carry))
    for hh, cols in enumerate(head_cols):
        acc = carry[2 * hh + 1]
        o_ref[:, cols] = (acc[:, :hd] / acc[:, hd:]).astype(o_ref.dtype)


def _moba(proj):
    n = proj.shape[0]
    n_blocks = n // MOBA_BLOCK
    assert n_blocks % MOBA_KV_BLOCKS == 0 and n_blocks <= LANES
    width = MOBA_HEADS_PER_STEP * MOBA_HEAD_DIM
    return pl.pallas_call(
        functools.partial(_moba_kernel, n_blocks=n_blocks),
        out_shape=jax.ShapeDtypeStruct((n, MOBA_WIDTH), BF16),
        grid=(MOBA_HEADS // MOBA_HEADS_PER_STEP, n_blocks),
        in_specs=[
            pl.BlockSpec((MOBA_BLOCK, width), lambda h, i: (i, COL_MQ // width + h)),
            pl.BlockSpec((n, width), lambda h, i: (0, COL_MK // width + h), pipeline_mode=pl.Buffered(1)),
            pl.BlockSpec((n, width), lambda h, i: (0, COL_MV // width + h), pipeline_mode=pl.Buffered(1)),
        ],
        out_specs=pl.BlockSpec((MOBA_BLOCK, width), lambda h, i: (i, h)),
        scratch_shapes=[pltpu.VMEM((LANES, width), F32)],
        compiler_params=_params("parallel", "arbitrary"),
        name="moba",
    )(proj, proj, proj)


def _gla_kernel(q_ref, k_ref, v_ref, r_ref, h_ref, wlow_ref, wdec_ref, bdec_ref, gn_ref, o_ref, state_ref, *, tb):
    i = pl.program_id(0)
    c = GLA_CHUNK
    n_chunks = tb // c
    chunk_shift = c.bit_length() - 1

    @pl.when(i == 0)
    def _():
        state_ref[...] = jnp.zeros_like(state_ref)

    g_low = _dot(h_ref[...], wlow_ref[...]).astype(BF16)
    z = _dot(g_low, wdec_ref[...]) + bdec_ref[...]
    log_a = jax.nn.log_sigmoid(z) / GLA_TAU

    row = lax.broadcasted_iota(jnp.int32, (tb, tb), 0)
    col = lax.broadcasted_iota(jnp.int32, (tb, tb), 1)
    causal = ((row >> chunk_shift) == (col >> chunk_shift)) & (col <= row)
    tri = (lax.broadcasted_iota(jnp.int32, (c, c), 1) <= lax.broadcasted_iota(jnp.int32, (c, c), 0)).astype(BF16)
    parts = _split_bf16(log_a, 3)
    b_chunks, last_chunks = [], []
    for ci in range(n_chunks):
        rows = slice(ci * c, (ci + 1) * c)
        b_c = sum(_dot(tri, part[rows, :]) for part in parts)
        b_chunks.append(b_c)
        last_chunks.append(jnp.broadcast_to(b_c[c - 1:c, :], b_c.shape))
    b = jnp.concatenate(b_chunks, axis=0)
    b_last = jnp.concatenate(last_chunks, axis=0)

    q_dec = (q_ref[...].astype(F32) * (GLA_DK ** -0.5) * jnp.exp(b)).astype(BF16)
    kf = k_ref[...].astype(F32)
    k_inv = (kf * jnp.exp(-b)).astype(BF16)
    k_end = (kf * jnp.exp(b_last - b)).astype(BF16)
    decay = jnp.exp(b_last)

    for h in range(GLA_HEADS):
        kc = slice(h * GLA_DK, (h + 1) * GLA_DK)
        vc = slice(h * GLA_DV, (h + 1) * GLA_DV)
        v = v_ref[:, vc]
        attn = jnp.where(causal, _dot_nt(q_dec[:, kc], k_inv[:, kc]), 0.0)
        o = _dot(attn.astype(BF16), v)
        state = state_ref[h]
        inter = []
        for ci in range(n_chunks):
            rows = slice(ci * c, (ci + 1) * c)
            inter.append(_dot_nt(q_dec[rows, kc], state.astype(BF16)))
            update = _dot_tn(v[rows, :], k_end[rows, kc])
            state = decay[ci * c:ci * c + 1, kc] * state + update
        state_ref[h] = state
        o = o + jnp.concatenate(inter, axis=0)
        o = _rms(o, gn_ref[...])
        o_ref[:, vc] = (o * jax.nn.silu(r_ref[:, vc].astype(F32))).astype(o_ref.dtype)


def _gla(proj, h, w_low, w_dec, b_dec, gla_norm, *, tb=512):
    n, d = h.shape
    tb = min(tb, n)
    return pl.pallas_call(
        functools.partial(_gla_kernel, tb=tb),
        out_shape=jax.ShapeDtypeStruct((n, GLA_V_WIDTH), BF16),
        grid=(n // tb,),
        in_specs=[
            pl.BlockSpec((tb, GLA_K_WIDTH), lambda i: (i, COL_GQ // GLA_K_WIDTH)),
            pl.BlockSpec((tb, GLA_K_WIDTH), lambda i: (i, COL_GK // GLA_K_WIDTH)),
            pl.BlockSpec((tb, GLA_V_WIDTH), lambda i: (i, COL_GV // GLA_V_WIDTH)),
            pl.BlockSpec((tb, GLA_V_WIDTH), lambda i: (i, COL_GR // GLA_V_WIDTH)),
            pl.BlockSpec((tb, d), lambda i: (i, 0)),
            pl.BlockSpec((d, LANES), lambda i: (0, 0)),
            pl.BlockSpec((LANES, GLA_K_WIDTH), lambda i: (0, 0)),
            pl.BlockSpec((1, GLA_K_WIDTH), lambda i: (0, 0)),
            pl.BlockSpec((1, GLA_DV), lambda i: (0, 0)),
        ],
        out_specs=pl.BlockSpec((tb, GLA_V_WIDTH), lambda i: (i, 0)),
        scratch_shapes=[pltpu.VMEM((GLA_HEADS, GLA_DV, GLA_DK), F32)],
        compiler_params=_params("arbitrary"),
        name="gla",
    )(proj, proj, proj, proj, h, w_low, w_dec, b_dec, gla_norm)


def _pool_kernel(u_ref, halo_ref, w_ref, sc_ref, o_ref, ext_ref, *, tb):
    i = pl.program_id(0)
    ext_ref[pl.ds(POOL_HALO, tb), :] = u_ref[...].astype(F32)
    ext_ref[pl.ds(0, POOL_HALO), :] = jnp.where(i == 0, 0.0, halo_ref[...].astype(F32))
    t = i * tb + lax.broadcasted_iota(jnp.int32, (tb, 1), 0)
    for g, win in enumerate(POOL_WINDOWS):
        cols = slice(g * POOL_GROUP_DIM, (g + 1) * POOL_GROUP_DIM)
        u = ext_ref[pl.ds(POOL_HALO, tb), cols]
        window_sum = u
        for back in range(1, win):
            window_sum = window_sum + ext_ref[pl.ds(POOL_HALO - back, tb), cols]
        count = jnp.minimum(t + 1, win).astype(F32)
        mixed = window_sum / count - u
        y = _dot(mixed.astype(BF16), w_ref[g])
        o_ref[:, cols] = (y * sc_ref[:, cols]).astype(o_ref.dtype)


def _pool(proj, w_pool, pool_scale, *, tb=512):
    n = proj.shape[0]
    tb = min(tb, n)
    halo_per_block = tb // POOL_HALO
    return pl.pallas_call(
        functools.partial(_pool_kernel, tb=tb),
        out_shape=jax.ShapeDtypeStruct((n, POOL_WIDTH), BF16),
        grid=(n // tb,),
        in_specs=[
            pl.BlockSpec((tb, POOL_WIDTH), lambda i: (i, COL_PU // POOL_WIDTH)),
            pl.BlockSpec((POOL_HALO, POOL_WIDTH),
                         lambda i: (jnp.maximum(i * halo_per_block - 1, 0), COL_PU // POOL_WIDTH)),
            pl.BlockSpec(w_pool.shape, lambda i: (0, 0, 0)),
            pl.BlockSpec((1, POOL_WIDTH), lambda i: (0, 0)),
        ],
        out_specs=pl.BlockSpec((tb, POOL_WIDTH), lambda i: (i, 0)),
        scratch_shapes=[pltpu.VMEM((POOL_HALO + tb, POOL_WIDTH), F32)],
        compiler_params=_params("parallel"),
        name="pool",
    )(proj, proj, w_pool, pool_scale)


def _merge_kernel(h_ref, a_ref, b_ref, c_ref, wg0, wg1, wg2, bg0, bg1, bg2, wb0, wb1, wb2, o_ref):
    h = h_ref[...]
    total = None
    for br_ref, wg, bg, wb in ((a_ref, wg0, bg0, wb0), (b_ref, wg1, bg1, wb1), (c_ref, wg2, bg2, wb2)):
        gate = jax.nn.sigmoid(_dot(h, wg[...].astype(BF16)) + bg[...])
        term = gate * _dot(br_ref[...], wb[...].astype(BF16))
        total = term if total is None else total + term
    o_ref[...] = total.astype(o_ref.dtype)


def _merge(h, o_moba, o_gla, o_pool, w_gate, b_gate, w_branch, l, *, tm=1024, tn=256):
    n, d = h.shape
    tm = min(tm, n)
    bw = o_moba.shape[1]
    tiles = d // tn
    row_spec = lambda width: pl.BlockSpec((tm, width), lambda i, j: (i, 0))
    gate_specs = [pl.BlockSpec((None, d, tn), lambda i, j, k=k: (l, 0, k * tiles + j)) for k in range(3)]
    bias_specs = [pl.BlockSpec((1, tn), lambda i, j, k=k: (0, k * tiles + j)) for k in range(3)]
    branch_specs = [pl.BlockSpec((None, None, bw, tn), lambda i, j, k=k: (l, k, 0, j)) for k in range(3)]
    return pl.pallas_call(
        _merge_kernel,
        out_shape=jax.ShapeDtypeStruct((n, d), BF16),
        grid=(n // tm, tiles),
        in_specs=[row_spec(d), row_spec(bw), row_spec(bw), row_spec(bw)] + gate_specs + bias_specs + branch_specs,
        out_specs=pl.BlockSpec((tm, tn), lambda i, j: (i, j)),
        compiler_params=_params("parallel", "arbitrary"),
        name="merge",
    )(h, o_moba, o_gla, o_pool, w_gate, w_gate, w_gate, b_gate, b_gate, b_gate, w_branch, w_branch, w_branch)


def _matmul_residual_kernel(a_ref, w_ref, x_ref, o_ref):
    o_ref[...] = x_ref[...] + _dot(a_ref[...], w_ref[...].astype(BF16))


def _matmul_residual(a, w, l, x, *, tm=1024, tn=1024):
    n, k = a.shape
    d = w.shape[-1]
    tm = min(tm, n)
    return pl.pallas_call(
        _matmul_residual_kernel,
        out_shape=jax.ShapeDtypeStruct((n, d), F32),
        grid=(n // tm, d // tn),
        in_specs=[
            pl.BlockSpec((tm, k), lambda i, j: (i, 0)),
            pl.BlockSpec((None, k, tn), lambda i, j: (l, 0, j)),
            pl.BlockSpec((tm, tn), lambda i, j: (i, j)),
        ],
        out_specs=pl.BlockSpec((tm, tn), lambda i, j: (i, j)),
        compiler_params=_params("parallel", "arbitrary"),
        name="mix_out",
    )(a, w, x)


def _norm_matmul_kernel(x_ref, g_ref, w_ref, o_ref):
    o_ref[...] = _dot(_rms(x_ref[...], g_ref[...]).astype(BF16), w_ref[...].astype(BF16)).astype(o_ref.dtype)


def _mem_kv(mem, g, w_xkv, l):
    n, d = mem.shape
    width = w_xkv.shape[-1]
    return pl.pallas_call(
        _norm_matmul_kernel,
        out_shape=jax.ShapeDtypeStruct((n, width), BF16),
        grid=(1,),
        in_specs=[
            pl.BlockSpec((n, d), lambda i: (0, 0)),
            pl.BlockSpec((1, d), lambda i: (0, 0)),
            pl.BlockSpec((None, d, width), lambda i: (l, 0, 0)),
        ],
        out_specs=pl.BlockSpec((n, width), lambda i: (0, 0)),
        compiler_params=_params("arbitrary"),
        name="mem_kv",
    )(mem, g, w_xkv)


def _xattn_kernel(x_ref, g_ref, wq_ref, kv_ref, wo_ref, gm_ref, wr_ref, br_ref,
                  xo_ref, h2_ref, route_ref, counts_ref, wq_b, wo_b, counts_acc, *, tm):
    i = pl.program_id(0)

    @pl.when(i == 0)
    def _():
        wq_b[...] = wq_ref[...].astype(BF16)
        wo_b[...] = wo_ref[...].astype(BF16)
        counts_acc[...] = jnp.zeros_like(counts_acc)

    x = x_ref[...]
    h = _rms(x, g_ref[...]).astype(BF16)
    q = (_dot(h, wq_b[...]) * (XATTN_HEAD_DIM ** -0.5)).astype(BF16)
    heads = []
    for hd in range(XATTN_HEADS):
        kc = slice(hd * XATTN_HEAD_DIM, (hd + 1) * XATTN_HEAD_DIM)
        vc = slice(XATTN_WIDTH + hd * XATTN_HEAD_DIM, XATTN_WIDTH + (hd + 1) * XATTN_HEAD_DIM)
        s = _dot_nt(q[:, kc], kv_ref[:, kc])
        p = jnp.exp(s - jnp.max(s, axis=1, keepdims=True))
        p = p / jnp.sum(p, axis=1, keepdims=True)
        heads.append(_dot(p.astype(BF16), kv_ref[:, vc]).astype(BF16))
    o = jnp.concatenate(heads, axis=1)
    xn = x + _dot(o, wo_b[...])
    xo_ref[...] = xn
    h2 = _rms(xn, gm_ref[...])
    h2_ref[...] = h2

    h_hi, h_mid = _split_bf16(h2, 2)
    w_hi, w_mid = wr_ref[0], wr_ref[1]
    lg = _dot(h_hi, w_hi) + (_dot(h_hi, w_mid) + _dot(h_mid, w_hi)) + br_ref[...]

    lane = lax.broadcasted_iota(jnp.int32, lg.shape, 1)
    lane_f = lane.astype(F32)
    big = float(LANES)

    def top1(v):
        top = jnp.max(v, axis=1, keepdims=True)
        return top, jnp.min(jnp.where(v == top, lane_f, big), axis=1, keepdims=True)

    g_logit = jnp.where(lane < MOE_GROUPS, lg, NEG_INF)
    g_top, g_idx = top1(g_logit)
    g_w = 1.0 / jnp.sum(jnp.exp(g_logit - g_top), axis=1, keepdims=True)
    first = MOE_GROUPS + MOE_EXPERTS_PER_GROUP * g_idx
    e_logit = jnp.where((lane_f >= first) & (lane_f < first + MOE_EXPERTS_PER_GROUP), lg, NEG_INF)
    e_top1, lane1 = top1(e_logit)
    e_top2, lane2 = top1(jnp.where(lane_f == lane1, NEG_INF, e_logit))
    z = jnp.sum(jnp.exp(e_logit - e_top1), axis=1, keepdims=True)
    p1 = 1.0 / z
    p2 = jnp.exp(e_top2 - e_top1) / z
    w1 = g_w * p1 / (p1 + p2)
    w2 = g_w * p2 / (p1 + p2)

    two_hot = (lane_f == lane1) | (lane_f == lane2)
    row = lax.broadcasted_iota(jnp.int32, (tm, tm), 0)
    col = lax.broadcasted_iota(jnp.int32, (tm, tm), 1)
    before = _dot((col < row).astype(BF16), two_hot.astype(BF16)) + counts_acc[...]
    rank1 = jnp.sum(jnp.where(lane_f == lane1, before, 0.0), axis=1, keepdims=True)
    rank2 = jnp.sum(jnp.where(lane_f == lane2, before, 0.0), axis=1, keepdims=True)
    counts_acc[...] += jnp.sum(two_hot.astype(F32), axis=0, keepdims=True)
    counts_ref[...] = counts_acc[...]

    record = jnp.zeros(lg.shape, F32)
    for slot, val in ((ROUTE_E1, lane1 - MOE_GROUPS), (ROUTE_E2, lane2 - MOE_GROUPS), (ROUTE_W1, w1),
                      (ROUTE_W2, w2), (ROUTE_RANK1, rank1), (ROUTE_RANK2, rank2)):
        record = jnp.where(lane == slot, val, record)
    route_ref[...] = record


def _xattn(x, g, wq, kv, wo, l, g_moe, w_route_split, b_route, *, tm=512):
    n, d = x.shape
    tm = min(tm, n)
    full = lambda a: pl.BlockSpec(a.shape, lambda i: (0,) * a.ndim)
    layer = lambda a: pl.BlockSpec((None,) + a.shape[1:], lambda i: (l,) + (0,) * (a.ndim - 1))
    rows = lambda width: pl.BlockSpec((tm, width), lambda i: (i, 0))
    return pl.pallas_call(
        functools.partial(_xattn_kernel, tm=tm),
        out_shape=(jax.ShapeDtypeStruct((n, d), F32), jax.ShapeDtypeStruct((n, d), F32),
                   jax.ShapeDtypeStruct((n, LANES), F32), jax.ShapeDtypeStruct((1, LANES), F32)),
        grid=(n // tm,),
        in_specs=[rows(d), full(g), layer(wq), full(kv), layer(wo), full(g_moe), full(w_route_split), full(b_route)],
        out_specs=(rows(d), rows(d), rows(LANES), pl.BlockSpec((1, LANES), lambda i: (0, 0))),
        scratch_shapes=[pltpu.VMEM(wq.shape[1:], BF16), pltpu.VMEM(wo.shape[1:], BF16),
                        pltpu.VMEM((1, LANES), F32)],
        compiler_params=_params("arbitrary"),
        name="xattn_route",
    )(x, g, wq, kv, wo, g_moe, w_route_split, b_route)


def _moe_kernel(start_ref, count_ref, fill_ref, x_hbm, w1_hbm, w3_hbm, w2_hbm, y_hbm,
                w1f, w3f, w2f, w1b, w3b, w2b, xbuf, ybuf, wsem, xsem, ysem, *, layer):
    e = pl.program_id(0)
    n_experts = pl.num_programs(0)
    rb = MOE_ROW_BLOCK
    par = e % 2
    first = start_ref[e]
    n_blocks = count_ref[e]

    def rows(row):
        return pl.ds(pl.multiple_of(row, MOE_ROW_ALIGN), rb)

    def w_copies(expert, slot):
        return (pltpu.make_async_copy(w1_hbm.at[layer, expert], w1f.at[slot], wsem.at[0, slot]),
                pltpu.make_async_copy(w3_hbm.at[layer, expert], w3f.at[slot], wsem.at[1, slot]),
                pltpu.make_async_copy(w2_hbm.at[layer, expert], w2f.at[slot], wsem.at[2, slot]))

    def x_copy(row, slot, xs):
        return pltpu.make_async_copy(x_hbm.at[rows(row), :], xbuf.at[slot, xs], xsem.at[slot, xs])

    def y_copy(row, ys):
        return pltpu.make_async_copy(ybuf.at[ys], y_hbm.at[rows(row), :], ysem.at[ys])

    def prefetch(expert, slot):
        for xs in range(MOE_X_SLOTS):
            @pl.when(xs < count_ref[expert])
            def _():
                x_copy(start_ref[expert] + xs * rb, slot, xs).start()
        for copy in w_copies(expert, slot):
            copy.start()

    @pl.when(e == 0)
    def _():
        prefetch(0, 0)
        ybuf[0] = jnp.zeros(ybuf.shape[1:], ybuf.dtype)

        def start(block, carry):
            y_copy(block * rb, 0).start()
            return carry

        def wait(block, carry):
            y_copy(block * rb, 0).wait()
            return carry

        total = y_hbm.shape[0] // rb
        lax.fori_loop(fill_ref[0], total, start, 0)
        lax.fori_loop(fill_ref[0], total, wait, 0)

    @pl.when(e + 1 < n_experts)
    def _():
        prefetch(e + 1, 1 - par)

    for copy in w_copies(e, par):
        copy.wait()
    w1b[...] = w1f[par].astype(BF16)
    w3b[...] = w3f[par].astype(BF16)
    w2b[...] = w2f[par].astype(BF16)

    def body(c, carry):
        xs = c % MOE_X_SLOTS
        ys = c % MOE_Y_SLOTS
        row = first + c * rb

        @pl.when(c < MOE_X_SLOTS)
        def _():
            x_copy(row, par, xs).wait()

        @pl.when(c >= MOE_X_SLOTS)
        def _():
            copy = x_copy(row, par, xs)
            copy.start()
            copy.wait()

        @pl.when(c >= MOE_Y_SLOTS)
        def _():
            y_copy(row - MOE_Y_SLOTS * rb, ys).wait()

        x = xbuf[par, xs].astype(BF16)
        hidden = jax.nn.silu(_dot(x, w1b[...])) * _dot(x, w3b[...])
        ybuf[ys] = _dot(hidden.astype(BF16), w2b[...])
        y_copy(row, ys).start()
        return carry

    lax.fori_loop(0, n_blocks, body, 0)

    for back in range(1, MOE_Y_SLOTS + 1):
        @pl.when(n_blocks >= back)
        def _():
            y_copy(first + (n_blocks - back) * rb, (n_blocks - back) % MOE_Y_SLOTS).wait()


def _moe_experts(row_start, block_count, fill_start, x_rows, w1, w3, w2, l):
    p, d = x_rows.shape
    rb = MOE_ROW_BLOCK
    n_experts, dff = w1.shape[1], w1.shape[-1]
    hbm = pl.BlockSpec(memory_space=pl.ANY)
    grid_spec = pltpu.PrefetchScalarGridSpec(
        num_scalar_prefetch=3,
        grid=(n_experts,),
        in_specs=[hbm, hbm, hbm, hbm],
        out_specs=hbm,
        scratch_shapes=[
            pltpu.VMEM((2, d, dff), F32), pltpu.VMEM((2, d, dff), F32), pltpu.VMEM((2, dff, d), F32),
            pltpu.VMEM((d, dff), BF16), pltpu.VMEM((d, dff), BF16), pltpu.VMEM((dff, d), BF16),
            pltpu.VMEM((2, MOE_X_SLOTS, rb, d), F32), pltpu.VMEM((MOE_Y_SLOTS, rb, d), F32),
            pltpu.SemaphoreType.DMA((3, 2)), pltpu.SemaphoreType.DMA((2, MOE_X_SLOTS)),
            pltpu.SemaphoreType.DMA((MOE_Y_SLOTS,)),
        ],
    )
    return pl.pallas_call(
        functools.partial(_moe_kernel, layer=l),
        out_shape=jax.ShapeDtypeStruct((p, d), F32),
        grid_spec=grid_spec,
        compiler_params=_params("arbitrary"),
        name="moe_experts",
    )(row_start, block_count, fill_start, x_rows, w1, w3, w2)


DISPATCH_DEST1, DISPATCH_DEST2 = 0, 1
META_ROW_START, META_ROW_COUNT = 0, 1


def _dispatch_kernel(route_ref, counts_ref, dest_ref, meta_ref):
    align = float(MOE_ROW_ALIGN)
    counts = jnp.broadcast_to(counts_ref[...], (8, LANES))
    padded = jnp.floor((counts + (align - 1.0)) * (1.0 / align)) * align
    src = lax.broadcasted_iota(jnp.int32, (LANES, LANES), 0)
    dst = lax.broadcasted_iota(jnp.int32, (LANES, LANES), 1)
    earlier = (src < dst).astype(BF16)
    seg_start = sum(_dot(part, earlier) for part in _split_bf16(padded, 3))[0:1]

    route = route_ref[...]
    lane = lax.broadcasted_iota(jnp.int32, route.shape, 1)
    lane_f = lane.astype(F32)

    def field(slot):
        return jnp.sum(jnp.where(lane == slot, route, 0.0), axis=1, keepdims=True)

    def dest(e_slot, rank_slot):
        hit = lane_f == field(e_slot) + float(MOE_GROUPS)
        return jnp.sum(jnp.where(hit, seg_start, 0.0), axis=1, keepdims=True) + field(rank_slot)

    dest1 = dest(ROUTE_E1, ROUTE_RANK1)
    dest2 = dest(ROUTE_E2, ROUTE_RANK2)
    dest_ref[...] = jnp.where(lane == DISPATCH_DEST1, dest1, jnp.where(lane == DISPATCH_DEST2, dest2, 0.0))

    sub = lax.broadcasted_iota(jnp.int32, (8, LANES), 0)
    meta_ref[...] = jnp.where(sub == META_ROW_START, jnp.broadcast_to(seg_start, (8, LANES)),
                              jnp.where(sub == META_ROW_COUNT, padded, 0.0))


def _dispatch(route, counts, *, tm=1024):
    n = route.shape[0]
    tm = min(tm, n)
    return pl.pallas_call(
        _dispatch_kernel,
        out_shape=(jax.ShapeDtypeStruct((n, LANES), F32), jax.ShapeDtypeStruct((8, LANES), F32)),
        grid=(n // tm,),
        in_specs=[pl.BlockSpec((tm, LANES), lambda i: (i, 0)), pl.BlockSpec((1, LANES), lambda i: (0, 0))],
        out_specs=(pl.BlockSpec((tm, LANES), lambda i: (i, 0)), pl.BlockSpec((8, LANES), lambda i: (0, 0))),
        compiler_params=_params("arbitrary"),
        name="moe_dispatch",
    )(route, counts)


def _dispatch_layout(route, counts):
    n = route.shape[0]
    rb = MOE_ROW_BLOCK
    dest, meta = _dispatch(route, counts)
    dest_pair = jnp.concatenate([dest[:, DISPATCH_DEST1], dest[:, DISPATCH_DEST2]]).astype(jnp.int32)
    experts = slice(MOE_GROUPS, MOE_GROUPS + MOE_EXPERTS)
    row_start = meta[META_ROW_START, experts].astype(jnp.int32)
    row_count = meta[META_ROW_COUNT, experts].astype(jnp.int32)
    block_count = (row_count + rb - 1) // rb
    fill_start = (row_start[-1:] + row_count[-1:]) // rb
    n_rows = (n * MOE_TOPK + MOE_EXPERTS * (MOE_ROW_ALIGN - 1) + rb - 1) // rb * rb + rb
    tok = jnp.arange(n, dtype=jnp.int32)
    row_tok = (jnp.arange(n_rows, dtype=jnp.int32) % n).at[dest_pair].set(
        jnp.concatenate([tok, tok]), mode="promise_in_bounds", unique_indices=True)
    return row_tok, row_start, block_count, fill_start, dest_pair


def _combine_kernel(x_ref, y1_ref, y2_ref, route_ref, g_ref, o_ref, *, final_norm):
    route = route_ref[...]
    w1 = route[:, ROUTE_W1:ROUTE_W1 + 1]
    w2 = route[:, ROUTE_W2:ROUTE_W2 + 1]
    out = x_ref[...] + (w1 * y1_ref[...] + w2 * y2_ref[...])
    o_ref[...] = _rms(out, g_ref[...]) if final_norm else out


def _combine(x, y_pair, route, g, *, final_norm, tm=512):
    n, d = x.shape
    tm = min(tm, n)
    steps = n // tm
    rows = lambda width: pl.BlockSpec((tm, width), lambda i: (i, 0))
    return pl.pallas_call(
        functools.partial(_combine_kernel, final_norm=final_norm),
        out_shape=jax.ShapeDtypeStruct((n, d), F32),
        grid=(steps,),
        in_specs=[rows(d), rows(d), pl.BlockSpec((tm, d), lambda i: (steps + i, 0)), rows(LANES),
                  pl.BlockSpec((1, d), lambda i: (0, 0))],
        out_specs=rows(d),
        compiler_params=_params("parallel"),
        name="moe_combine",
    )(x, y_pair, y_pair, route, g)


def _rope_tables(positions):
    inv = 1.0 / (ROPE_THETA ** (jnp.arange(0, MOBA_HEAD_DIM, 2, dtype=F32) / MOBA_HEAD_DIM))
    ang = positions.astype(F32)[:, None] * inv
    cos, sin = jnp.cos(ang), jnp.sin(ang)
    return jnp.concatenate([cos, cos], axis=-1), jnp.concatenate([-sin, sin], axis=-1)


def _layer(x, mem, cos_f, sin_f, p, l, norm_final):
    w_in = p["w_in"]
    w_pool_in = w_in[l, :, W_IN_POOL:]
    w_low = jnp.pad(w_in[l, :, W_IN_LOW:W_IN_POOL], ((0, 0), (0, LANES - GLA_RANK)))
    proj, h = _inproj(x, _layer_vec(p["norm_mix"], l), w_in, w_pool_in, l, cos_f, sin_f)
    o_moba = _moba(proj)
    w_dec = jnp.pad(p["w_gla_decay"][l], ((0, LANES - GLA_RANK), (0, 0))).astype(BF16)
    o_gla = _gla(proj, h, w_low, w_dec, _layer_vec(p["b_gla_decay"], l), _layer_vec(p["gla_norm"], l))
    o_pool = _pool(proj, p["w_pool"][l].astype(BF16), _layer_vec(p["pool_scale"], l))
    merged = _merge(h, o_moba, o_gla, o_pool, p["w_gate"], _layer_vec(p["b_gate"], l), p["w_branch"], l)
    x = _matmul_residual(merged, p["w_mix_out"], l, x)

    kv = _mem_kv(mem, _layer_vec(p["norm_mem"], l), p["w_xkv"], l)
    w_route = jnp.concatenate([p["w_route_group"][l], p["w_route_expert"][l]], axis=1)
    n_route = w_route.shape[1]
    w_route_split = jnp.stack(_split_bf16(jnp.pad(w_route, ((0, 0), (0, LANES - n_route))), 2))
    b_route = jnp.pad(jnp.concatenate([p["b_route_group"][l], p["b_route_expert"][l]]), (0, LANES - n_route))
    x, h2, route, counts = _xattn(x, _layer_vec(p["norm_xattn"], l), p["w_xq"], kv, p["w_xo"], l,
                                  _layer_vec(p["norm_moe"], l), w_route_split, b_route.reshape(1, -1))

    row_tok, row_start, block_count, fill_start, dest_pair = _dispatch_layout(route, counts)
    x_rows = h2.at[row_tok].get(mode="promise_in_bounds")
    y_rows = _moe_experts(row_start, block_count, fill_start, x_rows,
                          p["w_exp_gate"], p["w_exp_up"], p["w_exp_down"], l)
    y_pair = y_rows.at[dest_pair].get(mode="promise_in_bounds")
    return _combine(x, y_pair, route, norm_final.reshape(1, -1), final_norm=l == p["norm_mix"].shape[0] - 1)


def kernel(x, mem, positions, norm_mix, w_in, w_gla_decay, b_gla_decay, gla_norm, w_pool, pool_scale, w_branch, w_gate, b_gate, w_mix_out, norm_xattn, norm_mem, w_xq, w_xkv, w_xo, norm_moe, w_route_group, b_route_group, w_route_expert, b_route_expert, w_exp_gate, w_exp_up, w_exp_down, norm_final):
    batch, seq, d = x.shape
    assert batch == 1, "kernels are written for a single sequence"
    params = dict(norm_mix=norm_mix, w_in=w_in.astype(BF16), w_gla_decay=w_gla_decay, b_gla_decay=b_gla_decay,
                  gla_norm=gla_norm, w_pool=w_pool, pool_scale=pool_scale, w_branch=w_branch, w_gate=w_gate,
                  b_gate=b_gate, w_mix_out=w_mix_out, norm_xattn=norm_xattn, norm_mem=norm_mem, w_xq=w_xq,
                  w_xkv=w_xkv, w_xo=w_xo, norm_moe=norm_moe, w_route_group=w_route_group,
                  b_route_group=b_route_group, w_route_expert=w_route_expert, b_route_expert=b_route_expert,
                  w_exp_gate=w_exp_gate, w_exp_up=w_exp_up, w_exp_down=w_exp_down)
    cos_f, sin_f = _rope_tables(positions[0])
    xs = x[0]
    for l in range(norm_mix.shape[0]):
        xs = _layer(xs, mem[0], cos_f, sin_f, params, l, norm_final)
    return xs.reshape(batch, seq, d)
```

```python
import functools

import jax
import jax.numpy as jnp
from jax import lax
from jax.experimental import pallas as pl
from jax.experimental.pallas import tpu as pltpu

F32 = jnp.float32
BF16 = jnp.bfloat16

EPS = 1e-6
NEG_INF = -1e30

MOBA_HEADS = 8
MOBA_HEAD_DIM = 128
MOBA_WIDTH = MOBA_HEADS * MOBA_HEAD_DIM
MOBA_BLOCK = 256
MOBA_TOPK = 3
MOBA_KV_BLOCKS = 4
MOBA_HEADS_PER_STEP = 8
MOBA_Q_BLOCKS = 1
ROPE_THETA = 10000.0
MOBA_Q_SCALE = MOBA_HEAD_DIM ** -0.5 * 1.4426950408889634

GLA_HEADS = 4
GLA_DK = 128
GLA_DV = 256
GLA_K_WIDTH = GLA_HEADS * GLA_DK
GLA_V_WIDTH = GLA_HEADS * GLA_DV
GLA_RANK = 16
GLA_TAU = 16.0
GLA_CHUNK = 64

POOL_WINDOWS = (2, 4, 8, 16)
POOL_GROUP_DIM = 256
POOL_WIDTH = len(POOL_WINDOWS) * POOL_GROUP_DIM
POOL_HALO = 16

XATTN_HEADS = 4
XATTN_HEAD_DIM = 128
XATTN_WIDTH = XATTN_HEADS * XATTN_HEAD_DIM

MOE_GROUPS = 8
MOE_EXPERTS_PER_GROUP = 8
MOE_EXPERTS = MOE_GROUPS * MOE_EXPERTS_PER_GROUP
MOE_TOPK = 2
MOE_ROW_BLOCK = 128
MOE_ROW_ALIGN = 8
MOE_X_SLOTS = 4
MOE_Y_SLOTS = 4

LANES = 128

COL_MQ = 0
COL_MK = COL_MQ + MOBA_WIDTH
COL_MV = COL_MK + MOBA_WIDTH
COL_GQ = COL_MV + MOBA_WIDTH
COL_GK = COL_GQ + GLA_K_WIDTH
COL_GV = COL_GK + GLA_K_WIDTH
COL_GR = COL_GV + GLA_V_WIDTH
COL_PU = COL_GR + GLA_V_WIDTH
PROJ_WIDTH = COL_PU + POOL_WIDTH
W_IN_LOW = COL_PU
W_IN_POOL = COL_PU + GLA_RANK

ROUTE_E1, ROUTE_E2, ROUTE_W1, ROUTE_W2, ROUTE_RANK1, ROUTE_RANK2 = range(6)

VMEM_LIMIT = 56 * 1024 * 1024


def _params(*semantics):
    return pltpu.CompilerParams(dimension_semantics=semantics, vmem_limit_bytes=VMEM_LIMIT)


def _rms(x, g):
    return x * lax.rsqrt(jnp.mean(x * x, axis=-1, keepdims=True) + EPS) * g


def _dot(a, b):
    return jnp.dot(a, b, preferred_element_type=F32)


def _dot_nt(a, b):
    return lax.dot_general(a, b, (((1,), (1,)), ((), ())), preferred_element_type=F32)


def _dot_tn(a, b):
    return lax.dot_general(a, b, (((0,), (0,)), ((), ())), preferred_element_type=F32)


def _split_bf16(x, terms):
    parts = []
    for _ in range(terms):
        part = x.astype(BF16)
        parts.append(part)
        x = x - part.astype(F32)
    return parts


def _layer_vec(v, l):
    return v[l].reshape(1, -1)


def _inproj_kernel(x_ref, g_ref, w_ref, wp_ref, cos_ref, sin_ref, o_ref, h_ref, *, tn, n_main):
    j = pl.program_id(1)

    @pl.when(j == 0)
    def _():
        h_ref[...] = _rms(x_ref[...], g_ref[...]).astype(BF16)

    n_rope = (2 * MOBA_WIDTH) // tn
    n_q = MOBA_WIDTH // tn

    @pl.when(j < n_rope)
    def _():
        acc = _dot(h_ref[...], w_ref[...])
        scale = jnp.where(j < n_q, MOBA_Q_SCALE, 1.0).astype(F32)
        cos = cos_ref[...] * scale
        sin = sin_ref[...] * scale
        for hh in range(tn // MOBA_HEAD_DIM):
            cols = slice(hh * MOBA_HEAD_DIM, (hh + 1) * MOBA_HEAD_DIM)
            a = acc[:, cols]
            rot = pltpu.roll(a, MOBA_HEAD_DIM // 2, axis=1)
            o_ref[:, cols] = (a * cos + rot * sin).astype(o_ref.dtype)

    @pl.when((j >= n_rope) & (j < n_main))
    def _():
        o_ref[...] = _dot(h_ref[...], w_ref[...]).astype(o_ref.dtype)

    @pl.when(j >= n_main)
    def _():
        o_ref[...] = _dot(h_ref[...], wp_ref[...]).astype(o_ref.dtype)


def _inproj(x, g, w_in, w_pool_in, l, cos_f, sin_f, *, tm=1024, tn=1024):
    n, d = x.shape
    tm = min(tm, n)
    n_main = COL_PU // tn
    n_tiles = PROJ_WIDTH // tn
    return pl.pallas_call(
        functools.partial(_inproj_kernel, tn=tn, n_main=n_main),
        out_shape=(jax.ShapeDtypeStruct((n, PROJ_WIDTH), BF16), jax.ShapeDtypeStruct((n, d), BF16)),
        grid=(n // tm, n_tiles),
        in_specs=[
            pl.BlockSpec((tm, d), lambda i, j: (i, 0)),
            pl.BlockSpec((1, d), lambda i, j: (0, 0)),
            pl.BlockSpec((None, d, tn), lambda i, j: (l, 0, jnp.minimum(j, n_main - 1))),
            pl.BlockSpec((d, tn), lambda i, j: (0, jnp.maximum(j - n_main, 0))),
            pl.BlockSpec((tm, MOBA_HEAD_DIM), lambda i, j: (i, 0)),
            pl.BlockSpec((tm, MOBA_HEAD_DIM), lambda i, j: (i, 0)),
        ],
        out_specs=(
            pl.BlockSpec((tm, tn), lambda i, j: (i, j)),
            pl.BlockSpec((tm, d), lambda i, j: (i, 0)),
        ),
        compiler_params=_params("parallel", "arbitrary"),
        name="inproj",
    )(x, g, w_in, w_pool_in, cos_f, sin_f)


def _moba_kernel(q_ref, k_ref, v_ref, o_ref, kmean_ref, *, n_blocks):
    qi = pl.program_id(1)
    blk = MOBA_BLOCK
    hd = MOBA_HEAD_DIM
    q_rows = MOBA_Q_BLOCKS * blk
    tile = MOBA_KV_BLOCKS * blk
    blk_shift = blk.bit_length() - 1
    head_cols = [slice(hh * hd, (hh + 1) * hd) for hh in range(MOBA_HEADS_PER_STEP)]
    first_block = qi * MOBA_Q_BLOCKS

    @pl.when(qi == 0)
    def _():
        kmean_ref[...] = jnp.zeros_like(kmean_ref)

        def fill(b, carry):
            kb = k_ref[pl.ds(pl.multiple_of(b * blk, blk), blk), :].astype(F32)
            kmean_ref[pl.ds(b, 1), :] = jnp.mean(kb, axis=0, keepdims=True)
            return carry

        lax.fori_loop(0, n_blocks, fill, 0)

    lane = lax.broadcasted_iota(jnp.int32, (q_rows, LANES), 1)
    lane_f = lane.astype(F32)
    own_block = first_block + (lax.broadcasted_iota(jnp.int32, (q_rows, LANES), 0) >> blk_shift)
    row = lax.broadcasted_iota(jnp.int32, (blk, blk), 0)
    col = lax.broadcasted_iota(jnp.int32, (blk, blk), 1)
    ones_own = jnp.ones((blk, hd), BF16)
    ones_tile = jnp.ones((tile, hd), BF16)

    q_augs, carry = [], []
    for cols in head_cols:
        q = q_ref[:, cols]
        kmean_hi, kmean_mid = _split_bf16(kmean_ref[:, cols], 2)
        gate = _dot_nt(q, kmean_hi) + _dot_nt(q, kmean_mid)
        gate = jnp.where(lane < own_block, gate, NEG_INF)
        sel = jnp.zeros(gate.shape, jnp.bool_)
        for _ in range(MOBA_TOPK):
            top = jnp.max(gate, axis=1, keepdims=True)
            idx = jnp.min(jnp.where(gate == top, lane_f, float(LANES)), axis=1, keepdims=True)
            hit = lane_f == idx
            sel = sel | (hit & (top > 0.5 * NEG_INF))
            gate = jnp.where(hit, NEG_INF, gate)
        q_augs.append(jnp.concatenate([q, jnp.where(sel, 0.0, NEG_INF).astype(BF16)], axis=1))

        ms, accs = [], []
        for qb in range(MOBA_Q_BLOCKS):
            start = pl.multiple_of((first_block + qb) * blk, blk)
            s = _dot_nt(q[qb * blk:(qb + 1) * blk, :], k_ref[pl.ds(start, blk), cols])
            s = jnp.where(col <= row, s, NEG_INF)
            m = jnp.max(s, axis=1, keepdims=True)
            p = jnp.exp2(s - m).astype(BF16)
            ms.append(m)
            accs.append(_dot(p, jnp.concatenate([v_ref[pl.ds(start, blk), cols], ones_own], axis=1)))
        carry += [jnp.concatenate(ms, axis=0), jnp.concatenate(accs, axis=0)]

    tile_row = lax.broadcasted_iota(jnp.int32, (tile, hd), 0)
    tile_lane = lax.broadcasted_iota(jnp.int32, (tile, hd), 1)

    def body(t, carry):
        off = pl.multiple_of(t * tile, tile)
        block_hot = (tile_lane == t * MOBA_KV_BLOCKS + (tile_row >> blk_shift)).astype(BF16)
        out = []
        for hh, cols in enumerate(head_cols):
            m, acc = carry[2 * hh], carry[2 * hh + 1]
            k_aug = jnp.concatenate([k_ref[pl.ds(off, tile), cols], block_hot], axis=1)
            s = _dot_nt(q_augs[hh], k_aug)
            m_new = jnp.maximum(m, jnp.max(s, axis=1, keepdims=True))
            alpha = jnp.exp2(m - m_new)
            p = jnp.exp2(s - m_new).astype(BF16)
            v_aug = jnp.concatenate([v_ref[pl.ds(off, tile), cols], ones_tile], axis=1)
            out += [m_new, alpha * acc + _dot(p, v_aug)]
        return tuple(out)

    n_tiles = (first_block + MOBA_Q_BLOCKS - 1 + MOBA_KV_BLOCKS - 1) // MOBA_KV_BLOCKS
    carry = lax.fori_loop(0, n_tiles, body, tuple(carry))
    for hh, cols in enumerate(head_cols):
        acc = carry[2 * hh + 1]
        o_ref[:, cols] = (acc[:, :hd] / acc[:, hd:]).astype(o_ref.dtype)


def _moba(proj):
    n = proj.shape[0]
    n_blocks = n // MOBA_BLOCK
    assert n_blocks % MOBA_KV_BLOCKS == 0 and n_blocks % MOBA_Q_BLOCKS == 0 and n_blocks <= LANES
    width = MOBA_HEADS_PER_STEP * MOBA_HEAD_DIM
    q_rows = MOBA_Q_BLOCKS * MOBA_BLOCK
    return pl.pallas_call(
        functools.partial(_moba_kernel, n_blocks=n_blocks),
        out_shape=jax.ShapeDtypeStruct((n, MOBA_WIDTH), BF16),
        grid=(MOBA_HEADS // MOBA_HEADS_PER_STEP, n_blocks // MOBA_Q_BLOCKS),
        in_specs=[
            pl.BlockSpec((q_rows, width), lambda h, i: (i, COL_MQ // width + h)),
            pl.BlockSpec((n, width), lambda h, i: (0, COL_MK // width + h), pipeline_mode=pl.Buffered(1)),
            pl.BlockSpec((n, width), lambda h, i: (0, COL_MV // width + h), pipeline_mode=pl.Buffered(1)),
        ],
        out_specs=pl.BlockSpec((q_rows, width), lambda h, i: (i, h)),
        scratch_shapes=[pltpu.VMEM((LANES, width), F32)],
        compiler_params=_params("parallel", "arbitrary"),
        name="moba",
    )(proj, proj, proj)


def _gla_kernel(q_ref, k_ref, v_ref, r_ref, h_ref, wlow_ref, wdec_ref, bdec_ref, gn_ref, o_ref, state_ref, *, tb):
    i = pl.program_id(0)
    c = GLA_CHUNK
    n_chunks = tb // c
    chunk_shift = c.bit_length() - 1

    @pl.when(i == 0)
    def _():
        state_ref[...] = jnp.zeros_like(state_ref)

    g_low = _dot(h_ref[...], wlow_ref[...]).astype(BF16)
    z = _dot(g_low, wdec_ref[...]) + bdec_ref[...]
    log_a = jax.nn.log_sigmoid(z) / GLA_TAU

    row = lax.broadcasted_iota(jnp.int32, (tb, tb), 0)
    col = lax.broadcasted_iota(jnp.int32, (tb, tb), 1)
    causal = ((row >> chunk_shift) == (col >> chunk_shift)) & (col <= row)
    tri = (lax.broadcasted_iota(jnp.int32, (c, c), 1) <= lax.broadcasted_iota(jnp.int32, (c, c), 0)).astype(BF16)
    parts = _split_bf16(log_a, 3)
    b_chunks, last_chunks = [], []
    for ci in range(n_chunks):
        rows = slice(ci * c, (ci + 1) * c)
        b_c = sum(_dot(tri, part[rows, :]) for part in parts)
        b_chunks.append(b_c)
        last_chunks.append(jnp.broadcast_to(b_c[c - 1:c, :], b_c.shape))
    b = jnp.concatenate(b_chunks, axis=0)
    b_last = jnp.concatenate(last_chunks, axis=0)

    q_dec = (q_ref[...].astype(F32) * (GLA_DK ** -0.5) * jnp.exp(b)).astype(BF16)
    kf = k_ref[...].astype(F32)
    k_inv = (kf * jnp.exp(-b)).astype(BF16)
    k_end = (kf * jnp.exp(b_last - b)).astype(BF16)
    decay = jnp.exp(b_last)

    for h in range(GLA_HEADS):
        kc = slice(h * GLA_DK, (h + 1) * GLA_DK)
        vc = slice(h * GLA_DV, (h + 1) * GLA_DV)
        v = v_ref[:, vc]
        attn = jnp.where(causal, _dot_nt(q_dec[:, kc], k_inv[:, kc]), 0.0)
        o = _dot(attn.astype(BF16), v)
        state = state_ref[h]
        inter = []
        for ci in range(n_chunks):
            rows = slice(ci * c, (ci + 1) * c)
            inter.append(_dot_nt(q_dec[rows, kc], state.astype(BF16)))
            update = _dot_tn(v[rows, :], k_end[rows, kc])
            state = decay[ci * c:ci * c + 1, kc] * state + update
        state_ref[h] = state
        o = o + jnp.concatenate(inter, axis=0)
        o = _rms(o, gn_ref[...])
        o_ref[:, vc] = (o * jax.nn.silu(r_ref[:, vc].astype(F32))).astype(o_ref.dtype)


def _gla(proj, h, w_low, w_dec, b_dec, gla_norm, *, tb=512):
    n, d = h.shape
    tb = min(tb, n)
    return pl.pallas_call(
        functools.partial(_gla_kernel, tb=tb),
        out_shape=jax.ShapeDtypeStruct((n, GLA_V_WIDTH), BF16),
        grid=(n // tb,),
        in_specs=[
            pl.BlockSpec((tb, GLA_K_WIDTH), lambda i: (i, COL_GQ // GLA_K_WIDTH)),
            pl.BlockSpec((tb, GLA_K_WIDTH), lambda i: (i, COL_GK // GLA_K_WIDTH)),
            pl.BlockSpec((tb, GLA_V_WIDTH), lambda i: (i, COL_GV // GLA_V_WIDTH)),
            pl.BlockSpec((tb, GLA_V_WIDTH), lambda i: (i, COL_GR // GLA_V_WIDTH)),
            pl.BlockSpec((tb, d), lambda i: (i, 0)),
            pl.BlockSpec((d, LANES), lambda i: (0, 0)),
            pl.BlockSpec((LANES, GLA_K_WIDTH), lambda i: (0, 0)),
            pl.BlockSpec((1, GLA_K_WIDTH), lambda i: (0, 0)),
            pl.BlockSpec((1, GLA_DV), lambda i: (0, 0)),
        ],
        out_specs=pl.BlockSpec((tb, GLA_V_WIDTH), lambda i: (i, 0)),
        scratch_shapes=[pltpu.VMEM((GLA_HEADS, GLA_DV, GLA_DK), F32)],
        compiler_params=_params("arbitrary"),
        name="gla",
    )(proj, proj, proj, proj, h, w_low, w_dec, b_dec, gla_norm)


def _pool_kernel(u_ref, halo_ref, w_ref, sc_ref, o_ref, ext_ref, *, tb):
    i = pl.program_id(0)
    ext_ref[pl.ds(POOL_HALO, tb), :] = u_ref[...].astype(F32)
    ext_ref[pl.ds(0, POOL_HALO), :] = jnp.where(i == 0, 0.0, halo_ref[...].astype(F32))
    t = i * tb + lax.broadcasted_iota(jnp.int32, (tb, 1), 0)
    for g, win in enumerate(POOL_WINDOWS):
        cols = slice(g * POOL_GROUP_DIM, (g + 1) * POOL_GROUP_DIM)
        u = ext_ref[pl.ds(POOL_HALO, tb), cols]
        window_sum = u
        for back in range(1, win):
            window_sum = window_sum + ext_ref[pl.ds(POOL_HALO - back, tb), cols]
        count = jnp.minimum(t + 1, win).astype(F32)
        mixed = window_sum / count - u
        y = _dot(mixed.astype(BF16), w_ref[g])
        o_ref[:, cols] = (y * sc_ref[:, cols]).astype(o_ref.dtype)


def _pool(proj, w_pool, pool_scale, *, tb=512):
    n = proj.shape[0]
    tb = min(tb, n)
    halo_per_block = tb // POOL_HALO
    return pl.pallas_call(
        functools.partial(_pool_kernel, tb=tb),
        out_shape=jax.ShapeDtypeStruct((n, POOL_WIDTH), BF16),
        grid=(n // tb,),
        in_specs=[
            pl.BlockSpec((tb, POOL_WIDTH), lambda i: (i, COL_PU // POOL_WIDTH)),
            pl.BlockSpec((POOL_HALO, POOL_WIDTH),
                         lambda i: (jnp.maximum(i * halo_per_block - 1, 0), COL_PU // POOL_WIDTH)),
            pl.BlockSpec(w_pool.shape, lambda i: (0, 0, 0)),
            pl.BlockSpec((1, POOL_WIDTH), lambda i: (0, 0)),
        ],
        out_specs=pl.BlockSpec((tb, POOL_WIDTH), lambda i: (i, 0)),
        scratch_shapes=[pltpu.VMEM((POOL_HALO + tb, POOL_WIDTH), F32)],
        compiler_params=_params("parallel"),
        name="pool",
    )(proj, proj, w_pool, pool_scale)


def _merge_kernel(h_ref, a_ref, b_ref, c_ref, wg0, wg1, wg2, bg0, bg1, bg2, wb0, wb1, wb2, o_ref):
    h = h_ref[...]
    total = None
    for br_ref, wg, bg, wb in ((a_ref, wg0, bg0, wb0), (b_ref, wg1, bg1, wb1), (c_ref, wg2, bg2, wb2)):
        gate = jax.nn.sigmoid(_dot(h, wg[...].astype(BF16)) + bg[...])
        term = gate * _dot(br_ref[...], wb[...].astype(BF16))
        total = term if total is None else total + term
    o_ref[...] = total.astype(o_ref.dtype)


def _merge(h, o_moba, o_gla, o_pool, w_gate, b_gate, w_branch, l, *, tm=1024, tn=256):
    n, d = h.shape
    tm = min(tm, n)
    bw = o_moba.shape[1]
    tiles = d // tn
    row_spec = lambda width: pl.BlockSpec((tm, width), lambda i, j: (i, 0))
    gate_specs = [pl.BlockSpec((None, d, tn), lambda i, j, k=k: (l, 0, k * tiles + j)) for k in range(3)]
    bias_specs = [pl.BlockSpec((1, tn), lambda i, j, k=k: (0, k * tiles + j)) for k in range(3)]
    branch_specs = [pl.BlockSpec((None, None, bw, tn), lambda i, j, k=k: (l, k, 0, j)) for k in range(3)]
    return pl.pallas_call(
        _merge_kernel,
        out_shape=jax.ShapeDtypeStruct((n, d), BF16),
        grid=(n // tm, tiles),
        in_specs=[row_spec(d), row_spec(bw), row_spec(bw), row_spec(bw)] + gate_specs + bias_specs + branch_specs,
        out_specs=pl.BlockSpec((tm, tn), lambda i, j: (i, j)),
        compiler_params=_params("parallel", "arbitrary"),
        name="merge",
    )(h, o_moba, o_gla, o_pool, w_gate, w_gate, w_gate, b_gate, b_gate, b_gate, w_branch, w_branch, w_branch)


def _matmul_residual_kernel(a_ref, w_ref, x_ref, o_ref):
    o_ref[...] = x_ref[...] + _dot(a_ref[...], w_ref[...].astype(BF16))


def _matmul_residual(a, w, l, x, *, tm=2048, tn=512):
    n, k = a.shape
    d = w.shape[-1]
    tm = min(tm, n)
    return pl.pallas_call(
        _matmul_residual_kernel,
        out_shape=jax.ShapeDtypeStruct((n, d), F32),
        grid=(n // tm, d // tn),
        in_specs=[
            pl.BlockSpec((tm, k), lambda i, j: (i, 0)),
            pl.BlockSpec((None, k, tn), lambda i, j: (l, 0, j)),
            pl.BlockSpec((tm, tn), lambda i, j: (i, j)),
        ],
        out_specs=pl.BlockSpec((tm, tn), lambda i, j: (i, j)),
        compiler_params=_params("parallel", "arbitrary"),
        name="mix_out",
    )(a, w, x)


def _norm_matmul_kernel(x_ref, g_ref, w_ref, o_ref):
    o_ref[...] = _dot(_rms(x_ref[...], g_ref[...]).astype(BF16), w_ref[...].astype(BF16)).astype(o_ref.dtype)


def _mem_kv(mem, g, w_xkv, l):
    n, d = mem.shape
    width = w_xkv.shape[-1]
    return pl.pallas_call(
        _norm_matmul_kernel,
        out_shape=jax.ShapeDtypeStruct((n, width), BF16),
        grid=(1,),
        in_specs=[
            pl.BlockSpec((n, d), lambda i: (0, 0)),
            pl.BlockSpec((1, d), lambda i: (0, 0)),
            pl.BlockSpec((None, d, width), lambda i: (l, 0, 0)),
        ],
        out_specs=pl.BlockSpec((n, width), lambda i: (0, 0)),
        compiler_params=_params("arbitrary"),
        name="mem_kv",
    )(mem, g, w_xkv)


def _xattn_kernel(x_ref, g_ref, wq_ref, kv_ref, wo_ref, gm_ref, wr_ref, br_ref,
                  xo_ref, h2_ref, route_ref, counts_ref, wq_b, wo_b, counts_acc, *, tm):
    i = pl.program_id(0)

    @pl.when(i == 0)
    def _():
        wq_b[...] = wq_ref[...].astype(BF16)
        wo_b[...] = wo_ref[...].astype(BF16)
        counts_acc[...] = jnp.zeros_like(counts_acc)

    x = x_ref[...]
    h = _rms(x, g_ref[...]).astype(BF16)
    q = (_dot(h, wq_b[...]) * (XATTN_HEAD_DIM ** -0.5)).astype(BF16)
    heads = []
    for hd in range(XATTN_HEADS):
        kc = slice(hd * XATTN_HEAD_DIM, (hd + 1) * XATTN_HEAD_DIM)
        vc = slice(XATTN_WIDTH + hd * XATTN_HEAD_DIM, XATTN_WIDTH + (hd + 1) * XATTN_HEAD_DIM)
        s = _dot_nt(q[:, kc], kv_ref[:, kc])
        p = jnp.exp(s - jnp.max(s, axis=1, keepdims=True))
        p = p / jnp.sum(p, axis=1, keepdims=True)
        heads.append(_dot(p.astype(BF16), kv_ref[:, vc]).astype(BF16))
    o = jnp.concatenate(heads, axis=1)
    xn = x + _dot(o, wo_b[...])
    xo_ref[...] = xn
    h2 = _rms(xn, gm_ref[...])
    h2_ref[...] = h2

    h_hi, h_mid = _split_bf16(h2, 2)
    w_hi, w_mid = wr_ref[0], wr_ref[1]
    lg = _dot(h_hi, w_hi) + (_dot(h_hi, w_mid) + _dot(h_mid, w_hi)) + br_ref[...]

    lane = lax.broadcasted_iota(jnp.int32, lg.shape, 1)
    lane_f = lane.astype(F32)
    big = float(LANES)

    def top1(v):
        top = jnp.max(v, axis=1, keepdims=True)
        return top, jnp.min(jnp.where(v == top, lane_f, big), axis=1, keepdims=True)

    g_logit = jnp.where(lane < MOE_GROUPS, lg, NEG_INF)
    g_top, g_idx = top1(g_logit)
    g_w = 1.0 / jnp.sum(jnp.exp(g_logit - g_top), axis=1, keepdims=True)
    first = MOE_GROUPS + MOE_EXPERTS_PER_GROUP * g_idx
    e_logit = jnp.where((lane_f >= first) & (lane_f < first + MOE_EXPERTS_PER_GROUP), lg, NEG_INF)
    e_top1, lane1 = top1(e_logit)
    e_top2, lane2 = top1(jnp.where(lane_f == lane1, NEG_INF, e_logit))
    z = jnp.sum(jnp.exp(e_logit - e_top1), axis=1, keepdims=True)
    p1 = 1.0 / z
    p2 = jnp.exp(e_top2 - e_top1) / z
    w1 = g_w * p1 / (p1 + p2)
    w2 = g_w * p2 / (p1 + p2)

    two_hot = (lane_f == lane1) | (lane_f == lane2)
    row = lax.broadcasted_iota(jnp.int32, (tm, tm), 0)
    col = lax.broadcasted_iota(jnp.int32, (tm, tm), 1)
    before = _dot((col < row).astype(BF16), two_hot.astype(BF16)) + counts_acc[...]
    rank1 = jnp.sum(jnp.where(lane_f == lane1, before, 0.0), axis=1, keepdims=True)
    rank2 = jnp.sum(jnp.where(lane_f == lane2, before, 0.0), axis=1, keepdims=True)
    counts_acc[...] += jnp.sum(two_hot.astype(F32), axis=0, keepdims=True)
    counts_ref[...] = counts_acc[...]

    record = jnp.zeros(lg.shape, F32)
    for slot, val in ((ROUTE_E1, lane1 - MOE_GROUPS), (ROUTE_E2, lane2 - MOE_GROUPS), (ROUTE_W1, w1),
                      (ROUTE_W2, w2), (ROUTE_RANK1, rank1), (ROUTE_RANK2, rank2)):
        record = jnp.where(lane == slot, val, record)
    route_ref[...] = record


def _xattn(x, g, wq, kv, wo, l, g_moe, w_route_split, b_route, *, tm=512):
    n, d = x.shape
    tm = min(tm, n)
    full = lambda a: pl.BlockSpec(a.shape, lambda i: (0,) * a.ndim)
    layer = lambda a: pl.BlockSpec((None,) + a.shape[1:], lambda i: (l,) + (0,) * (a.ndim - 1))
    rows = lambda width: pl.BlockSpec((tm, width), lambda i: (i, 0))
    return pl.pallas_call(
        functools.partial(_xattn_kernel, tm=tm),
        out_shape=(jax.ShapeDtypeStruct((n, d), F32), jax.ShapeDtypeStruct((n, d), F32),
                   jax.ShapeDtypeStruct((n, LANES), F32), jax.ShapeDtypeStruct((1, LANES), F32)),
        grid=(n // tm,),
        in_specs=[rows(d), full(g), layer(wq), full(kv), layer(wo), full(g_moe), full(w_route_split), full(b_route)],
        out_specs=(rows(d), rows(d), rows(LANES), pl.BlockSpec((1, LANES), lambda i: (0, 0))),
        scratch_shapes=[pltpu.VMEM(wq.shape[1:], BF16), pltpu.VMEM(wo.shape[1:], BF16),
                        pltpu.VMEM((1, LANES), F32)],
        compiler_params=_params("arbitrary"),
        name="xattn_route",
    )(x, g, wq, kv, wo, g_moe, w_route_split, b_route)


def _moe_kernel(start_ref, count_ref, fill_ref, x_hbm, w1_hbm, w3_hbm, w2_hbm, y_hbm,
                w1f, w3f, w2f, w1b, w3b, w2b, xbuf, ybuf, wsem, xsem, ysem, *, layer):
    e = pl.program_id(0)
    n_experts = pl.num_programs(0)
    rb = MOE_ROW_BLOCK
    par = e % 2
    first = start_ref[e]
    n_blocks = count_ref[e]

    def rows(row):
        return pl.ds(pl.multiple_of(row, MOE_ROW_ALIGN), rb)

    def w_copies(expert, slot):
        return (pltpu.make_async_copy(w1_hbm.at[layer, expert], w1f.at[slot], wsem.at[0, slot]),
                pltpu.make_async_copy(w3_hbm.at[layer, expert], w3f.at[slot], wsem.at[1, slot]),
                pltpu.make_async_copy(w2_hbm.at[layer, expert], w2f.at[slot], wsem.at[2, slot]))

    def x_copy(row, slot, xs):
        return pltpu.make_async_copy(x_hbm.at[rows(row), :], xbuf.at[slot, xs], xsem.at[slot, xs])

    def y_copy(row, ys):
        return pltpu.make_async_copy(ybuf.at[ys], y_hbm.at[rows(row), :], ysem.at[ys])

    def prefetch(expert, slot):
        for xs in range(MOE_X_SLOTS):
            @pl.when(xs < count_ref[expert])
            def _():
                x_copy(start_ref[expert] + xs * rb, slot, xs).start()
        for copy in w_copies(expert, slot):
            copy.start()

    @pl.when(e == 0)
    def _():
        prefetch(0, 0)
        ybuf[0] = jnp.zeros(ybuf.shape[1:], ybuf.dtype)

        def start(block, carry):
            y_copy(block * rb, 0).start()
            return carry

        def wait(block, carry):
            y_copy(block * rb, 0).wait()
            return carry

        total = y_hbm.shape[0] // rb
        lax.fori_loop(fill_ref[0], total, start, 0)
        lax.fori_loop(fill_ref[0], total, wait, 0)

    @pl.when(e + 1 < n_experts)
    def _():
        prefetch(e + 1, 1 - par)

    for copy in w_copies(e, par):
        copy.wait()
    w1b[...] = w1f[par].astype(BF16)
    w3b[...] = w3f[par].astype(BF16)
    w2b[...] = w2f[par].astype(BF16)

    def body(c, carry):
        xs = c % MOE_X_SLOTS
        ys = c % MOE_Y_SLOTS
        row = first + c * rb

        @pl.when(c < MOE_X_SLOTS)
        def _():
            x_copy(row, par, xs).wait()

        @pl.when(c >= MOE_X_SLOTS)
        def _():
            copy = x_copy(row, par, xs)
            copy.start()
            copy.wait()

        @pl.when(c >= MOE_Y_SLOTS)
        def _():
            y_copy(row - MOE_Y_SLOTS * rb, ys).wait()

        x = xbuf[par, xs].astype(BF16)
        hidden = jax.nn.silu(_dot(x, w1b[...])) * _dot(x, w3b[...])
        ybuf[ys] = _dot(hidden.astype(BF16), w2b[...])
        y_copy(row, ys).start()
        return carry

    lax.fori_loop(0, n_blocks, body, 0)

    for back in range(1, MOE_Y_SLOTS + 1):
        @pl.when(n_blocks >= back)
        def _():
            y_copy(first + (n_blocks - back) * rb, (n_blocks - back) % MOE_Y_SLOTS).wait()


def _moe_experts(row_start, block_count, fill_start, x_rows, w1, w3, w2, l):
    p, d = x_rows.shape
    rb = MOE_ROW_BLOCK
    n_experts, dff = w1.shape[1], w1.shape[-1]
    hbm = pl.BlockSpec(memory_space=pl.ANY)
    grid_spec = pltpu.PrefetchScalarGridSpec(
        num_scalar_prefetch=3,
        grid=(n_experts,),
        in_specs=[hbm, hbm, hbm, hbm],
        out_specs=hbm,
        scratch_shapes=[
            pltpu.VMEM((2, d, dff), F32), pltpu.VMEM((2, d, dff), F32), pltpu.VMEM((2, dff, d), F32),
            pltpu.VMEM((d, dff), BF16), pltpu.VMEM((d, dff), BF16), pltpu.VMEM((dff, d), BF16),
            pltpu.VMEM((2, MOE_X_SLOTS, rb, d), F32), pltpu.VMEM((MOE_Y_SLOTS, rb, d), F32),
            pltpu.SemaphoreType.DMA((3, 2)), pltpu.SemaphoreType.DMA((2, MOE_X_SLOTS)),
            pltpu.SemaphoreType.DMA((MOE_Y_SLOTS,)),
        ],
    )
    return pl.pallas_call(
        functools.partial(_moe_kernel, layer=l),
        out_shape=jax.ShapeDtypeStruct((p, d), F32),
        grid_spec=grid_spec,
        compiler_params=_params("arbitrary"),
        name="moe_experts",
    )(row_start, block_count, fill_start, x_rows, w1, w3, w2)


DISPATCH_DEST1, DISPATCH_DEST2 = 0, 1
META_ROW_START, META_ROW_COUNT = 0, 1


def _dispatch_kernel(route_ref, counts_ref, dest_ref, meta_ref):
    align = float(MOE_ROW_ALIGN)
    counts = jnp.broadcast_to(counts_ref[...], (8, LANES))
    padded = jnp.floor((counts + (align - 1.0)) * (1.0 / align)) * align
    src = lax.broadcasted_iota(jnp.int32, (LANES, LANES), 0)
    dst = lax.broadcasted_iota(jnp.int32, (LANES, LANES), 1)
    earlier = (src < dst).astype(BF16)
    seg_start = sum(_dot(part, earlier) for part in _split_bf16(padded, 3))[0:1]

    route = route_ref[...]
    lane = lax.broadcasted_iota(jnp.int32, route.shape, 1)
    lane_f = lane.astype(F32)

    def field(slot):
        return jnp.sum(jnp.where(lane == slot, route, 0.0), axis=1, keepdims=True)

    def dest(e_slot, rank_slot):
        hit = lane_f == field(e_slot) + float(MOE_GROUPS)
        return jnp.sum(jnp.where(hit, seg_start, 0.0), axis=1, keepdims=True) + field(rank_slot)

    dest1 = dest(ROUTE_E1, ROUTE_RANK1)
    dest2 = dest(ROUTE_E2, ROUTE_RANK2)
    dest_ref[...] = jnp.where(lane == DISPATCH_DEST1, dest1, jnp.where(lane == DISPATCH_DEST2, dest2, 0.0))

    sub = lax.broadcasted_iota(jnp.int32, (8, LANES), 0)
    meta_ref[...] = jnp.where(sub == META_ROW_START, jnp.broadcast_to(seg_start, (8, LANES)),
                              jnp.where(sub == META_ROW_COUNT, padded, 0.0))


def _dispatch(route, counts, *, tm=1024):
    n = route.shape[0]
    tm = min(tm, n)
    return pl.pallas_call(
        _dispatch_kernel,
        out_shape=(jax.ShapeDtypeStruct((n, LANES), F32), jax.ShapeDtypeStruct((8, LANES), F32)),
        grid=(n // tm,),
        in_specs=[pl.BlockSpec((tm, LANES), lambda i: (i, 0)), pl.BlockSpec((1, LANES), lambda i: (0, 0))],
        out_specs=(pl.BlockSpec((tm, LANES), lambda i: (i, 0)), pl.BlockSpec((8, LANES), lambda i: (0, 0))),
        compiler_params=_params("arbitrary"),
        name="moe_dispatch",
    )(route, counts)


def _dispatch_layout(route, counts):
    n = route.shape[0]
    rb = MOE_ROW_BLOCK
    dest, meta = _dispatch(route, counts)
    dest_pair = jnp.concatenate([dest[:, DISPATCH_DEST1], dest[:, DISPATCH_DEST2]]).astype(jnp.int32)
    experts = slice(MOE_GROUPS, MOE_GROUPS + MOE_EXPERTS)
    row_start = meta[META_ROW_START, experts].astype(jnp.int32)
    row_count = meta[META_ROW_COUNT, experts].astype(jnp.int32)
    block_count = (row_count + rb - 1) // rb
    fill_start = (row_start[-1:] + row_count[-1:]) // rb
    n_rows = (n * MOE_TOPK + MOE_EXPERTS * (MOE_ROW_ALIGN - 1) + rb - 1) // rb * rb + rb
    tok = jnp.arange(n, dtype=jnp.int32)
    row_tok = (jnp.arange(n_rows, dtype=jnp.int32) % n).at[dest_pair].set(
        jnp.concatenate([tok, tok]), mode="promise_in_bounds", unique_indices=True)
    return row_tok, row_start, block_count, fill_start, dest_pair


def _combine_kernel(x_ref, y1_ref, y2_ref, route_ref, g_ref, o_ref, *, final_norm):
    route = route_ref[...]
    w1 = route[:, ROUTE_W1:ROUTE_W1 + 1]
    w2 = route[:, ROUTE_W2:ROUTE_W2 + 1]
    out = x_ref[...] + (w1 * y1_ref[...] + w2 * y2_ref[...])
    o_ref[...] = _rms(out, g_ref[...]) if final_norm else out


def _combine(x, y_pair, route, g, *, final_norm, tm=512):
    n, d = x.shape
    tm = min(tm, n)
    steps = n // tm
    rows = lambda width: pl.BlockSpec((tm, width), lambda i: (i, 0))
    return pl.pallas_call(
        functools.partial(_combine_kernel, final_norm=final_norm),
        out_shape=jax.ShapeDtypeStruct((n, d), F32),
        grid=(steps,),
        in_specs=[rows(d), rows(d), pl.BlockSpec((tm, d), lambda i: (steps + i, 0)), rows(LANES),
                  pl.BlockSpec((1, d), lambda i: (0, 0))],
        out_specs=rows(d),
        compiler_params=_params("parallel"),
        name="moe_combine",
    )(x, y_pair, y_pair, route, g)


def _rope_tables(positions):
    inv = 1.0 / (ROPE_THETA ** (jnp.arange(0, MOBA_HEAD_DIM, 2, dtype=F32) / MOBA_HEAD_DIM))
    ang = positions.astype(F32)[:, None] * inv
    cos, sin = jnp.cos(ang), jnp.sin(ang)
    return jnp.concatenate([cos, cos], axis=-1), jnp.concatenate([-sin, sin], axis=-1)


def _layer(x, mem, cos_f, sin_f, p, l, norm_final):
    w_in = p["w_in"]
    w_pool_in = w_in[l, :, W_IN_POOL:]
    w_low = jnp.pad(w_in[l, :, W_IN_LOW:W_IN_POOL], ((0, 0), (0, LANES - GLA_RANK)))
    proj, h = _inproj(x, _layer_vec(p["norm_mix"], l), w_in, w_pool_in, l, cos_f, sin_f)
    o_moba = _moba(proj)
    w_dec = jnp.pad(p["w_gla_decay"][l], ((0, LANES - GLA_RANK), (0, 0))).astype(BF16)
    o_gla = _gla(proj, h, w_low, w_dec, _layer_vec(p["b_gla_decay"], l), _layer_vec(p["gla_norm"], l))
    o_pool = _pool(proj, p["w_pool"][l].astype(BF16), _layer_vec(p["pool_scale"], l))
    merged = _merge(h, o_moba, o_gla, o_pool, p["w_gate"], _layer_vec(p["b_gate"], l), p["w_branch"], l)
    x = _matmul_residual(merged, p["w_mix_out"], l, x)

    kv = _mem_kv(mem, _layer_vec(p["norm_mem"], l), p["w_xkv"], l)
    w_route = jnp.concatenate([p["w_route_group"][l], p["w_route_expert"][l]], axis=1)
    n_route = w_route.shape[1]
    w_route_split = jnp.stack(_split_bf16(jnp.pad(w_route, ((0, 0), (0, LANES - n_route))), 2))
    b_route = jnp.pad(jnp.concatenate([p["b_route_group"][l], p["b_route_expert"][l]]), (0, LANES - n_route))
    x, h2, route, counts = _xattn(x, _layer_vec(p["norm_xattn"], l), p["w_xq"], kv, p["w_xo"], l,
                                  _layer_vec(p["norm_moe"], l), w_route_split, b_route.reshape(1, -1))

    row_tok, row_start, block_count, fill_start, dest_pair = _dispatch_layout(route, counts)
    x_rows = h2.at[row_tok].get(mode="promise_in_bounds")
    y_rows = _moe_experts(row_start, block_count, fill_start, x_rows,
                          p["w_exp_gate"], p["w_exp_up"], p["w_exp_down"], l)
    y_pair = y_rows.at[dest_pair].get(mode="promise_in_bounds")
    return _combine(x, y_pair, route, norm_final.reshape(1, -1), final_norm=l == p["norm_mix"].shape[0] - 1)


def kernel(x, mem, positions, norm_mix, w_in, w_gla_decay, b_gla_decay, gla_norm, w_pool, pool_scale, w_branch, w_gate, b_gate, w_mix_out, norm_xattn, norm_mem, w_xq, w_xkv, w_xo, norm_moe, w_route_group, b_route_group, w_route_expert, b_route_expert, w_exp_gate, w_exp_up, w_exp_down, norm_final):
    batch, seq, d = x.shape
    assert batch == 1, "kernels are written for a single sequence"
    params = dict(norm_mix=norm_mix, w_in=w_in.astype(BF16), w_gla_decay=w_gla_decay, b_gla_decay=b_gla_decay,
                  gla_norm=gla_norm, w_pool=w_pool, pool_scale=pool_scale, w_branch=w_branch, w_gate=w_gate,
                  b_gate=b_gate, w_mix_out=w_mix_out, norm_xattn=norm_xattn, norm_mem=norm_mem, w_xq=w_xq,
                  w_xkv=w_xkv, w_xo=w_xo, norm_moe=norm_moe, w_route_group=w_route_group,
                  b_route_group=b_route_group, w_route_expert=w_route_expert, b_route_expert=b_route_expert,
                  w_exp_gate=w_exp_gate, w_exp_up=w_exp_up, w_exp_down=w_exp_down)
    cos_f, sin_f = _rope_tables(positions[0])
    xs = x[0]
    for l in range(norm_mix.shape[0]):
        xs = _layer(xs, mem[0], cos_f, sin_f, params, l, norm_final)
    return xs.reshape(batch, seq, d)
```

```python
import functools

import jax
import jax.numpy as jnp
from jax import lax
from jax.experimental import pallas as pl
from jax.experimental.pallas import tpu as pltpu

F32 = jnp.float32
BF16 = jnp.bfloat16

EPS = 1e-6
NEG_INF = -1e30

MOBA_HEADS = 8
MOBA_HEAD_DIM = 128
MOBA_WIDTH = MOBA_HEADS * MOBA_HEAD_DIM
MOBA_BLOCK = 256
MOBA_TOPK = 3
MOBA_KV_BLOCKS = 4
MOBA_HEADS_PER_STEP = 8
MOBA_Q_BLOCKS = 1
ROPE_THETA = 10000.0
MOBA_Q_SCALE = MOBA_HEAD_DIM ** -0.5 * 1.4426950408889634

GLA_HEADS = 4
GLA_DK = 128
GLA_DV = 256
GLA_K_WIDTH = GLA_HEADS * GLA_DK
GLA_V_WIDTH = GLA_HEADS * GLA_DV
GLA_RANK = 16
GLA_TAU = 16.0
GLA_CHUNK = 64

POOL_WINDOWS = (2, 4, 8, 16)
POOL_GROUP_DIM = 256
POOL_WIDTH = len(POOL_WINDOWS) * POOL_GROUP_DIM
POOL_HALO = 16

XATTN_HEADS = 4
XATTN_HEAD_DIM = 128
XATTN_WIDTH = XATTN_HEADS * XATTN_HEAD_DIM

MOE_GROUPS = 8
MOE_EXPERTS_PER_GROUP = 8
MOE_EXPERTS = MOE_GROUPS * MOE_EXPERTS_PER_GROUP
MOE_TOPK = 2
MOE_ROW_BLOCK = 128
MOE_ROW_ALIGN = 8
MOE_X_SLOTS = 4
MOE_Y_SLOTS = 4

LANES = 128

COL_MQ = 0
COL_MK = COL_MQ + MOBA_WIDTH
COL_MV = COL_MK + MOBA_WIDTH
COL_GQ = COL_MV + MOBA_WIDTH
COL_GK = COL_GQ + GLA_K_WIDTH
COL_GV = COL_GK + GLA_K_WIDTH
COL_GR = COL_GV + GLA_V_WIDTH
COL_PU = COL_GR + GLA_V_WIDTH
PROJ_WIDTH = COL_PU + POOL_WIDTH
W_IN_LOW = COL_PU
W_IN_POOL = COL_PU + GLA_RANK

ROUTE_E1, ROUTE_E2, ROUTE_W1, ROUTE_W2, ROUTE_RANK1, ROUTE_RANK2 = range(6)

VMEM_LIMIT = 56 * 1024 * 1024


def _params(*semantics):
    return pltpu.CompilerParams(dimension_semantics=semantics, vmem_limit_bytes=VMEM_LIMIT)


def _rms(x, g):
    return x * lax.rsqrt(jnp.mean(x * x, axis=-1, keepdims=True) + EPS) * g


def _dot(a, b):
    return jnp.dot(a, b, preferred_element_type=F32)


def _dot_nt(a, b):
    return lax.dot_general(a, b, (((1,), (1,)), ((), ())), preferred_element_type=F32)


def _dot_tn(a, b):
    return lax.dot_general(a, b, (((0,), (0,)), ((), ())), preferred_element_type=F32)


def _split_bf16(x, terms):
    parts = []
    for _ in range(terms):
        part = x.astype(BF16)
        parts.append(part)
        x = x - part.astype(F32)
    return parts


def _layer_vec(v, l):
    return v[l].reshape(1, -1)


def _inproj_kernel(x_ref, g_ref, w_ref, wp_ref, cos_ref, sin_ref, o_ref, h_ref, *, tn, n_main):
    j = pl.program_id(1)

    @pl.when(j == 0)
    def _():
        h_ref[...] = _rms(x_ref[...], g_ref[...]).astype(BF16)

    n_rope = (2 * MOBA_WIDTH) // tn
    n_q = MOBA_WIDTH // tn

    @pl.when(j < n_rope)
    def _():
        acc = _dot(h_ref[...], w_ref[...])
        scale = jnp.where(j < n_q, MOBA_Q_SCALE, 1.0).astype(F32)
        cos = cos_ref[...] * scale
        sin = sin_ref[...] * scale
        for hh in range(tn // MOBA_HEAD_DIM):
            cols = slice(hh * MOBA_HEAD_DIM, (hh + 1) * MOBA_HEAD_DIM)
            a = acc[:, cols]
            rot = pltpu.roll(a, MOBA_HEAD_DIM // 2, axis=1)
            o_ref[:, cols] = (a * cos + rot * sin).astype(o_ref.dtype)

    @pl.when((j >= n_rope) & (j < n_main))
    def _():
        o_ref[...] = _dot(h_ref[...], w_ref[...]).astype(o_ref.dtype)

    @pl.when(j >= n_main)
    def _():
        o_ref[...] = _dot(h_ref[...], wp_ref[...]).astype(o_ref.dtype)


def _inproj(x, g, w_in, w_pool_in, l, cos_f, sin_f, *, tm=1024, tn=1024):
    n, d = x.shape
    tm = min(tm, n)
    n_main = COL_PU // tn
    n_tiles = PROJ_WIDTH // tn
    return pl.pallas_call(
        functools.partial(_inproj_kernel, tn=tn, n_main=n_main),
        out_shape=(jax.ShapeDtypeStruct((n, PROJ_WIDTH), BF16), jax.ShapeDtypeStruct((n, d), BF16)),
        grid=(n // tm, n_tiles),
        in_specs=[
            pl.BlockSpec((tm, d), lambda i, j: (i, 0)),
            pl.BlockSpec((1, d), lambda i, j: (0, 0)),
            pl.BlockSpec((None, d, tn), lambda i, j: (l, 0, jnp.minimum(j, n_main - 1))),
            pl.BlockSpec((d, tn), lambda i, j: (0, jnp.maximum(j - n_main, 0))),
            pl.BlockSpec((tm, MOBA_HEAD_DIM), lambda i, j: (i, 0)),
            pl.BlockSpec((tm, MOBA_HEAD_DIM), lambda i, j: (i, 0)),
        ],
        out_specs=(
            pl.BlockSpec((tm, tn), lambda i, j: (i, j)),
            pl.BlockSpec((tm, d), lambda i, j: (i, 0)),
        ),
        compiler_params=_params("parallel", "arbitrary"),
        name="inproj",
    )(x, g, w_in, w_pool_in, cos_f, sin_f)


def _moba_kernel(q_ref, k_ref, v_ref, o_ref, kmean_ref, *, n_blocks):
    qi = pl.program_id(1)
    blk = MOBA_BLOCK
    hd = MOBA_HEAD_DIM
    q_rows = MOBA_Q_BLOCKS * blk
    tile = MOBA_KV_BLOCKS * blk
    blk_shift = blk.bit_length() - 1
    head_cols = [slice(hh * hd, (hh + 1) * hd) for hh in range(MOBA_HEADS_PER_STEP)]
    first_block = qi * MOBA_Q_BLOCKS

    @pl.when(qi == 0)
    def _():
        kmean_ref[...] = jnp.zeros_like(kmean_ref)

        def fill(b, carry):
            kb = k_ref[pl.ds(pl.multiple_of(b * blk, blk), blk), :].astype(F32)
            kmean_ref[pl.ds(b, 1), :] = jnp.mean(kb, axis=0, keepdims=True)
            return carry

        lax.fori_loop(0, n_blocks, fill, 0)

    lane = lax.broadcasted_iota(jnp.int32, (q_rows, LANES), 1)
    lane_f = lane.astype(F32)
    own_block = first_block + (lax.broadcasted_iota(jnp.int32, (q_rows, LANES), 0) >> blk_shift)
    row = lax.broadcasted_iota(jnp.int32, (blk, blk), 0)
    col = lax.broadcasted_iota(jnp.int32, (blk, blk), 1)
    ones_own = jnp.ones((blk, hd), BF16)
    ones_tile = jnp.ones((tile, hd), BF16)

    q_augs, carry = [], []
    for cols in head_cols:
        q = q_ref[:, cols]
        kmean_hi, kmean_mid = _split_bf16(kmean_ref[:, cols], 2)
        gate = _dot_nt(q, kmean_hi) + _dot_nt(q, kmean_mid)
        gate = jnp.where(lane < own_block, gate, NEG_INF)
        sel = jnp.zeros(gate.shape, jnp.bool_)
        for _ in range(MOBA_TOPK):
            top = jnp.max(gate, axis=1, keepdims=True)
            idx = jnp.min(jnp.where(gate == top, lane_f, float(LANES)), axis=1, keepdims=True)
            hit = lane_f == idx
            sel = sel | (hit & (top > 0.5 * NEG_INF))
            gate = jnp.where(hit, NEG_INF, gate)
        q_augs.append(jnp.concatenate([q, jnp.where(sel, 0.0, NEG_INF).astype(BF16)], axis=1))

        ms, accs = [], []
        for qb in range(MOBA_Q_BLOCKS):
            start = pl.multiple_of((first_block + qb) * blk, blk)
            s = _dot_nt(q[qb * blk:(qb + 1) * blk, :], k_ref[pl.ds(start, blk), cols])
            s = jnp.where(col <= row, s, NEG_INF)
            m = jnp.max(s, axis=1, keepdims=True)
            p = jnp.exp2(s - m).astype(BF16)
            ms.append(m)
            accs.append(_dot(p, jnp.concatenate([v_ref[pl.ds(start, blk), cols], ones_own], axis=1)))
        carry += [jnp.concatenate(ms, axis=0), jnp.concatenate(accs, axis=0)]

    tile_row = lax.broadcasted_iota(jnp.int32, (tile, hd), 0)
    tile_lane = lax.broadcasted_iota(jnp.int32, (tile, hd), 1)

    def body(t, carry):
        off = pl.multiple_of(t * tile, tile)
        block_hot = (tile_lane == t * MOBA_KV_BLOCKS + (tile_row >> blk_shift)).astype(BF16)
        out = []
        for hh, cols in enumerate(head_cols):
            m, acc = carry[2 * hh], carry[2 * hh + 1]
            k_aug = jnp.concatenate([k_ref[pl.ds(off, tile), cols], block_hot], axis=1)
            s = _dot_nt(q_augs[hh], k_aug)
            m_new = jnp.maximum(m, jnp.max(s, axis=1, keepdims=True))
            alpha = jnp.exp2(m - m_new)
            p = jnp.exp2(s - m_new).astype(BF16)
            v_aug = jnp.concatenate([v_ref[pl.ds(off, tile), cols], ones_tile], axis=1)
            out += [m_new, alpha * acc + _dot(p, v_aug)]
        return tuple(out)

    n_tiles = (first_block + MOBA_Q_BLOCKS - 1 + MOBA_KV_BLOCKS - 1) // MOBA_KV_BLOCKS
    carry = lax.fori_loop(0, n_tiles, body, tuple(carry))
    for hh, cols in enumerate(head_cols):
        acc = carry[2 * hh + 1]
        o_ref[:, cols] = (acc[:, :hd] / acc[:, hd:]).astype(o_ref.dtype)


def _moba(proj):
    n = proj.shape[0]
    n_blocks = n // MOBA_BLOCK
    assert n_blocks % MOBA_KV_BLOCKS == 0 and n_blocks % MOBA_Q_BLOCKS == 0 and n_blocks <= LANES
    width = MOBA_HEADS_PER_STEP * MOBA_HEAD_DIM
    q_rows = MOBA_Q_BLOCKS * MOBA_BLOCK
    return pl.pallas_call(
        functools.partial(_moba_kernel, n_blocks=n_blocks),
        out_shape=jax.ShapeDtypeStruct((n, MOBA_WIDTH), BF16),
        grid=(MOBA_HEADS // MOBA_HEADS_PER_STEP, n_blocks // MOBA_Q_BLOCKS),
        in_specs=[
            pl.BlockSpec((q_rows, width), lambda h, i: (i, COL_MQ // width + h)),
            pl.BlockSpec((n, width), lambda h, i: (0, COL_MK // width + h), pipeline_mode=pl.Buffered(1)),
            pl.BlockSpec((n, width), lambda h, i: (0, COL_MV // width + h), pipeline_mode=pl.Buffered(1)),
        ],
        out_specs=pl.BlockSpec((q_rows, width), lambda h, i: (i, h)),
        scratch_shapes=[pltpu.VMEM((LANES, width), F32)],
        compiler_params=_params("parallel", "arbitrary"),
        name="moba",
    )(proj, proj, proj)


def _gla_kernel(q_ref, k_ref, v_ref, r_ref, h_ref, wlow_ref, wdec_ref, bdec_ref, gn_ref, o_ref, state_ref, *, tb):
    i = pl.program_id(0)
    c = GLA_CHUNK
    n_chunks = tb // c
    chunk_shift = c.bit_length() - 1

    @pl.when(i == 0)
    def _():
        state_ref[...] = jnp.zeros_like(state_ref)

    g_low = _dot(h_ref[...], wlow_ref[...]).astype(BF16)
    z = _dot(g_low, wdec_ref[...]) + bdec_ref[...]
    log_a = jax.nn.log_sigmoid(z) / GLA_TAU

    row = lax.broadcasted_iota(jnp.int32, (tb, tb), 0)
    col = lax.broadcasted_iota(jnp.int32, (tb, tb), 1)
    causal = ((row >> chunk_shift) == (col >> chunk_shift)) & (col <= row)
    tri = (lax.broadcasted_iota(jnp.int32, (c, c), 1) <= lax.broadcasted_iota(jnp.int32, (c, c), 0)).astype(BF16)
    parts = _split_bf16(log_a, 3)
    b_chunks, last_chunks = [], []
    for ci in range(n_chunks):
        rows = slice(ci * c, (ci + 1) * c)
        b_c = sum(_dot(tri, part[rows, :]) for part in parts)
        b_chunks.append(b_c)
        last_chunks.append(jnp.broadcast_to(b_c[c - 1:c, :], b_c.shape))
    b = jnp.concatenate(b_chunks, axis=0)
    b_last = jnp.concatenate(last_chunks, axis=0)

    q_dec = (q_ref[...].astype(F32) * (GLA_DK ** -0.5) * jnp.exp(b)).astype(BF16)
    kf = k_ref[...].astype(F32)
    k_inv = (kf * jnp.exp(-b)).astype(BF16)
    k_end = (kf * jnp.exp(b_last - b)).astype(BF16)
    decay = jnp.exp(b_last)

    for h in range(GLA_HEADS):
        kc = slice(h * GLA_DK, (h + 1) * GLA_DK)
        vc = slice(h * GLA_DV, (h + 1) * GLA_DV)
        v = v_ref[:, vc]
        attn = jnp.where(causal, _dot_nt(q_dec[:, kc], k_inv[:, kc]), 0.0)
        o = _dot(attn.astype(BF16), v)
        state = state_ref[h]
        inter = []
        for ci in range(n_chunks):
            rows = slice(ci * c, (ci + 1) * c)
            inter.append(_dot_nt(q_dec[rows, kc], state.astype(BF16)))
            update = _dot_tn(v[rows, :], k_end[rows, kc])
            state = decay[ci * c:ci * c + 1, kc] * state + update
        state_ref[h] = state
        o = o + jnp.concatenate(inter, axis=0)
        o = _rms(o, gn_ref[...])
        o_ref[:, vc] = (o * jax.nn.silu(r_ref[:, vc].astype(F32))).astype(o_ref.dtype)


def _gla(proj, h, w_low, w_dec, b_dec, gla_norm, *, tb=512):
    n, d = h.shape
    tb = min(tb, n)
    return pl.pallas_call(
        functools.partial(_gla_kernel, tb=tb),
        out_shape=jax.ShapeDtypeStruct((n, GLA_V_WIDTH), BF16),
        grid=(n // tb,),
        in_specs=[
            pl.BlockSpec((tb, GLA_K_WIDTH), lambda i: (i, COL_GQ // GLA_K_WIDTH)),
            pl.BlockSpec((tb, GLA_K_WIDTH), lambda i: (i, COL_GK // GLA_K_WIDTH)),
            pl.BlockSpec((tb, GLA_V_WIDTH), lambda i: (i, COL_GV // GLA_V_WIDTH)),
            pl.BlockSpec((tb, GLA_V_WIDTH), lambda i: (i, COL_GR // GLA_V_WIDTH)),
            pl.BlockSpec((tb, d), lambda i: (i, 0)),
            pl.BlockSpec((d, LANES), lambda i: (0, 0)),
            pl.BlockSpec((LANES, GLA_K_WIDTH), lambda i: (0, 0)),
            pl.BlockSpec((1, GLA_K_WIDTH), lambda i: (0, 0)),
            pl.BlockSpec((1, GLA_DV), lambda i: (0, 0)),
        ],
        out_specs=pl.BlockSpec((tb, GLA_V_WIDTH), lambda i: (i, 0)),
        scratch_shapes=[pltpu.VMEM((GLA_HEADS, GLA_DV, GLA_DK), F32)],
        compiler_params=_params("arbitrary"),
        name="gla",
    )(proj, proj, proj, proj, h, w_low, w_dec, b_dec, gla_norm)


def _pool_kernel(u_ref, halo_ref, w_ref, sc_ref, o_ref, ext_ref, lvl_ref, *, tb):
    i = pl.program_id(0)
    pad = POOL_HALO
    body = POOL_HALO + tb
    first = pad + POOL_HALO
    ext_ref[pl.ds(0, pad), :] = jnp.zeros((pad, POOL_WIDTH), F32)
    lvl_ref[:, pl.ds(0, pad), :] = jnp.zeros((2, pad, POOL_GROUP_DIM), F32)
    ext_ref[pl.ds(pad, POOL_HALO), :] = jnp.where(i == 0, 0.0, halo_ref[...].astype(F32))
    ext_ref[pl.ds(first, tb), :] = u_ref[...].astype(F32)
    t = i * tb + lax.broadcasted_iota(jnp.int32, (tb, 1), 0)
    for g, win in enumerate(POOL_WINDOWS):
        cols = slice(g * POOL_GROUP_DIM, (g + 1) * POOL_GROUP_DIM)

        def read(level, start, rows, cols=cols):
            if level is None:
                return ext_ref[pl.ds(start, rows), cols]
            return lvl_ref[level, pl.ds(start, rows), :]

        u = read(None, first, tb)
        level, span, stored = None, 1, 0
        while 2 * span < win:
            doubled = read(level, pad, body) + read(level, pad - span, body)
            lvl_ref[stored % 2, pl.ds(pad, body), :] = doubled
            level, span, stored = stored % 2, 2 * span, stored + 1
        window_sum = read(level, first, tb) + read(level, first - span, tb)
        count = jnp.minimum(t + 1, win).astype(F32)
        mixed = window_sum / count - u
        y = _dot(mixed.astype(BF16), w_ref[g])
        o_ref[:, cols] = (y * sc_ref[:, cols]).astype(o_ref.dtype)


def _pool(proj, w_pool, pool_scale, *, tb=512):
    n = proj.shape[0]
    tb = min(tb, n)
    halo_per_block = tb // POOL_HALO
    return pl.pallas_call(
        functools.partial(_pool_kernel, tb=tb),
        out_shape=jax.ShapeDtypeStruct((n, POOL_WIDTH), BF16),
        grid=(n // tb,),
        in_specs=[
            pl.BlockSpec((tb, POOL_WIDTH), lambda i: (i, COL_PU // POOL_WIDTH)),
            pl.BlockSpec((POOL_HALO, POOL_WIDTH),
                         lambda i: (jnp.maximum(i * halo_per_block - 1, 0), COL_PU // POOL_WIDTH)),
            pl.BlockSpec(w_pool.shape, lambda i: (0, 0, 0)),
            pl.BlockSpec((1, POOL_WIDTH), lambda i: (0, 0)),
        ],
        out_specs=pl.BlockSpec((tb, POOL_WIDTH), lambda i: (i, 0)),
        scratch_shapes=[pltpu.VMEM((2 * POOL_HALO + tb, POOL_WIDTH), F32),
                        pltpu.VMEM((2, 2 * POOL_HALO + tb, POOL_GROUP_DIM), F32)],
        compiler_params=_params("parallel"),
        name="pool",
    )(proj, proj, w_pool, pool_scale)


def _merge_kernel(h_ref, a_ref, b_ref, c_ref, wg0, wg1, wg2, bg0, bg1, bg2, wb0, wb1, wb2, o_ref):
    h = h_ref[...]
    total = None
    for br_ref, wg, bg, wb in ((a_ref, wg0, bg0, wb0), (b_ref, wg1, bg1, wb1), (c_ref, wg2, bg2, wb2)):
        gate = jax.nn.sigmoid(_dot(h, wg[...].astype(BF16)) + bg[...])
        term = gate * _dot(br_ref[...], wb[...].astype(BF16))
        total = term if total is None else total + term
    o_ref[...] = total.astype(o_ref.dtype)


def _merge(h, o_moba, o_gla, o_pool, w_gate, b_gate, w_branch, l, *, tm=1024, tn=256):
    n, d = h.shape
    tm = min(tm, n)
    bw = o_moba.shape[1]
    tiles = d // tn
    row_spec = lambda width: pl.BlockSpec((tm, width), lambda i, j: (i, 0))
    gate_specs = [pl.BlockSpec((None, d, tn), lambda i, j, k=k: (l, 0, k * tiles + j)) for k in range(3)]
    bias_specs = [pl.BlockSpec((1, tn), lambda i, j, k=k: (0, k * tiles + j)) for k in range(3)]
    branch_specs = [pl.BlockSpec((None, None, bw, tn), lambda i, j, k=k: (l, k, 0, j)) for k in range(3)]
    return pl.pallas_call(
        _merge_kernel,
        out_shape=jax.ShapeDtypeStruct((n, d), BF16),
        grid=(n // tm, tiles),
        in_specs=[row_spec(d), row_spec(bw), row_spec(bw), row_spec(bw)] + gate_specs + bias_specs + branch_specs,
        out_specs=pl.BlockSpec((tm, tn), lambda i, j: (i, j)),
        compiler_params=_params("parallel", "arbitrary"),
        name="merge",
    )(h, o_moba, o_gla, o_pool, w_gate, w_gate, w_gate, b_gate, b_gate, b_gate, w_branch, w_branch, w_branch)


def _matmul_residual_kernel(a_ref, w_ref, x_ref, o_ref):
    o_ref[...] = x_ref[...] + _dot(a_ref[...], w_ref[...].astype(BF16))


def _matmul_residual(a, w, l, x, *, tm=2048, tn=512):
    n, k = a.shape
    d = w.shape[-1]
    tm = min(tm, n)
    return pl.pallas_call(
        _matmul_residual_kernel,
        out_shape=jax.ShapeDtypeStruct((n, d), F32),
        grid=(n // tm, d // tn),
        in_specs=[
            pl.BlockSpec((tm, k), lambda i, j: (i, 0)),
            pl.BlockSpec((None, k, tn), lambda i, j: (l, 0, j)),
            pl.BlockSpec((tm, tn), lambda i, j: (i, j)),
        ],
        out_specs=pl.BlockSpec((tm, tn), lambda i, j: (i, j)),
        compiler_params=_params("parallel", "arbitrary"),
        name="mix_out",
    )(a, w, x)


def _norm_matmul_kernel(x_ref, g_ref, w_ref, o_ref):
    o_ref[...] = _dot(_rms(x_ref[...], g_ref[...]).astype(BF16), w_ref[...].astype(BF16)).astype(o_ref.dtype)


def _mem_kv(mem, g, w_xkv, l):
    n, d = mem.shape
    width = w_xkv.shape[-1]
    return pl.pallas_call(
        _norm_matmul_kernel,
        out_shape=jax.ShapeDtypeStruct((n, width), BF16),
        grid=(1,),
        in_specs=[
            pl.BlockSpec((n, d), lambda i: (0, 0)),
            pl.BlockSpec((1, d), lambda i: (0, 0)),
            pl.BlockSpec((None, d, width), lambda i: (l, 0, 0)),
        ],
        out_specs=pl.BlockSpec((n, width), lambda i: (0, 0)),
        compiler_params=_params("arbitrary"),
        name="mem_kv",
    )(mem, g, w_xkv)


def _xattn_kernel(x_ref, g_ref, wq_ref, kv_ref, wo_ref, gm_ref, wr_ref, br_ref,
                  xo_ref, h2_ref, route_ref, counts_ref, wq_b, wo_b, counts_acc, *, tm):
    i = pl.program_id(0)

    @pl.when(i == 0)
    def _():
        wq_b[...] = wq_ref[...].astype(BF16)
        wo_b[...] = wo_ref[...].astype(BF16)
        counts_acc[...] = jnp.zeros_like(counts_acc)

    x = x_ref[...]
    h = _rms(x, g_ref[...]).astype(BF16)
    q = (_dot(h, wq_b[...]) * (XATTN_HEAD_DIM ** -0.5)).astype(BF16)
    heads = []
    for hd in range(XATTN_HEADS):
        kc = slice(hd * XATTN_HEAD_DIM, (hd + 1) * XATTN_HEAD_DIM)
        vc = slice(XATTN_WIDTH + hd * XATTN_HEAD_DIM, XATTN_WIDTH + (hd + 1) * XATTN_HEAD_DIM)
        s = _dot_nt(q[:, kc], kv_ref[:, kc])
        p = jnp.exp(s - jnp.max(s, axis=1, keepdims=True))
        p = p / jnp.sum(p, axis=1, keepdims=True)
        heads.append(_dot(p.astype(BF16), kv_ref[:, vc]).astype(BF16))
    o = jnp.concatenate(heads, axis=1)
    xn = x + _dot(o, wo_b[...])
    xo_ref[...] = xn
    h2 = _rms(xn, gm_ref[...])
    h2_ref[...] = h2

    h_hi, h_mid = _split_bf16(h2, 2)
    hi_terms = _dot(h_hi, wr_ref[...])
    lg = hi_terms[:, :LANES] + (hi_terms[:, LANES:] + _dot(h_mid, wr_ref[:, :LANES])) + br_ref[...]

    lane = lax.broadcasted_iota(jnp.int32, lg.shape, 1)
    lane_f = lane.astype(F32)
    big = float(LANES)

    def top1(v):
        top = jnp.max(v, axis=1, keepdims=True)
        return top, jnp.min(jnp.where(v == top, lane_f, big), axis=1, keepdims=True)

    g_logit = jnp.where(lane < MOE_GROUPS, lg, NEG_INF)
    g_top, g_idx = top1(g_logit)
    g_w = 1.0 / jnp.sum(jnp.exp(g_logit - g_top), axis=1, keepdims=True)
    first = MOE_GROUPS + MOE_EXPERTS_PER_GROUP * g_idx
    e_logit = jnp.where((lane_f >= first) & (lane_f < first + MOE_EXPERTS_PER_GROUP), lg, NEG_INF)
    e_top1, lane1 = top1(e_logit)
    e_top2, lane2 = top1(jnp.where(lane_f == lane1, NEG_INF, e_logit))
    z = jnp.sum(jnp.exp(e_logit - e_top1), axis=1, keepdims=True)
    p1 = 1.0 / z
    p2 = jnp.exp(e_top2 - e_top1) / z
    w1 = g_w * p1 / (p1 + p2)
    w2 = g_w * p2 / (p1 + p2)

    two_hot = (lane_f == lane1) | (lane_f == lane2)
    row = lax.broadcasted_iota(jnp.int32, (tm, tm), 0)
    col = lax.broadcasted_iota(jnp.int32, (tm, tm), 1)
    before = _dot((col < row).astype(BF16), two_hot.astype(BF16)) + counts_acc[...]
    rank1 = jnp.sum(jnp.where(lane_f == lane1, before, 0.0), axis=1, keepdims=True)
    rank2 = jnp.sum(jnp.where(lane_f == lane2, before, 0.0), axis=1, keepdims=True)
    counts_acc[...] += jnp.sum(two_hot.astype(F32), axis=0, keepdims=True)
    counts_ref[...] = counts_acc[...]

    record = jnp.zeros(lg.shape, F32)
    for slot, val in ((ROUTE_E1, lane1 - MOE_GROUPS), (ROUTE_E2, lane2 - MOE_GROUPS), (ROUTE_W1, w1),
                      (ROUTE_W2, w2), (ROUTE_RANK1, rank1), (ROUTE_RANK2, rank2)):
        record = jnp.where(lane == slot, val, record)
    route_ref[...] = record


def _xattn(x, g, wq, kv, wo, l, g_moe, w_route_split, b_route, *, tm=512):
    n, d = x.shape
    tm = min(tm, n)
    full = lambda a: pl.BlockSpec(a.shape, lambda i: (0,) * a.ndim)
    layer = lambda a: pl.BlockSpec((None,) + a.shape[1:], lambda i: (l,) + (0,) * (a.ndim - 1))
    rows = lambda width: pl.BlockSpec((tm, width), lambda i: (i, 0))
    return pl.pallas_call(
        functools.partial(_xattn_kernel, tm=tm),
        out_shape=(jax.ShapeDtypeStruct((n, d), F32), jax.ShapeDtypeStruct((n, d), F32),
                   jax.ShapeDtypeStruct((n, LANES), F32), jax.ShapeDtypeStruct((1, LANES), F32)),
        grid=(n // tm,),
        in_specs=[rows(d), full(g), layer(wq), full(kv), layer(wo), full(g_moe), full(w_route_split), full(b_route)],
        out_specs=(rows(d), rows(d), rows(LANES), pl.BlockSpec((1, LANES), lambda i: (0, 0))),
        scratch_shapes=[pltpu.VMEM(wq.shape[1:], BF16), pltpu.VMEM(wo.shape[1:], BF16),
                        pltpu.VMEM((1, LANES), F32)],
        compiler_params=_params("arbitrary"),
        name="xattn_route",
    )(x, g, wq, kv, wo, g_moe, w_route_split, b_route)


def _moe_kernel(start_ref, count_ref, fill_ref, x_hbm, w1_hbm, w3_hbm, w2_hbm, y_hbm,
                w1f, w3f, w2f, w1b, w3b, w2b, xbuf, ybuf, wsem, xsem, ysem, *, layer):
    e = pl.program_id(0)
    n_experts = pl.num_programs(0)
    rb = MOE_ROW_BLOCK
    par = e % 2
    first = start_ref[e]
    n_blocks = count_ref[e]

    def rows(row):
        return pl.ds(pl.multiple_of(row, MOE_ROW_ALIGN), rb)

    def w_copies(expert, slot):
        return (pltpu.make_async_copy(w1_hbm.at[layer, expert], w1f.at[slot], wsem.at[0, slot]),
                pltpu.make_async_copy(w3_hbm.at[layer, expert], w3f.at[slot], wsem.at[1, slot]),
                pltpu.make_async_copy(w2_hbm.at[layer, expert], w2f.at[slot], wsem.at[2, slot]))

    def x_copy(row, slot, xs):
        return pltpu.make_async_copy(x_hbm.at[rows(row), :], xbuf.at[slot, xs], xsem.at[slot, xs])

    def y_copy(row, ys):
        return pltpu.make_async_copy(ybuf.at[ys], y_hbm.at[rows(row), :], ysem.at[ys])

    def prefetch(expert, slot):
        for xs in range(MOE_X_SLOTS):
            @pl.when(xs < count_ref[expert])
            def _():
                x_copy(start_ref[expert] + xs * rb, slot, xs).start()
        for copy in w_copies(expert, slot):
            copy.start()

    @pl.when(e == 0)
    def _():
        prefetch(0, 0)
        ybuf[0] = jnp.zeros(ybuf.shape[1:], ybuf.dtype)

        def start(block, carry):
            y_copy(block * rb, 0).start()
            return carry

        def wait(block, carry):
            y_copy(block * rb, 0).wait()
            return carry

        total = y_hbm.shape[0] // rb
        lax.fori_loop(fill_ref[0], total, start, 0)
        lax.fori_loop(fill_ref[0], total, wait, 0)

    @pl.when(e + 1 < n_experts)
    def _():
        prefetch(e + 1, 1 - par)

    for copy in w_copies(e, par):
        copy.wait()
    w1b[...] = w1f[par].astype(BF16)
    w3b[...] = w3f[par].astype(BF16)
    w2b[...] = w2f[par].astype(BF16)

    def body(c, carry):
        xs = c % MOE_X_SLOTS
        ys = c % MOE_Y_SLOTS
        row = first + c * rb

        @pl.when(c < MOE_X_SLOTS)
        def _():
            x_copy(row, par, xs).wait()

        @pl.when(c >= MOE_X_SLOTS)
        def _():
            copy = x_copy(row, par, xs)
            copy.start()
            copy.wait()

        @pl.when(c >= MOE_Y_SLOTS)
        def _():
            y_copy(row - MOE_Y_SLOTS * rb, ys).wait()

        x = xbuf[par, xs].astype(BF16)
        hidden = jax.nn.silu(_dot(x, w1b[...])) * _dot(x, w3b[...])
        ybuf[ys] = _dot(hidden.astype(BF16), w2b[...])
        y_copy(row, ys).start()
        return carry

    lax.fori_loop(0, n_blocks, body, 0)

    for back in range(1, MOE_Y_SLOTS + 1):
        @pl.when(n_blocks >= back)
        def _():
            y_copy(first + (n_blocks - back) * rb, (n_blocks - back) % MOE_Y_SLOTS).wait()


def _moe_experts(row_start, block_count, fill_start, x_rows, w1, w3, w2, l):
    p, d = x_rows.shape
    rb = MOE_ROW_BLOCK
    n_experts, dff = w1.shape[1], w1.shape[-1]
    hbm = pl.BlockSpec(memory_space=pl.ANY)
    grid_spec = pltpu.PrefetchScalarGridSpec(
        num_scalar_prefetch=3,
        grid=(n_experts,),
        in_specs=[hbm, hbm, hbm, hbm],
        out_specs=hbm,
        scratch_shapes=[
            pltpu.VMEM((2, d, dff), F32), pltpu.VMEM((2, d, dff), F32), pltpu.VMEM((2, dff, d), F32),
            pltpu.VMEM((d, dff), BF16), pltpu.VMEM((d, dff), BF16), pltpu.VMEM((dff, d), BF16),
            pltpu.VMEM((2, MOE_X_SLOTS, rb, d), F32), pltpu.VMEM((MOE_Y_SLOTS, rb, d), F32),
            pltpu.SemaphoreType.DMA((3, 2)), pltpu.SemaphoreType.DMA((2, MOE_X_SLOTS)),
            pltpu.SemaphoreType.DMA((MOE_Y_SLOTS,)),
        ],
    )
    return pl.pallas_call(
        functools.partial(_moe_kernel, layer=l),
        out_shape=jax.ShapeDtypeStruct((p, d), F32),
        grid_spec=grid_spec,
        compiler_params=_params("arbitrary"),
        name="moe_experts",
    )(row_start, block_count, fill_start, x_rows, w1, w3, w2)


DISPATCH_DEST1, DISPATCH_DEST2 = 0, 1
META_ROW_START, META_ROW_COUNT = 0, 1


def _dispatch_kernel(route_ref, counts_ref, dest_ref, meta_ref):
    align = float(MOE_ROW_ALIGN)
    counts = jnp.broadcast_to(counts_ref[...], (8, LANES))
    padded = jnp.floor((counts + (align - 1.0)) * (1.0 / align)) * align
    src = lax.broadcasted_iota(jnp.int32, (LANES, LANES), 0)
    dst = lax.broadcasted_iota(jnp.int32, (LANES, LANES), 1)
    earlier = (src < dst).astype(BF16)
    seg_start = sum(_dot(part, earlier) for part in _split_bf16(padded, 3))[0:1]

    route = route_ref[...]
    lane = lax.broadcasted_iota(jnp.int32, route.shape, 1)
    lane_f = lane.astype(F32)

    def field(slot):
        return jnp.sum(jnp.where(lane == slot, route, 0.0), axis=1, keepdims=True)

    def dest(e_slot, rank_slot):
        hit = lane_f == field(e_slot) + float(MOE_GROUPS)
        return jnp.sum(jnp.where(hit, seg_start, 0.0), axis=1, keepdims=True) + field(rank_slot)

    dest1 = dest(ROUTE_E1, ROUTE_RANK1)
    dest2 = dest(ROUTE_E2, ROUTE_RANK2)
    dest_ref[...] = jnp.where(lane == DISPATCH_DEST1, dest1, jnp.where(lane == DISPATCH_DEST2, dest2, 0.0))

    sub = lax.broadcasted_iota(jnp.int32, (8, LANES), 0)
    meta_ref[...] = jnp.where(sub == META_ROW_START, jnp.broadcast_to(seg_start, (8, LANES)),
                              jnp.where(sub == META_ROW_COUNT, padded, 0.0))


def _dispatch(route, counts, *, tm=1024):
    n = route.shape[0]
    tm = min(tm, n)
    return pl.pallas_call(
        _dispatch_kernel,
        out_shape=(jax.ShapeDtypeStruct((n, LANES), F32), jax.ShapeDtypeStruct((8, LANES), F32)),
        grid=(n // tm,),
        in_specs=[pl.BlockSpec((tm, LANES), lambda i: (i, 0)), pl.BlockSpec((1, LANES), lambda i: (0, 0))],
        out_specs=(pl.BlockSpec((tm, LANES), lambda i: (i, 0)), pl.BlockSpec((8, LANES), lambda i: (0, 0))),
        compiler_params=_params("arbitrary"),
        name="moe_dispatch",
    )(route, counts)


def _dispatch_layout(route, counts):
    n = route.shape[0]
    rb = MOE_ROW_BLOCK
    dest, meta = _dispatch(route, counts)
    dest_pair = jnp.concatenate([dest[:, DISPATCH_DEST1], dest[:, DISPATCH_DEST2]]).astype(jnp.int32)
    experts = slice(MOE_GROUPS, MOE_GROUPS + MOE_EXPERTS)
    row_start = meta[META_ROW_START, experts].astype(jnp.int32)
    row_count = meta[META_ROW_COUNT, experts].astype(jnp.int32)
    block_count = (row_count + rb - 1) // rb
    fill_start = (row_start[-1:] + row_count[-1:]) // rb
    n_rows = (n * MOE_TOPK + MOE_EXPERTS * (MOE_ROW_ALIGN - 1) + rb - 1) // rb * rb + rb
    tok = jnp.arange(n, dtype=jnp.int32)
    row_tok = (jnp.arange(n_rows, dtype=jnp.int32) % n).at[dest_pair].set(
        jnp.concatenate([tok, tok]), mode="promise_in_bounds", unique_indices=True)
    return row_tok, row_start, block_count, fill_start, dest_pair


def _combine_kernel(x_ref, y1_ref, y2_ref, route_ref, g_ref, o_ref, *, final_norm):
    route = route_ref[...]
    w1 = route[:, ROUTE_W1:ROUTE_W1 + 1]
    w2 = route[:, ROUTE_W2:ROUTE_W2 + 1]
    out = x_ref[...] + (w1 * y1_ref[...] + w2 * y2_ref[...])
    o_ref[...] = _rms(out, g_ref[...]) if final_norm else out


def _combine(x, y_pair, route, g, *, final_norm, tm=512):
    n, d = x.shape
    tm = min(tm, n)
    steps = n // tm
    rows = lambda width: pl.BlockSpec((tm, width), lambda i: (i, 0))
    return pl.pallas_call(
        functools.partial(_combine_kernel, final_norm=final_norm),
        out_shape=jax.ShapeDtypeStruct((n, d), F32),
        grid=(steps,),
        in_specs=[rows(d), rows(d), pl.BlockSpec((tm, d), lambda i: (steps + i, 0)), rows(LANES),
                  pl.BlockSpec((1, d), lambda i: (0, 0))],
        out_specs=rows(d),
        compiler_params=_params("parallel"),
        name="moe_combine",
    )(x, y_pair, y_pair, route, g)


def _rope_tables(positions):
    inv = 1.0 / (ROPE_THETA ** (jnp.arange(0, MOBA_HEAD_DIM, 2, dtype=F32) / MOBA_HEAD_DIM))
    ang = positions.astype(F32)[:, None] * inv
    cos, sin = jnp.cos(ang), jnp.sin(ang)
    return jnp.concatenate([cos, cos], axis=-1), jnp.concatenate([-sin, sin], axis=-1)


def _layer(x, mem, cos_f, sin_f, p, l, norm_final):
    w_in = p["w_in"]
    w_pool_in = w_in[l, :, W_IN_POOL:]
    w_low = jnp.pad(w_in[l, :, W_IN_LOW:W_IN_POOL], ((0, 0), (0, LANES - GLA_RANK)))
    proj, h = _inproj(x, _layer_vec(p["norm_mix"], l), w_in, w_pool_in, l, cos_f, sin_f)
    o_moba = _moba(proj)
    w_dec = jnp.pad(p["w_gla_decay"][l], ((0, LANES - GLA_RANK), (0, 0))).astype(BF16)
    o_gla = _gla(proj, h, w_low, w_dec, _layer_vec(p["b_gla_decay"], l), _layer_vec(p["gla_norm"], l))
    o_pool = _pool(proj, p["w_pool"][l].astype(BF16), _layer_vec(p["pool_scale"], l))
    merged = _merge(h, o_moba, o_gla, o_pool, p["w_gate"], _layer_vec(p["b_gate"], l), p["w_branch"], l)
    x = _matmul_residual(merged, p["w_mix_out"], l, x)

    kv = _mem_kv(mem, _layer_vec(p["norm_mem"], l), p["w_xkv"], l)
    w_route = jnp.concatenate([p["w_route_group"][l], p["w_route_expert"][l]], axis=1)
    n_route = w_route.shape[1]
    w_route_split = jnp.concatenate(_split_bf16(jnp.pad(w_route, ((0, 0), (0, LANES - n_route))), 2), axis=1)
    b_route = jnp.pad(jnp.concatenate([p["b_route_group"][l], p["b_route_expert"][l]]), (0, LANES - n_route))
    x, h2, route, counts = _xattn(x, _layer_vec(p["norm_xattn"], l), p["w_xq"], kv, p["w_xo"], l,
                                  _layer_vec(p["norm_moe"], l), w_route_split, b_route.reshape(1, -1))

    row_tok, row_start, block_count, fill_start, dest_pair = _dispatch_layout(route, counts)
    x_rows = h2.at[row_tok].get(mode="promise_in_bounds")
    y_rows = _moe_experts(row_start, block_count, fill_start, x_rows,
                          p["w_exp_gate"], p["w_exp_up"], p["w_exp_down"], l)
    y_pair = y_rows.at[dest_pair].get(mode="promise_in_bounds")
    return _combine(x, y_pair, route, norm_final.reshape(1, -1), final_norm=l == p["norm_mix"].shape[0] - 1)


def kernel(x, mem, positions, norm_mix, w_in, w_gla_decay, b_gla_decay, gla_norm, w_pool, pool_scale, w_branch, w_gate, b_gate, w_mix_out, norm_xattn, norm_mem, w_xq, w_xkv, w_xo, norm_moe, w_route_group, b_route_group, w_route_expert, b_route_expert, w_exp_gate, w_exp_up, w_exp_down, norm_final):
    batch, seq, d = x.shape
    assert batch == 1, "kernels are written for a single sequence"
    params = dict(norm_mix=norm_mix, w_in=w_in.astype(BF16), w_gla_decay=w_gla_decay, b_gla_decay=b_gla_decay,
                  gla_norm=gla_norm, w_pool=w_pool, pool_scale=pool_scale, w_branch=w_branch, w_gate=w_gate,
                  b_gate=b_gate, w_mix_out=w_mix_out, norm_xattn=norm_xattn, norm_mem=norm_mem, w_xq=w_xq,
                  w_xkv=w_xkv, w_xo=w_xo, norm_moe=norm_moe, w_route_group=w_route_group,
                  b_route_group=b_route_group, w_route_expert=w_route_expert, b_route_expert=b_route_expert,
                  w_exp_gate=w_exp_gate, w_exp_up=w_exp_up, w_exp_down=w_exp_down)
    cos_f, sin_f = _rope_tables(positions[0])
    xs = x[0]
    for l in range(norm_mix.shape[0]):
        xs = _layer(xs, mem[0], cos_f, sin_f, params, l, norm_final)
    return xs.reshape(batch, seq, d)
```

```python
import functools

import jax
import jax.numpy as jnp
from jax import lax
from jax.experimental import pallas as pl
from jax.experimental.pallas import tpu as pltpu

F32 = jnp.float32
BF16 = jnp.bfloat16

EPS = 1e-6
NEG_INF = -1e30

MOBA_HEADS = 8
MOBA_HEAD_DIM = 128
MOBA_WIDTH = MOBA_HEADS * MOBA_HEAD_DIM
MOBA_BLOCK = 256
MOBA_TOPK = 3
MOBA_KV_BLOCKS = 4
MOBA_HEADS_PER_STEP = 8
MOBA_Q_BLOCKS = 1
ROPE_THETA = 10000.0
MOBA_Q_SCALE = MOBA_HEAD_DIM ** -0.5 * 1.4426950408889634

GLA_HEADS = 4
GLA_DK = 128
GLA_DV = 256
GLA_K_WIDTH = GLA_HEADS * GLA_DK
GLA_V_WIDTH = GLA_HEADS * GLA_DV
GLA_RANK = 16
GLA_TAU = 16.0
GLA_CHUNK = 64

POOL_WINDOWS = (2, 4, 8, 16)
POOL_GROUP_DIM = 256
POOL_WIDTH = len(POOL_WINDOWS) * POOL_GROUP_DIM
POOL_HALO = 16

XATTN_HEADS = 4
XATTN_HEAD_DIM = 128
XATTN_WIDTH = XATTN_HEADS * XATTN_HEAD_DIM

MOE_GROUPS = 8
MOE_EXPERTS_PER_GROUP = 8
MOE_EXPERTS = MOE_GROUPS * MOE_EXPERTS_PER_GROUP
MOE_TOPK = 2
MOE_ROW_BLOCK = 128
MOE_ROW_ALIGN = 8
MOE_X_SLOTS = 4
MOE_Y_SLOTS = 4

LANES = 128

COL_MQ = 0
COL_MK = COL_MQ + MOBA_WIDTH
COL_MV = COL_MK + MOBA_WIDTH
COL_GQ = COL_MV + MOBA_WIDTH
COL_GK = COL_GQ + GLA_K_WIDTH
COL_GV = COL_GK + GLA_K_WIDTH
COL_GR = COL_GV + GLA_V_WIDTH
COL_PU = COL_GR + GLA_V_WIDTH
PROJ_WIDTH = COL_PU + POOL_WIDTH
W_IN_LOW = COL_PU
W_IN_POOL = COL_PU + GLA_RANK

ROUTE_E1, ROUTE_E2, ROUTE_W1, ROUTE_W2, ROUTE_RANK1, ROUTE_RANK2 = range(6)

VMEM_LIMIT = 56 * 1024 * 1024


def _params(*semantics):
    return pltpu.CompilerParams(dimension_semantics=semantics, vmem_limit_bytes=VMEM_LIMIT)


def _rms(x, g):
    return x * lax.rsqrt(jnp.mean(x * x, axis=-1, keepdims=True) + EPS) * g


def _dot(a, b):
    return jnp.dot(a, b, preferred_element_type=F32)


def _dot_nt(a, b):
    return lax.dot_general(a, b, (((1,), (1,)), ((), ())), preferred_element_type=F32)


def _dot_tn(a, b):
    return lax.dot_general(a, b, (((0,), (0,)), ((), ())), preferred_element_type=F32)


def _split_bf16(x, terms):
    parts = []
    for _ in range(terms):
        part = x.astype(BF16)
        parts.append(part)
        x = x - part.astype(F32)
    return parts


def _layer_vec(v, l):
    return v[l].reshape(1, -1)


def _inproj_kernel(x_ref, g_ref, w_ref, wp_ref, cos_ref, sin_ref, o_ref, h_ref, *, tn, n_main):
    j = pl.program_id(1)

    @pl.when(j == 0)
    def _():
        h_ref[...] = _rms(x_ref[...], g_ref[...]).astype(BF16)

    n_rope = (2 * MOBA_WIDTH) // tn
    n_q = MOBA_WIDTH // tn

    @pl.when(j < n_rope)
    def _():
        acc = _dot_nt(h_ref[...], w_ref[...])
        scale = jnp.where(j < n_q, MOBA_Q_SCALE, 1.0).astype(F32)
        cos = cos_ref[...] * scale
        sin = sin_ref[...] * scale
        for hh in range(tn // MOBA_HEAD_DIM):
            cols = slice(hh * MOBA_HEAD_DIM, (hh + 1) * MOBA_HEAD_DIM)
            a = acc[:, cols]
            rot = pltpu.roll(a, MOBA_HEAD_DIM // 2, axis=1)
            o_ref[:, cols] = (a * cos + rot * sin).astype(o_ref.dtype)

    @pl.when((j >= n_rope) & (j < n_main))
    def _():
        o_ref[...] = _dot_nt(h_ref[...], w_ref[...]).astype(o_ref.dtype)

    @pl.when(j >= n_main)
    def _():
        o_ref[...] = _dot_nt(h_ref[...], wp_ref[...]).astype(o_ref.dtype)


def _inproj(x, g, w_in, w_pool_in, l, cos_f, sin_f, *, tm=1024, tn=1024):
    n, d = x.shape
    tm = min(tm, n)
    n_main = COL_PU // tn
    n_tiles = PROJ_WIDTH // tn
    return pl.pallas_call(
        functools.partial(_inproj_kernel, tn=tn, n_main=n_main),
        out_shape=(jax.ShapeDtypeStruct((n, PROJ_WIDTH), BF16), jax.ShapeDtypeStruct((n, d), BF16)),
        grid=(n // tm, n_tiles),
        in_specs=[
            pl.BlockSpec((tm, d), lambda i, j: (i, 0)),
            pl.BlockSpec((1, d), lambda i, j: (0, 0)),
            pl.BlockSpec((None, tn, d), lambda i, j: (l, jnp.minimum(j, n_main - 1), 0)),
            pl.BlockSpec((tn, d), lambda i, j: (jnp.maximum(j - n_main, 0), 0)),
            pl.BlockSpec((tm, MOBA_HEAD_DIM), lambda i, j: (i, 0)),
            pl.BlockSpec((tm, MOBA_HEAD_DIM), lambda i, j: (i, 0)),
        ],
        out_specs=(
            pl.BlockSpec((tm, tn), lambda i, j: (i, j)),
            pl.BlockSpec((tm, d), lambda i, j: (i, 0)),
        ),
        compiler_params=_params("parallel", "arbitrary"),
        name="inproj",
    )(x, g, w_in, w_pool_in, cos_f, sin_f)


def _moba_kernel(q_ref, k_ref, v_ref, o_ref, kmean_ref, *, n_blocks):
    qi = pl.program_id(1)
    blk = MOBA_BLOCK
    hd = MOBA_HEAD_DIM
    q_rows = MOBA_Q_BLOCKS * blk
    tile = MOBA_KV_BLOCKS * blk
    blk_shift = blk.bit_length() - 1
    head_cols = [slice(hh * hd, (hh + 1) * hd) for hh in range(MOBA_HEADS_PER_STEP)]
    first_block = qi * MOBA_Q_BLOCKS

    @pl.when(qi == 0)
    def _():
        kmean_ref[...] = jnp.zeros_like(kmean_ref)

        def fill(b, carry):
            kb = k_ref[pl.ds(pl.multiple_of(b * blk, blk), blk), :].astype(F32)
            kmean_ref[pl.ds(b, 1), :] = jnp.mean(kb, axis=0, keepdims=True)
            return carry

        lax.fori_loop(0, n_blocks, fill, 0)

    lane = lax.broadcasted_iota(jnp.int32, (q_rows, LANES), 1)
    lane_f = lane.astype(F32)
    own_block = first_block + (lax.broadcasted_iota(jnp.int32, (q_rows, LANES), 0) >> blk_shift)
    row = lax.broadcasted_iota(jnp.int32, (blk, blk), 0)
    col = lax.broadcasted_iota(jnp.int32, (blk, blk), 1)
    ones_own = jnp.ones((blk, hd), BF16)
    ones_tile = jnp.ones((tile, hd), BF16)

    q_augs, carry = [], []
    for cols in head_cols:
        q = q_ref[:, cols]
        kmean_hi, kmean_mid = _split_bf16(kmean_ref[:, cols], 2)
        gate = _dot_nt(q, kmean_hi) + _dot_nt(q, kmean_mid)
        gate = jnp.where(lane < own_block, gate, NEG_INF)
        sel = jnp.zeros(gate.shape, jnp.bool_)
        for _ in range(MOBA_TOPK):
            top = jnp.max(gate, axis=1, keepdims=True)
            idx = jnp.min(jnp.where(gate == top, lane_f, float(LANES)), axis=1, keepdims=True)
            hit = lane_f == idx
            sel = sel | (hit & (top > 0.5 * NEG_INF))
            gate = jnp.where(hit, NEG_INF, gate)
        q_augs.append(jnp.concatenate([q, jnp.where(sel, 0.0, NEG_INF).astype(BF16)], axis=1))

        ms, accs = [], []
        for qb in range(MOBA_Q_BLOCKS):
            start = pl.multiple_of((first_block + qb) * blk, blk)
            s = _dot_nt(q[qb * blk:(qb + 1) * blk, :], k_ref[pl.ds(start, blk), cols])
            s = jnp.where(col <= row, s, NEG_INF)
            m = jnp.max(s, axis=1, keepdims=True)
            p = jnp.exp2(s - m).astype(BF16)
            ms.append(m)
            accs.append(_dot(p, jnp.concatenate([v_ref[pl.ds(start, blk), cols], ones_own], axis=1)))
        carry += [jnp.concatenate(ms, axis=0), jnp.concatenate(accs, axis=0)]

    tile_row = lax.broadcasted_iota(jnp.int32, (tile, hd), 0)
    tile_lane = lax.broadcasted_iota(jnp.int32, (tile, hd), 1)

    def body(t, carry):
        off = pl.multiple_of(t * tile, tile)
        block_hot = (tile_lane == t * MOBA_KV_BLOCKS + (tile_row >> blk_shift)).astype(BF16)
        out = []
        for hh, cols in enumerate(head_cols):
            m, acc = carry[2 * hh], carry[2 * hh + 1]
            k_aug = jnp.concatenate([k_ref[pl.ds(off, tile), cols], block_hot], axis=1)
            s = _dot_nt(q_augs[hh], k_aug)
            m_new = jnp.maximum(m, jnp.max(s, axis=1, keepdims=True))
            alpha = jnp.exp2(m - m_new)
            p = jnp.exp2(s - m_new).astype(BF16)
            v_aug = jnp.concatenate([v_ref[pl.ds(off, tile), cols], ones_tile], axis=1)
            out += [m_new, alpha * acc + _dot(p, v_aug)]
        return tuple(out)

    n_tiles = (first_block + MOBA_Q_BLOCKS - 1 + MOBA_KV_BLOCKS - 1) // MOBA_KV_BLOCKS
    carry = lax.fori_loop(0, n_tiles, body, tuple(carry))
    for hh, cols in enumerate(head_cols):
        acc = carry[2 * hh + 1]
        o_ref[:, cols] = (acc[:, :hd] / acc[:, hd:]).astype(o_ref.dtype)


def _moba(proj):
    n = proj.shape[0]
    n_blocks = n // MOBA_BLOCK
    assert n_blocks % MOBA_KV_BLOCKS == 0 and n_blocks % MOBA_Q_BLOCKS == 0 and n_blocks <= LANES
    width = MOBA_HEADS_PER_STEP * MOBA_HEAD_DIM
    q_rows = MOBA_Q_BLOCKS * MOBA_BLOCK
    return pl.pallas_call(
        functools.partial(_moba_kernel, n_blocks=n_blocks),
        out_shape=jax.ShapeDtypeStruct((n, MOBA_WIDTH), BF16),
        grid=(MOBA_HEADS // MOBA_HEADS_PER_STEP, n_blocks // MOBA_Q_BLOCKS),
        in_specs=[
            pl.BlockSpec((q_rows, width), lambda h, i: (i, COL_MQ // width + h)),
            pl.BlockSpec((n, width), lambda h, i: (0, COL_MK // width + h), pipeline_mode=pl.Buffered(1)),
            pl.BlockSpec((n, width), lambda h, i: (0, COL_MV // width + h), pipeline_mode=pl.Buffered(1)),
        ],
        out_specs=pl.BlockSpec((q_rows, width), lambda h, i: (i, h)),
        scratch_shapes=[pltpu.VMEM((LANES, width), F32)],
        compiler_params=_params("parallel", "arbitrary"),
        name="moba",
    )(proj, proj, proj)


def _gla_kernel(q_ref, k_ref, v_ref, r_ref, h_ref, wlow_ref, wdec_ref, bdec_ref, gn_ref, o_ref, state_ref, *, tb):
    i = pl.program_id(0)
    c = GLA_CHUNK
    n_chunks = tb // c
    chunk_shift = c.bit_length() - 1

    @pl.when(i == 0)
    def _():
        state_ref[...] = jnp.zeros_like(state_ref)

    g_low = _dot_nt(h_ref[...], wlow_ref[...]).astype(BF16)
    z = _dot(g_low, wdec_ref[...]) + bdec_ref[...]
    log_a = jax.nn.log_sigmoid(z) / GLA_TAU

    row = lax.broadcasted_iota(jnp.int32, (tb, tb), 0)
    col = lax.broadcasted_iota(jnp.int32, (tb, tb), 1)
    causal = ((row >> chunk_shift) == (col >> chunk_shift)) & (col <= row)
    tri = (lax.broadcasted_iota(jnp.int32, (c, c), 1) <= lax.broadcasted_iota(jnp.int32, (c, c), 0)).astype(BF16)
    parts = _split_bf16(log_a, 3)
    b_chunks, last_chunks = [], []
    for ci in range(n_chunks):
        rows = slice(ci * c, (ci + 1) * c)
        b_c = sum(_dot(tri, part[rows, :]) for part in parts)
        b_chunks.append(b_c)
        last_chunks.append(jnp.broadcast_to(b_c[c - 1:c, :], b_c.shape))
    b = jnp.concatenate(b_chunks, axis=0)
    b_last = jnp.concatenate(last_chunks, axis=0)

    q_dec = (q_ref[...].astype(F32) * (GLA_DK ** -0.5) * jnp.exp(b)).astype(BF16)
    kf = k_ref[...].astype(F32)
    k_inv = (kf * jnp.exp(-b)).astype(BF16)
    k_end = (kf * jnp.exp(b_last - b)).astype(BF16)
    decay = jnp.exp(b_last)

    for h in range(GLA_HEADS):
        kc = slice(h * GLA_DK, (h + 1) * GLA_DK)
        vc = slice(h * GLA_DV, (h + 1) * GLA_DV)
        v = v_ref[:, vc]
        attn = jnp.where(causal, _dot_nt(q_dec[:, kc], k_inv[:, kc]), 0.0)
        o = _dot(attn.astype(BF16), v)
        state = state_ref[h]
        inter = []
        for ci in range(n_chunks):
            rows = slice(ci * c, (ci + 1) * c)
            inter.append(_dot_nt(q_dec[rows, kc], state.astype(BF16)))
            update = _dot_tn(v[rows, :], k_end[rows, kc])
            state = decay[ci * c:ci * c + 1, kc] * state + update
        state_ref[h] = state
        o = o + jnp.concatenate(inter, axis=0)
        o = _rms(o, gn_ref[...])
        o_ref[:, vc] = (o * jax.nn.silu(r_ref[:, vc].astype(F32))).astype(o_ref.dtype)


def _gla(proj, h, w_low, w_dec, b_dec, gla_norm, *, tb=512):
    n, d = h.shape
    tb = min(tb, n)
    return pl.pallas_call(
        functools.partial(_gla_kernel, tb=tb),
        out_shape=jax.ShapeDtypeStruct((n, GLA_V_WIDTH), BF16),
        grid=(n // tb,),
        in_specs=[
            pl.BlockSpec((tb, GLA_K_WIDTH), lambda i: (i, COL_GQ // GLA_K_WIDTH)),
            pl.BlockSpec((tb, GLA_K_WIDTH), lambda i: (i, COL_GK // GLA_K_WIDTH)),
            pl.BlockSpec((tb, GLA_V_WIDTH), lambda i: (i, COL_GV // GLA_V_WIDTH)),
            pl.BlockSpec((tb, GLA_V_WIDTH), lambda i: (i, COL_GR // GLA_V_WIDTH)),
            pl.BlockSpec((tb, d), lambda i: (i, 0)),
            pl.BlockSpec((LANES, d), lambda i: (0, 0)),
            pl.BlockSpec((LANES, GLA_K_WIDTH), lambda i: (0, 0)),
            pl.BlockSpec((1, GLA_K_WIDTH), lambda i: (0, 0)),
            pl.BlockSpec((1, GLA_DV), lambda i: (0, 0)),
        ],
        out_specs=pl.BlockSpec((tb, GLA_V_WIDTH), lambda i: (i, 0)),
        scratch_shapes=[pltpu.VMEM((GLA_HEADS, GLA_DV, GLA_DK), F32)],
        compiler_params=_params("arbitrary"),
        name="gla",
    )(proj, proj, proj, proj, h, w_low, w_dec, b_dec, gla_norm)


def _pool_kernel(u_ref, halo_ref, w_ref, sc_ref, o_ref, ext_ref, lvl_ref, *, tb):
    i = pl.program_id(0)
    pad = POOL_HALO
    body = POOL_HALO + tb
    first = pad + POOL_HALO
    ext_ref[pl.ds(0, pad), :] = jnp.zeros((pad, POOL_WIDTH), F32)
    lvl_ref[:, pl.ds(0, pad), :] = jnp.zeros((2, pad, POOL_GROUP_DIM), F32)
    ext_ref[pl.ds(pad, POOL_HALO), :] = jnp.where(i == 0, 0.0, halo_ref[...].astype(F32))
    ext_ref[pl.ds(first, tb), :] = u_ref[...].astype(F32)
    t = i * tb + lax.broadcasted_iota(jnp.int32, (tb, 1), 0)
    for g, win in enumerate(POOL_WINDOWS):
        cols = slice(g * POOL_GROUP_DIM, (g + 1) * POOL_GROUP_DIM)

        def read(level, start, rows, cols=cols):
            if level is None:
                return ext_ref[pl.ds(start, rows), cols]
            return lvl_ref[level, pl.ds(start, rows), :]

        u = read(None, first, tb)
        level, span, stored = None, 1, 0
        while 2 * span < win:
            doubled = read(level, pad, body) + read(level, pad - span, body)
            lvl_ref[stored % 2, pl.ds(pad, body), :] = doubled
            level, span, stored = stored % 2, 2 * span, stored + 1
        window_sum = read(level, first, tb) + read(level, first - span, tb)
        count = jnp.minimum(t + 1, win).astype(F32)
        mixed = window_sum / count - u
        y = _dot(mixed.astype(BF16), w_ref[g])
        o_ref[:, cols] = (y * sc_ref[:, cols]).astype(o_ref.dtype)


def _pool(proj, w_pool, pool_scale, *, tb=512):
    n = proj.shape[0]
    tb = min(tb, n)
    halo_per_block = tb // POOL_HALO
    return pl.pallas_call(
        functools.partial(_pool_kernel, tb=tb),
        out_shape=jax.ShapeDtypeStruct((n, POOL_WIDTH), BF16),
        grid=(n // tb,),
        in_specs=[
            pl.BlockSpec((tb, POOL_WIDTH), lambda i: (i, COL_PU // POOL_WIDTH)),
            pl.BlockSpec((POOL_HALO, POOL_WIDTH),
                         lambda i: (jnp.maximum(i * halo_per_block - 1, 0), COL_PU // POOL_WIDTH)),
            pl.BlockSpec(w_pool.shape, lambda i: (0, 0, 0)),
            pl.BlockSpec((1, POOL_WIDTH), lambda i: (0, 0)),
        ],
        out_specs=pl.BlockSpec((tb, POOL_WIDTH), lambda i: (i, 0)),
        scratch_shapes=[pltpu.VMEM((2 * POOL_HALO + tb, POOL_WIDTH), F32),
                        pltpu.VMEM((2, 2 * POOL_HALO + tb, POOL_GROUP_DIM), F32)],
        compiler_params=_params("parallel"),
        name="pool",
    )(proj, proj, w_pool, pool_scale)


def _merge_kernel(h_ref, a_ref, b_ref, c_ref, wg0, wg1, wg2, bg0, bg1, bg2, wb0, wb1, wb2, o_ref):
    h = h_ref[...]
    total = None
    for br_ref, wg, bg, wb in ((a_ref, wg0, bg0, wb0), (b_ref, wg1, bg1, wb1), (c_ref, wg2, bg2, wb2)):
        gate = jax.nn.sigmoid(_dot(h, wg[...].astype(BF16)) + bg[...])
        term = gate * _dot(br_ref[...], wb[...].astype(BF16))
        total = term if total is None else total + term
    o_ref[...] = total.astype(o_ref.dtype)


def _merge(h, o_moba, o_gla, o_pool, w_gate, b_gate, w_branch, l, *, tm=1024, tn=256):
    n, d = h.shape
    tm = min(tm, n)
    bw = o_moba.shape[1]
    tiles = d // tn
    row_spec = lambda width: pl.BlockSpec((tm, width), lambda i, j: (i, 0))
    gate_specs = [pl.BlockSpec((None, d, tn), lambda i, j, k=k: (l, 0, k * tiles + j)) for k in range(3)]
    bias_specs = [pl.BlockSpec((1, tn), lambda i, j, k=k: (0, k * tiles + j)) for k in range(3)]
    branch_specs = [pl.BlockSpec((None, None, bw, tn), lambda i, j, k=k: (l, k, 0, j)) for k in range(3)]
    return pl.pallas_call(
        _merge_kernel,
        out_shape=jax.ShapeDtypeStruct((n, d), BF16),
        grid=(n // tm, tiles),
        in_specs=[row_spec(d), row_spec(bw), row_spec(bw), row_spec(bw)] + gate_specs + bias_specs + branch_specs,
        out_specs=pl.BlockSpec((tm, tn), lambda i, j: (i, j)),
        compiler_params=_params("parallel", "arbitrary"),
        name="merge",
    )(h, o_moba, o_gla, o_pool, w_gate, w_gate, w_gate, b_gate, b_gate, b_gate, w_branch, w_branch, w_branch)


def _matmul_residual_kernel(a_ref, w_ref, x_ref, o_ref):
    o_ref[...] = x_ref[...] + _dot(a_ref[...], w_ref[...].astype(BF16))


def _matmul_residual(a, w, l, x, *, tm=2048, tn=512):
    n, k = a.shape
    d = w.shape[-1]
    tm = min(tm, n)
    return pl.pallas_call(
        _matmul_residual_kernel,
        out_shape=jax.ShapeDtypeStruct((n, d), F32),
        grid=(n // tm, d // tn),
        in_specs=[
            pl.BlockSpec((tm, k), lambda i, j: (i, 0)),
            pl.BlockSpec((None, k, tn), lambda i, j: (l, 0, j)),
            pl.BlockSpec((tm, tn), lambda i, j: (i, j)),
        ],
        out_specs=pl.BlockSpec((tm, tn), lambda i, j: (i, j)),
        compiler_params=_params("parallel", "arbitrary"),
        name="mix_out",
    )(a, w, x)


def _norm_matmul_kernel(x_ref, g_ref, w_ref, o_ref):
    o_ref[...] = _dot(_rms(x_ref[...], g_ref[...]).astype(BF16), w_ref[...].astype(BF16)).astype(o_ref.dtype)


def _mem_kv(mem, g, w_xkv, l):
    n, d = mem.shape
    width = w_xkv.shape[-1]
    return pl.pallas_call(
        _norm_matmul_kernel,
        out_shape=jax.ShapeDtypeStruct((n, width), BF16),
        grid=(1,),
        in_specs=[
            pl.BlockSpec((n, d), lambda i: (0, 0)),
            pl.BlockSpec((1, d), lambda i: (0, 0)),
            pl.BlockSpec((None, d, width), lambda i: (l, 0, 0)),
        ],
        out_specs=pl.BlockSpec((n, width), lambda i: (0, 0)),
        compiler_params=_params("arbitrary"),
        name="mem_kv",
    )(mem, g, w_xkv)


def _xattn_kernel(x_ref, g_ref, wq_ref, kv_ref, wo_ref, gm_ref, wr_ref, br_ref,
                  xo_ref, h2_ref, route_ref, counts_ref, wq_b, wo_b, counts_acc, *, tm):
    i = pl.program_id(0)

    @pl.when(i == 0)
    def _():
        wq_b[...] = wq_ref[...].astype(BF16)
        wo_b[...] = wo_ref[...].astype(BF16)
        counts_acc[...] = jnp.zeros_like(counts_acc)

    x = x_ref[...]
    h = _rms(x, g_ref[...]).astype(BF16)
    q = (_dot(h, wq_b[...]) * (XATTN_HEAD_DIM ** -0.5)).astype(BF16)
    heads = []
    for hd in range(XATTN_HEADS):
        kc = slice(hd * XATTN_HEAD_DIM, (hd + 1) * XATTN_HEAD_DIM)
        vc = slice(XATTN_WIDTH + hd * XATTN_HEAD_DIM, XATTN_WIDTH + (hd + 1) * XATTN_HEAD_DIM)
        s = _dot_nt(q[:, kc], kv_ref[:, kc])
        p = jnp.exp(s - jnp.max(s, axis=1, keepdims=True))
        p = p / jnp.sum(p, axis=1, keepdims=True)
        heads.append(_dot(p.astype(BF16), kv_ref[:, vc]).astype(BF16))
    o = jnp.concatenate(heads, axis=1)
    xn = x + _dot(o, wo_b[...])
    xo_ref[...] = xn
    h2 = _rms(xn, gm_ref[...])
    h2_ref[...] = h2

    h_hi, h_mid = _split_bf16(h2, 2)
    hi_terms = _dot(h_hi, wr_ref[...])
    lg = hi_terms[:, :LANES] + (hi_terms[:, LANES:] + _dot(h_mid, wr_ref[:, :LANES])) + br_ref[...]

    lane = lax.broadcasted_iota(jnp.int32, lg.shape, 1)
    lane_f = lane.astype(F32)
    big = float(LANES)

    def top1(v):
        top = jnp.max(v, axis=1, keepdims=True)
        return top, jnp.min(jnp.where(v == top, lane_f, big), axis=1, keepdims=True)

    g_logit = jnp.where(lane < MOE_GROUPS, lg, NEG_INF)
    g_top, g_idx = top1(g_logit)
    g_w = 1.0 / jnp.sum(jnp.exp(g_logit - g_top), axis=1, keepdims=True)
    first = MOE_GROUPS + MOE_EXPERTS_PER_GROUP * g_idx
    e_logit = jnp.where((lane_f >= first) & (lane_f < first + MOE_EXPERTS_PER_GROUP), lg, NEG_INF)
    e_top1, lane1 = top1(e_logit)
    e_top2, lane2 = top1(jnp.where(lane_f == lane1, NEG_INF, e_logit))
    z = jnp.sum(jnp.exp(e_logit - e_top1), axis=1, keepdims=True)
    p1 = 1.0 / z
    p2 = jnp.exp(e_top2 - e_top1) / z
    w1 = g_w * p1 / (p1 + p2)
    w2 = g_w * p2 / (p1 + p2)

    two_hot = (lane_f == lane1) | (lane_f == lane2)
    row = lax.broadcasted_iota(jnp.int32, (tm, tm), 0)
    col = lax.broadcasted_iota(jnp.int32, (tm, tm), 1)
    before = _dot((col < row).astype(BF16), two_hot.astype(BF16)) + counts_acc[...]
    rank1 = jnp.sum(jnp.where(lane_f == lane1, before, 0.0), axis=1, keepdims=True)
    rank2 = jnp.sum(jnp.where(lane_f == lane2, before, 0.0), axis=1, keepdims=True)
    counts_acc[...] += jnp.sum(two_hot.astype(F32), axis=0, keepdims=True)
    counts_ref[...] = counts_acc[...]

    record = jnp.zeros(lg.shape, F32)
    for slot, val in ((ROUTE_E1, lane1 - MOE_GROUPS), (ROUTE_E2, lane2 - MOE_GROUPS), (ROUTE_W1, w1),
                      (ROUTE_W2, w2), (ROUTE_RANK1, rank1), (ROUTE_RANK2, rank2)):
        record = jnp.where(lane == slot, val, record)
    route_ref[...] = record


def _xattn(x, g, wq, kv, wo, l, g_moe, w_route_split, b_route, *, tm=512):
    n, d = x.shape
    tm = min(tm, n)
    full = lambda a: pl.BlockSpec(a.shape, lambda i: (0,) * a.ndim)
    layer = lambda a: pl.BlockSpec((None,) + a.shape[1:], lambda i: (l,) + (0,) * (a.ndim - 1))
    rows = lambda width: pl.BlockSpec((tm, width), lambda i: (i, 0))
    return pl.pallas_call(
        functools.partial(_xattn_kernel, tm=tm),
        out_shape=(jax.ShapeDtypeStruct((n, d), F32), jax.ShapeDtypeStruct((n, d), F32),
                   jax.ShapeDtypeStruct((n, LANES), F32), jax.ShapeDtypeStruct((1, LANES), F32)),
        grid=(n // tm,),
        in_specs=[rows(d), full(g), layer(wq), full(kv), layer(wo), full(g_moe), full(w_route_split), full(b_route)],
        out_specs=(rows(d), rows(d), rows(LANES), pl.BlockSpec((1, LANES), lambda i: (0, 0))),
        scratch_shapes=[pltpu.VMEM(wq.shape[1:], BF16), pltpu.VMEM(wo.shape[1:], BF16),
                        pltpu.VMEM((1, LANES), F32)],
        compiler_params=_params("arbitrary"),
        name="xattn_route",
    )(x, g, wq, kv, wo, g_moe, w_route_split, b_route)


def _moe_kernel(start_ref, count_ref, fill_ref, x_hbm, w1_hbm, w3_hbm, w2_hbm, y_hbm,
                w1f, w3f, w2f, w1b, w3b, w2b, xbuf, ybuf, wsem, xsem, ysem, *, layer):
    e = pl.program_id(0)
    n_experts = pl.num_programs(0)
    rb = MOE_ROW_BLOCK
    par = e % 2
    first = start_ref[e]
    n_blocks = count_ref[e]

    def rows(row):
        return pl.ds(pl.multiple_of(row, MOE_ROW_ALIGN), rb)

    def w_copies(expert, slot):
        return (pltpu.make_async_copy(w1_hbm.at[layer, expert], w1f.at[slot], wsem.at[0, slot]),
                pltpu.make_async_copy(w3_hbm.at[layer, expert], w3f.at[slot], wsem.at[1, slot]),
                pltpu.make_async_copy(w2_hbm.at[layer, expert], w2f.at[slot], wsem.at[2, slot]))

    def x_copy(row, slot, xs):
        return pltpu.make_async_copy(x_hbm.at[rows(row), :], xbuf.at[slot, xs], xsem.at[slot, xs])

    def y_copy(row, ys):
        return pltpu.make_async_copy(ybuf.at[ys], y_hbm.at[rows(row), :], ysem.at[ys])

    def prefetch(expert, slot):
        for xs in range(MOE_X_SLOTS):
            @pl.when(xs < count_ref[expert])
            def _():
                x_copy(start_ref[expert] + xs * rb, slot, xs).start()
        for copy in w_copies(expert, slot):
            copy.start()

    @pl.when(e == 0)
    def _():
        prefetch(0, 0)
        ybuf[0] = jnp.zeros(ybuf.shape[1:], ybuf.dtype)

        def start(block, carry):
            y_copy(block * rb, 0).start()
            return carry

        def wait(block, carry):
            y_copy(block * rb, 0).wait()
            return carry

        total = y_hbm.shape[0] // rb
        lax.fori_loop(fill_ref[0], total, start, 0)
        lax.fori_loop(fill_ref[0], total, wait, 0)

    @pl.when(e + 1 < n_experts)
    def _():
        prefetch(e + 1, 1 - par)

    for copy in w_copies(e, par):
        copy.wait()
    w1b[...] = w1f[par].astype(BF16)
    w3b[...] = w3f[par].astype(BF16)
    w2b[...] = w2f[par].astype(BF16)

    def body(c, carry):
        xs = c % MOE_X_SLOTS
        ys = c % MOE_Y_SLOTS
        row = first + c * rb

        @pl.when(c < MOE_X_SLOTS)
        def _():
            x_copy(row, par, xs).wait()

        @pl.when(c >= MOE_X_SLOTS)
        def _():
            copy = x_copy(row, par, xs)
            copy.start()
            copy.wait()

        @pl.when(c >= MOE_Y_SLOTS)
        def _():
            y_copy(row - MOE_Y_SLOTS * rb, ys).wait()

        x = xbuf[par, xs].astype(BF16)
        hidden = jax.nn.silu(_dot(x, w1b[...])) * _dot(x, w3b[...])
        ybuf[ys] = _dot(hidden.astype(BF16), w2b[...])
        y_copy(row, ys).start()
        return carry

    lax.fori_loop(0, n_blocks, body, 0)

    for back in range(1, MOE_Y_SLOTS + 1):
        @pl.when(n_blocks >= back)
        def _():
            y_copy(first + (n_blocks - back) * rb, (n_blocks - back) % MOE_Y_SLOTS).wait()


def _moe_experts(row_start, block_count, fill_start, x_rows, w1, w3, w2, l):
    p, d = x_rows.shape
    rb = MOE_ROW_BLOCK
    n_experts, dff = w1.shape[1], w1.shape[-1]
    hbm = pl.BlockSpec(memory_space=pl.ANY)
    grid_spec = pltpu.PrefetchScalarGridSpec(
        num_scalar_prefetch=3,
        grid=(n_experts,),
        in_specs=[hbm, hbm, hbm, hbm],
        out_specs=hbm,
        scratch_shapes=[
            pltpu.VMEM((2, d, dff), F32), pltpu.VMEM((2, d, dff), F32), pltpu.VMEM((2, dff, d), F32),
            pltpu.VMEM((d, dff), BF16), pltpu.VMEM((d, dff), BF16), pltpu.VMEM((dff, d), BF16),
            pltpu.VMEM((2, MOE_X_SLOTS, rb, d), F32), pltpu.VMEM((MOE_Y_SLOTS, rb, d), F32),
            pltpu.SemaphoreType.DMA((3, 2)), pltpu.SemaphoreType.DMA((2, MOE_X_SLOTS)),
            pltpu.SemaphoreType.DMA((MOE_Y_SLOTS,)),
        ],
    )
    return pl.pallas_call(
        functools.partial(_moe_kernel, layer=l),
        out_shape=jax.ShapeDtypeStruct((p, d), F32),
        grid_spec=grid_spec,
        compiler_params=_params("arbitrary"),
        name="moe_experts",
    )(row_start, block_count, fill_start, x_rows, w1, w3, w2)


DISPATCH_DEST1, DISPATCH_DEST2 = 0, 1
META_ROW_START, META_ROW_COUNT = 0, 1


def _dispatch_kernel(route_ref, counts_ref, dest_ref, meta_ref):
    align = float(MOE_ROW_ALIGN)
    counts = jnp.broadcast_to(counts_ref[...], (8, LANES))
    padded = jnp.floor((counts + (align - 1.0)) * (1.0 / align)) * align
    src = lax.broadcasted_iota(jnp.int32, (LANES, LANES), 0)
    dst = lax.broadcasted_iota(jnp.int32, (LANES, LANES), 1)
    earlier = (src < dst).astype(BF16)
    seg_start = sum(_dot(part, earlier) for part in _split_bf16(padded, 3))[0:1]

    route = route_ref[...]
    lane = lax.broadcasted_iota(jnp.int32, route.shape, 1)
    lane_f = lane.astype(F32)

    def field(slot):
        return jnp.sum(jnp.where(lane == slot, route, 0.0), axis=1, keepdims=True)

    def dest(e_slot, rank_slot):
        hit = lane_f == field(e_slot) + float(MOE_GROUPS)
        return jnp.sum(jnp.where(hit, seg_start, 0.0), axis=1, keepdims=True) + field(rank_slot)

    dest1 = dest(ROUTE_E1, ROUTE_RANK1)
    dest2 = dest(ROUTE_E2, ROUTE_RANK2)
    dest_ref[...] = jnp.where(lane == DISPATCH_DEST1, dest1, jnp.where(lane == DISPATCH_DEST2, dest2, 0.0))

    sub = lax.broadcasted_iota(jnp.int32, (8, LANES), 0)
    meta_ref[...] = jnp.where(sub == META_ROW_START, jnp.broadcast_to(seg_start, (8, LANES)),
                              jnp.where(sub == META_ROW_COUNT, padded, 0.0))


def _dispatch(route, counts, *, tm=1024):
    n = route.shape[0]
    tm = min(tm, n)
    return pl.pallas_call(
        _dispatch_kernel,
        out_shape=(jax.ShapeDtypeStruct((n, LANES), F32), jax.ShapeDtypeStruct((8, LANES), F32)),
        grid=(n // tm,),
        in_specs=[pl.BlockSpec((tm, LANES), lambda i: (i, 0)), pl.BlockSpec((1, LANES), lambda i: (0, 0))],
        out_specs=(pl.BlockSpec((tm, LANES), lambda i: (i, 0)), pl.BlockSpec((8, LANES), lambda i: (0, 0))),
        compiler_params=_params("arbitrary"),
        name="moe_dispatch",
    )(route, counts)


def _dispatch_layout(route, counts):
    n = route.shape[0]
    rb = MOE_ROW_BLOCK
    dest, meta = _dispatch(route, counts)
    dest_pair = jnp.concatenate([dest[:, DISPATCH_DEST1], dest[:, DISPATCH_DEST2]]).astype(jnp.int32)
    experts = slice(MOE_GROUPS, MOE_GROUPS + MOE_EXPERTS)
    row_start = meta[META_ROW_START, experts].astype(jnp.int32)
    row_count = meta[META_ROW_COUNT, experts].astype(jnp.int32)
    block_count = (row_count + rb - 1) // rb
    fill_start = (row_start[-1:] + row_count[-1:]) // rb
    n_rows = (n * MOE_TOPK + MOE_EXPERTS * (MOE_ROW_ALIGN - 1) + rb - 1) // rb * rb + rb
    tok = jnp.arange(n, dtype=jnp.int32)
    row_tok = (jnp.arange(n_rows, dtype=jnp.int32) % n).at[dest_pair].set(
        jnp.concatenate([tok, tok]), mode="promise_in_bounds", unique_indices=True)
    return row_tok, row_start, block_count, fill_start, dest_pair


def _combine_kernel(x_ref, y1_ref, y2_ref, route_ref, g_ref, o_ref, *, final_norm):
    route = route_ref[...]
    w1 = route[:, ROUTE_W1:ROUTE_W1 + 1]
    w2 = route[:, ROUTE_W2:ROUTE_W2 + 1]
    out = x_ref[...] + (w1 * y1_ref[...] + w2 * y2_ref[...])
    o_ref[...] = _rms(out, g_ref[...]) if final_norm else out


def _combine(x, y_pair, route, g, *, final_norm, tm=512):
    n, d = x.shape
    tm = min(tm, n)
    steps = n // tm
    rows = lambda width: pl.BlockSpec((tm, width), lambda i: (i, 0))
    return pl.pallas_call(
        functools.partial(_combine_kernel, final_norm=final_norm),
        out_shape=jax.ShapeDtypeStruct((n, d), F32),
        grid=(steps,),
        in_specs=[rows(d), rows(d), pl.BlockSpec((tm, d), lambda i: (steps + i, 0)), rows(LANES),
                  pl.BlockSpec((1, d), lambda i: (0, 0))],
        out_specs=rows(d),
        compiler_params=_params("parallel"),
        name="moe_combine",
    )(x, y_pair, y_pair, route, g)


def _rope_tables(positions):
    inv = 1.0 / (ROPE_THETA ** (jnp.arange(0, MOBA_HEAD_DIM, 2, dtype=F32) / MOBA_HEAD_DIM))
    ang = positions.astype(F32)[:, None] * inv
    cos, sin = jnp.cos(ang), jnp.sin(ang)
    return jnp.concatenate([cos, cos], axis=-1), jnp.concatenate([-sin, sin], axis=-1)


def _layer(x, mem, cos_f, sin_f, p, l, norm_final):
    w_in = p["w_in"]
    w_pool_in = w_in[l, W_IN_POOL:, :]
    w_low = jnp.pad(w_in[l, W_IN_LOW:W_IN_POOL, :], ((0, LANES - GLA_RANK), (0, 0)))
    proj, h = _inproj(x, _layer_vec(p["norm_mix"], l), w_in, w_pool_in, l, cos_f, sin_f)
    o_moba = _moba(proj)
    w_dec = jnp.pad(p["w_gla_decay"][l], ((0, LANES - GLA_RANK), (0, 0))).astype(BF16)
    o_gla = _gla(proj, h, w_low, w_dec, _layer_vec(p["b_gla_decay"], l), _layer_vec(p["gla_norm"], l))
    o_pool = _pool(proj, p["w_pool"][l].astype(BF16), _layer_vec(p["pool_scale"], l))
    merged = _merge(h, o_moba, o_gla, o_pool, p["w_gate"], _layer_vec(p["b_gate"], l), p["w_branch"], l)
    x = _matmul_residual(merged, p["w_mix_out"], l, x)

    kv = _mem_kv(mem, _layer_vec(p["norm_mem"], l), p["w_xkv"], l)
    w_route = jnp.concatenate([p["w_route_group"][l], p["w_route_expert"][l]], axis=1)
    n_route = w_route.shape[1]
    w_route_split = jnp.concatenate(_split_bf16(jnp.pad(w_route, ((0, 0), (0, LANES - n_route))), 2), axis=1)
    b_route = jnp.pad(jnp.concatenate([p["b_route_group"][l], p["b_route_expert"][l]]), (0, LANES - n_route))
    x, h2, route, counts = _xattn(x, _layer_vec(p["norm_xattn"], l), p["w_xq"], kv, p["w_xo"], l,
                                  _layer_vec(p["norm_moe"], l), w_route_split, b_route.reshape(1, -1))

    row_tok, row_start, block_count, fill_start, dest_pair = _dispatch_layout(route, counts)
    x_rows = h2.at[row_tok].get(mode="promise_in_bounds")
    y_rows = _moe_experts(row_start, block_count, fill_start, x_rows,
                          p["w_exp_gate"], p["w_exp_up"], p["w_exp_down"], l)
    y_pair = y_rows.at[dest_pair].get(mode="promise_in_bounds")
    return _combine(x, y_pair, route, norm_final.reshape(1, -1), final_norm=l == p["norm_mix"].shape[0] - 1)


def kernel(x, mem, positions, norm_mix, w_in, w_gla_decay, b_gla_decay, gla_norm, w_pool, pool_scale, w_branch, w_gate, b_gate, w_mix_out, norm_xattn, norm_mem, w_xq, w_xkv, w_xo, norm_moe, w_route_group, b_route_group, w_route_expert, b_route_expert, w_exp_gate, w_exp_up, w_exp_down, norm_final):
    batch, seq, d = x.shape
    assert batch == 1, "kernels are written for a single sequence"
    params = dict(norm_mix=norm_mix, w_in=jnp.swapaxes(w_in, 1, 2).astype(BF16), w_gla_decay=w_gla_decay, b_gla_decay=b_gla_decay,
                  gla_norm=gla_norm, w_pool=w_pool, pool_scale=pool_scale, w_branch=w_branch, w_gate=w_gate,
                  b_gate=b_gate, w_mix_out=w_mix_out, norm_xattn=norm_xattn, norm_mem=norm_mem, w_xq=w_xq,
                  w_xkv=w_xkv, w_xo=w_xo, norm_moe=norm_moe, w_route_group=w_route_group,
                  b_route_group=b_route_group, w_route_expert=w_route_expert, b_route_expert=b_route_expert,
                  w_exp_gate=w_exp_gate, w_exp_up=w_exp_up, w_exp_down=w_exp_down)
    cos_f, sin_f = _rope_tables(positions[0])
    xs = x[0]
    for l in range(norm_mix.shape[0]):
        xs = _layer(xs, mem[0], cos_f, sin_f, params, l, norm_final)
    return xs.reshape(batch, seq, d)
```

```python
import functools

import jax
import jax.numpy as jnp
from jax import lax
from jax.experimental import pallas as pl
from jax.experimental.pallas import tpu as pltpu

F32 = jnp.float32
BF16 = jnp.bfloat16

EPS = 1e-6
NEG_INF = -1e30

MOBA_HEADS = 8
MOBA_HEAD_DIM = 128
MOBA_WIDTH = MOBA_HEADS * MOBA_HEAD_DIM
MOBA_BLOCK = 256
MOBA_TOPK = 3
MOBA_KV_BLOCKS = 8
MOBA_HEADS_PER_STEP = 8
MOBA_Q_BLOCKS = 1
ROPE_THETA = 10000.0
MOBA_Q_SCALE = MOBA_HEAD_DIM ** -0.5 * 1.4426950408889634

GLA_HEADS = 4
GLA_DK = 128
GLA_DV = 256
GLA_K_WIDTH = GLA_HEADS * GLA_DK
GLA_V_WIDTH = GLA_HEADS * GLA_DV
GLA_RANK = 16
GLA_TAU = 16.0
GLA_CHUNK = 64

POOL_WINDOWS = (2, 4, 8, 16)
POOL_GROUP_DIM = 256
POOL_WIDTH = len(POOL_WINDOWS) * POOL_GROUP_DIM
POOL_HALO = 16

XATTN_HEADS = 4
XATTN_HEAD_DIM = 128
XATTN_WIDTH = XATTN_HEADS * XATTN_HEAD_DIM

MOE_GROUPS = 8
MOE_EXPERTS_PER_GROUP = 8
MOE_EXPERTS = MOE_GROUPS * MOE_EXPERTS_PER_GROUP
MOE_TOPK = 2
MOE_ROW_BLOCK = 128
MOE_ROW_ALIGN = 8
MOE_X_SLOTS = 4
MOE_Y_SLOTS = 4

LANES = 128

COL_MQ = 0
COL_MK = COL_MQ + MOBA_WIDTH
COL_MV = COL_MK + MOBA_WIDTH
COL_GQ = COL_MV + MOBA_WIDTH
COL_GK = COL_GQ + GLA_K_WIDTH
COL_GV = COL_GK + GLA_K_WIDTH
COL_GR = COL_GV + GLA_V_WIDTH
COL_PU = COL_GR + GLA_V_WIDTH
PROJ_WIDTH = COL_PU + POOL_WIDTH
W_IN_LOW = COL_PU
W_IN_POOL = COL_PU + GLA_RANK

ROUTE_E1, ROUTE_E2, ROUTE_W1, ROUTE_W2, ROUTE_RANK1, ROUTE_RANK2 = range(6)

VMEM_LIMIT = 56 * 1024 * 1024


def _params(*semantics):
    return pltpu.CompilerParams(dimension_semantics=semantics, vmem_limit_bytes=VMEM_LIMIT)


def _rms(x, g):
    return x * lax.rsqrt(jnp.mean(x * x, axis=-1, keepdims=True) + EPS) * g


def _dot(a, b):
    return jnp.dot(a, b, preferred_element_type=F32)


def _dot_nt(a, b):
    return lax.dot_general(a, b, (((1,), (1,)), ((), ())), preferred_element_type=F32)


def _dot_tn(a, b):
    return lax.dot_general(a, b, (((0,), (0,)), ((), ())), preferred_element_type=F32)


def _split_bf16(x, terms):
    parts = []
    for _ in range(terms):
        part = x.astype(BF16)
        parts.append(part)
        x = x - part.astype(F32)
    return parts


def _layer_vec(v, l):
    return v[l].reshape(1, -1)


def _inproj_kernel(x_ref, g_ref, w_ref, wp_ref, cos_ref, sin_ref, o_ref, h_ref, *, tn, n_main):
    j = pl.program_id(1)

    @pl.when(j == 0)
    def _():
        h_ref[...] = _rms(x_ref[...], g_ref[...]).astype(BF16)

    n_rope = (2 * MOBA_WIDTH) // tn
    n_q = MOBA_WIDTH // tn

    @pl.when(j < n_rope)
    def _():
        acc = _dot_nt(h_ref[...], w_ref[...])
        scale = jnp.where(j < n_q, MOBA_Q_SCALE, 1.0).astype(F32)
        cos = cos_ref[...] * scale
        sin = sin_ref[...] * scale
        for hh in range(tn // MOBA_HEAD_DIM):
            cols = slice(hh * MOBA_HEAD_DIM, (hh + 1) * MOBA_HEAD_DIM)
            a = acc[:, cols]
            rot = pltpu.roll(a, MOBA_HEAD_DIM // 2, axis=1)
            o_ref[:, cols] = (a * cos + rot * sin).astype(o_ref.dtype)

    @pl.when((j >= n_rope) & (j < n_main))
    def _():
        o_ref[...] = _dot_nt(h_ref[...], w_ref[...]).astype(o_ref.dtype)

    @pl.when(j >= n_main)
    def _():
        o_ref[...] = _dot_nt(h_ref[...], wp_ref[...]).astype(o_ref.dtype)


def _inproj(x, g, w_in, w_pool_in, l, cos_f, sin_f, *, tm=1024, tn=1024):
    n, d = x.shape
    tm = min(tm, n)
    n_main = COL_PU // tn
    n_tiles = PROJ_WIDTH // tn
    return pl.pallas_call(
        functools.partial(_inproj_kernel, tn=tn, n_main=n_main),
        out_shape=(jax.ShapeDtypeStruct((n, PROJ_WIDTH), BF16), jax.ShapeDtypeStruct((n, d), BF16)),
        grid=(n // tm, n_tiles),
        in_specs=[
            pl.BlockSpec((tm, d), lambda i, j: (i, 0)),
            pl.BlockSpec((1, d), lambda i, j: (0, 0)),
            pl.BlockSpec((None, tn, d), lambda i, j: (l, jnp.minimum(j, n_main - 1), 0)),
            pl.BlockSpec((tn, d), lambda i, j: (jnp.maximum(j - n_main, 0), 0)),
            pl.BlockSpec((tm, MOBA_HEAD_DIM), lambda i, j: (i, 0)),
            pl.BlockSpec((tm, MOBA_HEAD_DIM), lambda i, j: (i, 0)),
        ],
        out_specs=(
            pl.BlockSpec((tm, tn), lambda i, j: (i, j)),
            pl.BlockSpec((tm, d), lambda i, j: (i, 0)),
        ),
        compiler_params=_params("parallel", "arbitrary"),
        name="inproj",
    )(x, g, w_in, w_pool_in, cos_f, sin_f)


def _moba_kernel(q_ref, k_ref, v_ref, o_ref, kmean_ref, *, n_blocks):
    qi = pl.program_id(1)
    blk = MOBA_BLOCK
    hd = MOBA_HEAD_DIM
    q_rows = MOBA_Q_BLOCKS * blk
    tile = MOBA_KV_BLOCKS * blk
    blk_shift = blk.bit_length() - 1
    head_cols = [slice(hh * hd, (hh + 1) * hd) for hh in range(MOBA_HEADS_PER_STEP)]
    first_block = qi * MOBA_Q_BLOCKS

    @pl.when(qi == 0)
    def _():
        kmean_ref[...] = jnp.zeros_like(kmean_ref)

        def fill(b, carry):
            kb = k_ref[pl.ds(pl.multiple_of(b * blk, blk), blk), :].astype(F32)
            kmean_ref[pl.ds(b, 1), :] = jnp.mean(kb, axis=0, keepdims=True)
            return carry

        lax.fori_loop(0, n_blocks, fill, 0)

    lane = lax.broadcasted_iota(jnp.int32, (q_rows, LANES), 1)
    lane_f = lane.astype(F32)
    own_block = first_block + (lax.broadcasted_iota(jnp.int32, (q_rows, LANES), 0) >> blk_shift)
    row = lax.broadcasted_iota(jnp.int32, (blk, blk), 0)
    col = lax.broadcasted_iota(jnp.int32, (blk, blk), 1)
    ones_own = jnp.ones((blk, hd), BF16)
    ones_tile = jnp.ones((tile, hd), BF16)

    q_augs, carry = [], []
    for cols in head_cols:
        q = q_ref[:, cols]
        kmean_hi, kmean_mid = _split_bf16(kmean_ref[:, cols], 2)
        gate = _dot_nt(q, kmean_hi) + _dot_nt(q, kmean_mid)
        gate = jnp.where(lane < own_block, gate, NEG_INF)
        sel = jnp.zeros(gate.shape, jnp.bool_)
        for _ in range(MOBA_TOPK):
            top = jnp.max(gate, axis=1, keepdims=True)
            idx = jnp.min(jnp.where(gate == top, lane_f, float(LANES)), axis=1, keepdims=True)
            hit = lane_f == idx
            sel = sel | (hit & (top > 0.5 * NEG_INF))
            gate = jnp.where(hit, NEG_INF, gate)
        q_augs.append(jnp.concatenate([q, jnp.where(sel, 0.0, NEG_INF).astype(BF16)], axis=1))

        ms, accs = [], []
        for qb in range(MOBA_Q_BLOCKS):
            start = pl.multiple_of((first_block + qb) * blk, blk)
            s = _dot_nt(q[qb * blk:(qb + 1) * blk, :], k_ref[pl.ds(start, blk), cols])
            s = jnp.where(col <= row, s, NEG_INF)
            m = jnp.max(s, axis=1, keepdims=True)
            p = jnp.exp2(s - m).astype(BF16)
            ms.append(m)
            accs.append(_dot(p, jnp.concatenate([v_ref[pl.ds(start, blk), cols], ones_own], axis=1)))
        carry += [jnp.concatenate(ms, axis=0), jnp.concatenate(accs, axis=0)]

    tile_row = lax.broadcasted_iota(jnp.int32, (tile, hd), 0)
    tile_lane = lax.broadcasted_iota(jnp.int32, (tile, hd), 1)

    def body(t, carry):
        off = pl.multiple_of(t * tile, tile)
        block_hot = (tile_lane == t * MOBA_KV_BLOCKS + (tile_row >> blk_shift)).astype(BF16)
        out = []
        for hh, cols in enumerate(head_cols):
            m, acc = carry[2 * hh], carry[2 * hh + 1]
            k_aug = jnp.concatenate([k_ref[pl.ds(off, tile), cols], block_hot], axis=1)
            s = _dot_nt(q_augs[hh], k_aug)
            m_new = jnp.maximum(m, jnp.max(s, axis=1, keepdims=True))
            alpha = jnp.exp2(m - m_new)
            p = jnp.exp2(s - m_new).astype(BF16)
            v_aug = jnp.concatenate([v_ref[pl.ds(off, tile), cols], ones_tile], axis=1)
            out += [m_new, alpha * acc + _dot(p, v_aug)]
        return tuple(out)

    n_tiles = (first_block + MOBA_Q_BLOCKS - 1 + MOBA_KV_BLOCKS - 1) // MOBA_KV_BLOCKS
    carry = lax.fori_loop(0, n_tiles, body, tuple(carry))
    for hh, cols in enumerate(head_cols):
        acc = carry[2 * hh + 1]
        o_ref[:, cols] = (acc[:, :hd] / acc[:, hd:]).astype(o_ref.dtype)


def _moba(proj):
    n = proj.shape[0]
    n_blocks = n // MOBA_BLOCK
    assert n_blocks % MOBA_KV_BLOCKS == 0 and n_blocks % MOBA_Q_BLOCKS == 0 and n_blocks <= LANES
    width = MOBA_HEADS_PER_STEP * MOBA_HEAD_DIM
    q_rows = MOBA_Q_BLOCKS * MOBA_BLOCK
    return pl.pallas_call(
        functools.partial(_moba_kernel, n_blocks=n_blocks),
        out_shape=jax.ShapeDtypeStruct((n, MOBA_WIDTH), BF16),
        grid=(MOBA_HEADS // MOBA_HEADS_PER_STEP, n_blocks // MOBA_Q_BLOCKS),
        in_specs=[
            pl.BlockSpec((q_rows, width), lambda h, i: (i, COL_MQ // width + h)),
            pl.BlockSpec((n, width), lambda h, i: (0, COL_MK // width + h), pipeline_mode=pl.Buffered(1)),
            pl.BlockSpec((n, width), lambda h, i: (0, COL_MV // width + h), pipeline_mode=pl.Buffered(1)),
        ],
        out_specs=pl.BlockSpec((q_rows, width), lambda h, i: (i, h)),
        scratch_shapes=[pltpu.VMEM((LANES, width), F32)],
        compiler_params=_params("parallel", "arbitrary"),
        name="moba",
    )(proj, proj, proj)


def _gla_kernel(q_ref, k_ref, v_ref, r_ref, h_ref, wlow_ref, wdec_ref, bdec_ref, gn_ref, o_ref, state_ref, *, tb):
    i = pl.program_id(0)
    c = GLA_CHUNK
    n_chunks = tb // c
    chunk_shift = c.bit_length() - 1

    @pl.when(i == 0)
    def _():
        state_ref[...] = jnp.zeros_like(state_ref)

    g_low = _dot_nt(h_ref[...], wlow_ref[...]).astype(BF16)
    z = _dot(g_low, wdec_ref[...]) + bdec_ref[...]
    log_a = jax.nn.log_sigmoid(z) / GLA_TAU

    row = lax.broadcasted_iota(jnp.int32, (tb, tb), 0)
    col = lax.broadcasted_iota(jnp.int32, (tb, tb), 1)
    causal = ((row >> chunk_shift) == (col >> chunk_shift)) & (col <= row)
    tri = (lax.broadcasted_iota(jnp.int32, (c, c), 1) <= lax.broadcasted_iota(jnp.int32, (c, c), 0)).astype(BF16)
    parts = _split_bf16(log_a, 3)
    b_chunks, last_chunks = [], []
    for ci in range(n_chunks):
        rows = slice(ci * c, (ci + 1) * c)
        b_c = sum(_dot(tri, part[rows, :]) for part in parts)
        b_chunks.append(b_c)
        last_chunks.append(jnp.broadcast_to(b_c[c - 1:c, :], b_c.shape))
    b = jnp.concatenate(b_chunks, axis=0)
    b_last = jnp.concatenate(last_chunks, axis=0)

    q_dec = (q_ref[...].astype(F32) * (GLA_DK ** -0.5) * jnp.exp(b)).astype(BF16)
    kf = k_ref[...].astype(F32)
    k_inv = (kf * jnp.exp(-b)).astype(BF16)
    k_end = (kf * jnp.exp(b_last - b)).astype(BF16)
    decay = jnp.exp(b_last)

    for h in range(GLA_HEADS):
        kc = slice(h * GLA_DK, (h + 1) * GLA_DK)
        vc = slice(h * GLA_DV, (h + 1) * GLA_DV)
        v = v_ref[:, vc]
        attn = jnp.where(causal, _dot_nt(q_dec[:, kc], k_inv[:, kc]), 0.0)
        o = _dot(attn.astype(BF16), v)
        state = state_ref[h]
        inter = []
        for ci in range(n_chunks):
            rows = slice(ci * c, (ci + 1) * c)
            inter.append(_dot_nt(q_dec[rows, kc], state.astype(BF16)))
            update = _dot_tn(v[rows, :], k_end[rows, kc])
            state = decay[ci * c:ci * c + 1, kc] * state + update
        state_ref[h] = state
        o = o + jnp.concatenate(inter, axis=0)
        o = _rms(o, gn_ref[...])
        o_ref[:, vc] = (o * jax.nn.silu(r_ref[:, vc].astype(F32))).astype(o_ref.dtype)


def _gla(proj, h, w_low, w_dec, b_dec, gla_norm, *, tb=512):
    n, d = h.shape
    tb = min(tb, n)
    return pl.pallas_call(
        functools.partial(_gla_kernel, tb=tb),
        out_shape=jax.ShapeDtypeStruct((n, GLA_V_WIDTH), BF16),
        grid=(n // tb,),
        in_specs=[
            pl.BlockSpec((tb, GLA_K_WIDTH), lambda i: (i, COL_GQ // GLA_K_WIDTH)),
            pl.BlockSpec((tb, GLA_K_WIDTH), lambda i: (i, COL_GK // GLA_K_WIDTH)),
            pl.BlockSpec((tb, GLA_V_WIDTH), lambda i: (i, COL_GV // GLA_V_WIDTH)),
            pl.BlockSpec((tb, GLA_V_WIDTH), lambda i: (i, COL_GR // GLA_V_WIDTH)),
            pl.BlockSpec((tb, d), lambda i: (i, 0)),
            pl.BlockSpec((LANES, d), lambda i: (0, 0)),
            pl.BlockSpec((LANES, GLA_K_WIDTH), lambda i: (0, 0)),
            pl.BlockSpec((1, GLA_K_WIDTH), lambda i: (0, 0)),
            pl.BlockSpec((1, GLA_DV), lambda i: (0, 0)),
        ],
        out_specs=pl.BlockSpec((tb, GLA_V_WIDTH), lambda i: (i, 0)),
        scratch_shapes=[pltpu.VMEM((GLA_HEADS, GLA_DV, GLA_DK), F32)],
        compiler_params=_params("arbitrary"),
        name="gla",
    )(proj, proj, proj, proj, h, w_low, w_dec, b_dec, gla_norm)


def _pool_kernel(u_ref, halo_ref, w_ref, sc_ref, o_ref, ext_ref, lvl_ref, *, tb):
    i = pl.program_id(0)
    pad = POOL_HALO
    body = POOL_HALO + tb
    first = pad + POOL_HALO
    ext_ref[pl.ds(0, pad), :] = jnp.zeros((pad, POOL_WIDTH), F32)
    lvl_ref[:, pl.ds(0, pad), :] = jnp.zeros((2, pad, POOL_GROUP_DIM), F32)
    ext_ref[pl.ds(pad, POOL_HALO), :] = jnp.where(i == 0, 0.0, halo_ref[...].astype(F32))
    ext_ref[pl.ds(first, tb), :] = u_ref[...].astype(F32)
    t = i * tb + lax.broadcasted_iota(jnp.int32, (tb, 1), 0)
    for g, win in enumerate(POOL_WINDOWS):
        cols = slice(g * POOL_GROUP_DIM, (g + 1) * POOL_GROUP_DIM)

        def read(level, start, rows, cols=cols):
            if level is None:
                return ext_ref[pl.ds(start, rows), cols]
            return lvl_ref[level, pl.ds(start, rows), :]

        u = read(None, first, tb)
        level, span, stored = None, 1, 0
        while 2 * span < win:
            doubled = read(level, pad, body) + read(level, pad - span, body)
            lvl_ref[stored % 2, pl.ds(pad, body), :] = doubled
            level, span, stored = stored % 2, 2 * span, stored + 1
        window_sum = read(level, first, tb) + read(level, first - span, tb)
        count = jnp.minimum(t + 1, win).astype(F32)
        mixed = window_sum / count - u
        y = _dot(mixed.astype(BF16), w_ref[g])
        o_ref[:, cols] = (y * sc_ref[:, cols]).astype(o_ref.dtype)


def _pool(proj, w_pool, pool_scale, *, tb=512):
    n = proj.shape[0]
    tb = min(tb, n)
    halo_per_block = tb // POOL_HALO
    return pl.pallas_call(
        functools.partial(_pool_kernel, tb=tb),
        out_shape=jax.ShapeDtypeStruct((n, POOL_WIDTH), BF16),
        grid=(n // tb,),
        in_specs=[
            pl.BlockSpec((tb, POOL_WIDTH), lambda i: (i, COL_PU // POOL_WIDTH)),
            pl.BlockSpec((POOL_HALO, POOL_WIDTH),
                         lambda i: (jnp.maximum(i * halo_per_block - 1, 0), COL_PU // POOL_WIDTH)),
            pl.BlockSpec(w_pool.shape, lambda i: (0, 0, 0)),
            pl.BlockSpec((1, POOL_WIDTH), lambda i: (0, 0)),
        ],
        out_specs=pl.BlockSpec((tb, POOL_WIDTH), lambda i: (i, 0)),
        scratch_shapes=[pltpu.VMEM((2 * POOL_HALO + tb, POOL_WIDTH), F32),
                        pltpu.VMEM((2, 2 * POOL_HALO + tb, POOL_GROUP_DIM), F32)],
        compiler_params=_params("parallel"),
        name="pool",
    )(proj, proj, w_pool, pool_scale)


def _merge_kernel(h_ref, a_ref, b_ref, c_ref, wg0, wg1, wg2, bg0, bg1, bg2, wb0, wb1, wb2, o_ref):
    h = h_ref[...]
    total = None
    for br_ref, wg, bg, wb in ((a_ref, wg0, bg0, wb0), (b_ref, wg1, bg1, wb1), (c_ref, wg2, bg2, wb2)):
        gate = jax.nn.sigmoid(_dot(h, wg[...].astype(BF16)) + bg[...])
        term = gate * _dot(br_ref[...], wb[...].astype(BF16))
        total = term if total is None else total + term
    o_ref[...] = total.astype(o_ref.dtype)


def _merge(h, o_moba, o_gla, o_pool, w_gate, b_gate, w_branch, l, *, tm=1024, tn=256):
    n, d = h.shape
    tm = min(tm, n)
    bw = o_moba.shape[1]
    tiles = d // tn
    row_spec = lambda width: pl.BlockSpec((tm, width), lambda i, j: (i, 0))
    gate_specs = [pl.BlockSpec((None, d, tn), lambda i, j, k=k: (l, 0, k * tiles + j)) for k in range(3)]
    bias_specs = [pl.BlockSpec((1, tn), lambda i, j, k=k: (0, k * tiles + j)) for k in range(3)]
    branch_specs = [pl.BlockSpec((None, None, bw, tn), lambda i, j, k=k: (l, k, 0, j)) for k in range(3)]
    return pl.pallas_call(
        _merge_kernel,
        out_shape=jax.ShapeDtypeStruct((n, d), BF16),
        grid=(n // tm, tiles),
        in_specs=[row_spec(d), row_spec(bw), row_spec(bw), row_spec(bw)] + gate_specs + bias_specs + branch_specs,
        out_specs=pl.BlockSpec((tm, tn), lambda i, j: (i, j)),
        compiler_params=_params("parallel", "arbitrary"),
        name="merge",
    )(h, o_moba, o_gla, o_pool, w_gate, w_gate, w_gate, b_gate, b_gate, b_gate, w_branch, w_branch, w_branch)


def _matmul_residual_kernel(a_ref, w_ref, x_ref, o_ref):
    o_ref[...] = x_ref[...] + _dot(a_ref[...], w_ref[...].astype(BF16))


def _matmul_residual(a, w, l, x, *, tm=2048, tn=512):
    n, k = a.shape
    d = w.shape[-1]
    tm = min(tm, n)
    return pl.pallas_call(
        _matmul_residual_kernel,
        out_shape=jax.ShapeDtypeStruct((n, d), F32),
        grid=(n // tm, d // tn),
        in_specs=[
            pl.BlockSpec((tm, k), lambda i, j: (i, 0)),
            pl.BlockSpec((None, k, tn), lambda i, j: (l, 0, j)),
            pl.BlockSpec((tm, tn), lambda i, j: (i, j)),
        ],
        out_specs=pl.BlockSpec((tm, tn), lambda i, j: (i, j)),
        compiler_params=_params("parallel", "arbitrary"),
        name="mix_out",
    )(a, w, x)


def _norm_matmul_kernel(x_ref, g_ref, w_ref, o_ref):
    o_ref[...] = _dot(_rms(x_ref[...], g_ref[...]).astype(BF16), w_ref[...].astype(BF16)).astype(o_ref.dtype)


def _mem_kv(mem, g, w_xkv, l):
    n, d = mem.shape
    width = w_xkv.shape[-1]
    return pl.pallas_call(
        _norm_matmul_kernel,
        out_shape=jax.ShapeDtypeStruct((n, width), BF16),
        grid=(1,),
        in_specs=[
            pl.BlockSpec((n, d), lambda i: (0, 0)),
            pl.BlockSpec((1, d), lambda i: (0, 0)),
            pl.BlockSpec((None, d, width), lambda i: (l, 0, 0)),
        ],
        out_specs=pl.BlockSpec((n, width), lambda i: (0, 0)),
        compiler_params=_params("arbitrary"),
        name="mem_kv",
    )(mem, g, w_xkv)


def _xattn_kernel(x_ref, g_ref, wq_ref, kv_ref, wo_ref, gm_ref, wr_ref, br_ref,
                  xo_ref, h2_ref, route_ref, counts_ref, wq_b, wo_b, counts_acc, *, tm):
    i = pl.program_id(0)

    @pl.when(i == 0)
    def _():
        wq_b[...] = wq_ref[...].astype(BF16)
        wo_b[...] = wo_ref[...].astype(BF16)
        counts_acc[...] = jnp.zeros_like(counts_acc)

    x = x_ref[...]
    h = _rms(x, g_ref[...]).astype(BF16)
    q = (_dot(h, wq_b[...]) * (XATTN_HEAD_DIM ** -0.5)).astype(BF16)
    heads = []
    for hd in range(XATTN_HEADS):
        kc = slice(hd * XATTN_HEAD_DIM, (hd + 1) * XATTN_HEAD_DIM)
        vc = slice(XATTN_WIDTH + hd * XATTN_HEAD_DIM, XATTN_WIDTH + (hd + 1) * XATTN_HEAD_DIM)
        s = _dot_nt(q[:, kc], kv_ref[:, kc])
        p = jnp.exp(s - jnp.max(s, axis=1, keepdims=True))
        p = p / jnp.sum(p, axis=1, keepdims=True)
        heads.append(_dot(p.astype(BF16), kv_ref[:, vc]).astype(BF16))
    o = jnp.concatenate(heads, axis=1)
    xn = x + _dot(o, wo_b[...])
    xo_ref[...] = xn
    h2 = _rms(xn, gm_ref[...])
    h2_ref[...] = h2

    h_hi, h_mid = _split_bf16(h2, 2)
    hi_terms = _dot(h_hi, wr_ref[...])
    lg = hi_terms[:, :LANES] + (hi_terms[:, LANES:] + _dot(h_mid, wr_ref[:, :LANES])) + br_ref[...]

    lane = lax.broadcasted_iota(jnp.int32, lg.shape, 1)
    lane_f = lane.astype(F32)
    big = float(LANES)

    def top1(v):
        top = jnp.max(v, axis=1, keepdims=True)
        return top, jnp.min(jnp.where(v == top, lane_f, big), axis=1, keepdims=True)

    g_logit = jnp.where(lane < MOE_GROUPS, lg, NEG_INF)
    g_top, g_idx = top1(g_logit)
    g_w = 1.0 / jnp.sum(jnp.exp(g_logit - g_top), axis=1, keepdims=True)
    first = MOE_GROUPS + MOE_EXPERTS_PER_GROUP * g_idx
    e_logit = jnp.where((lane_f >= first) & (lane_f < first + MOE_EXPERTS_PER_GROUP), lg, NEG_INF)
    e_top1, lane1 = top1(e_logit)
    e_top2, lane2 = top1(jnp.where(lane_f == lane1, NEG_INF, e_logit))
    z = jnp.sum(jnp.exp(e_logit - e_top1), axis=1, keepdims=True)
    p1 = 1.0 / z
    p2 = jnp.exp(e_top2 - e_top1) / z
    w1 = g_w * p1 / (p1 + p2)
    w2 = g_w * p2 / (p1 + p2)

    two_hot = (lane_f == lane1) | (lane_f == lane2)
    row = lax.broadcasted_iota(jnp.int32, (tm, tm), 0)
    col = lax.broadcasted_iota(jnp.int32, (tm, tm), 1)
    before = _dot((col < row).astype(BF16), two_hot.astype(BF16)) + counts_acc[...]
    rank1 = jnp.sum(jnp.where(lane_f == lane1, before, 0.0), axis=1, keepdims=True)
    rank2 = jnp.sum(jnp.where(lane_f == lane2, before, 0.0), axis=1, keepdims=True)
    counts_acc[...] += jnp.sum(two_hot.astype(F32), axis=0, keepdims=True)
    counts_ref[...] = counts_acc[...]

    record = jnp.zeros(lg.shape, F32)
    for slot, val in ((ROUTE_E1, lane1 - MOE_GROUPS), (ROUTE_E2, lane2 - MOE_GROUPS), (ROUTE_W1, w1),
                      (ROUTE_W2, w2), (ROUTE_RANK1, rank1), (ROUTE_RANK2, rank2)):
        record = jnp.where(lane == slot, val, record)
    route_ref[...] = record


def _xattn(x, g, wq, kv, wo, l, g_moe, w_route_split, b_route, *, tm=512):
    n, d = x.shape
    tm = min(tm, n)
    full = lambda a: pl.BlockSpec(a.shape, lambda i: (0,) * a.ndim)
    layer = lambda a: pl.BlockSpec((None,) + a.shape[1:], lambda i: (l,) + (0,) * (a.ndim - 1))
    rows = lambda width: pl.BlockSpec((tm, width), lambda i: (i, 0))
    return pl.pallas_call(
        functools.partial(_xattn_kernel, tm=tm),
        out_shape=(jax.ShapeDtypeStruct((n, d), F32), jax.ShapeDtypeStruct((n, d), F32),
                   jax.ShapeDtypeStruct((n, LANES), F32), jax.ShapeDtypeStruct((1, LANES), F32)),
        grid=(n // tm,),
        in_specs=[rows(d), full(g), layer(wq), full(kv), layer(wo), full(g_moe), full(w_route_split), full(b_route)],
        out_specs=(rows(d), rows(d), rows(LANES), pl.BlockSpec((1, LANES), lambda i: (0, 0))),
        scratch_shapes=[pltpu.VMEM(wq.shape[1:], BF16), pltpu.VMEM(wo.shape[1:], BF16),
                        pltpu.VMEM((1, LANES), F32)],
        compiler_params=_params("arbitrary"),
        name="xattn_route",
    )(x, g, wq, kv, wo, g_moe, w_route_split, b_route)


def _moe_kernel(start_ref, count_ref, fill_ref, x_hbm, w1_hbm, w3_hbm, w2_hbm, y_hbm,
                w1f, w3f, w2f, w1b, w3b, w2b, xbuf, ybuf, wsem, xsem, ysem, *, layer):
    e = pl.program_id(0)
    n_experts = pl.num_programs(0)
    rb = MOE_ROW_BLOCK
    par = e % 2
    first = start_ref[e]
    n_blocks = count_ref[e]

    def rows(row):
        return pl.ds(pl.multiple_of(row, MOE_ROW_ALIGN), rb)

    def w_copies(expert, slot):
        return (pltpu.make_async_copy(w1_hbm.at[layer, expert], w1f.at[slot], wsem.at[0, slot]),
                pltpu.make_async_copy(w3_hbm.at[layer, expert], w3f.at[slot], wsem.at[1, slot]),
                pltpu.make_async_copy(w2_hbm.at[layer, expert], w2f.at[slot], wsem.at[2, slot]))

    def x_copy(row, slot, xs):
        return pltpu.make_async_copy(x_hbm.at[rows(row), :], xbuf.at[slot, xs], xsem.at[slot, xs])

    def y_copy(row, ys):
        return pltpu.make_async_copy(ybuf.at[ys], y_hbm.at[rows(row), :], ysem.at[ys])

    def prefetch(expert, slot):
        for xs in range(MOE_X_SLOTS):
            @pl.when(xs < count_ref[expert])
            def _():
                x_copy(start_ref[expert] + xs * rb, slot, xs).start()
        for copy in w_copies(expert, slot):
            copy.start()

    @pl.when(e == 0)
    def _():
        prefetch(0, 0)
        ybuf[0] = jnp.zeros(ybuf.shape[1:], ybuf.dtype)

        def start(block, carry):
            y_copy(block * rb, 0).start()
            return carry

        def wait(block, carry):
            y_copy(block * rb, 0).wait()
            return carry

        total = y_hbm.shape[0] // rb
        lax.fori_loop(fill_ref[0], total, start, 0)
        lax.fori_loop(fill_ref[0], total, wait, 0)

    @pl.when(e + 1 < n_experts)
    def _():
        prefetch(e + 1, 1 - par)

    for copy in w_copies(e, par):
        copy.wait()
    w1b[...] = w1f[par].astype(BF16)
    w3b[...] = w3f[par].astype(BF16)
    w2b[...] = w2f[par].astype(BF16)

    def body(c, carry):
        xs = c % MOE_X_SLOTS
        ys = c % MOE_Y_SLOTS
        row = first + c * rb

        @pl.when(c < MOE_X_SLOTS)
        def _():
            x_copy(row, par, xs).wait()

        @pl.when(c >= MOE_X_SLOTS)
        def _():
            copy = x_copy(row, par, xs)
            copy.start()
            copy.wait()

        @pl.when(c >= MOE_Y_SLOTS)
        def _():
            y_copy(row - MOE_Y_SLOTS * rb, ys).wait()

        x = xbuf[par, xs].astype(BF16)
        hidden = jax.nn.silu(_dot(x, w1b[...])) * _dot(x, w3b[...])
        ybuf[ys] = _dot(hidden.astype(BF16), w2b[...])
        y_copy(row, ys).start()
        return carry

    lax.fori_loop(0, n_blocks, body, 0)

    for back in range(1, MOE_Y_SLOTS + 1):
        @pl.when(n_blocks >= back)
        def _():
            y_copy(first + (n_blocks - back) * rb, (n_blocks - back) % MOE_Y_SLOTS).wait()


def _moe_experts(row_start, block_count, fill_start, x_rows, w1, w3, w2, l):
    p, d = x_rows.shape
    rb = MOE_ROW_BLOCK
    n_experts, dff = w1.shape[1], w1.shape[-1]
    hbm = pl.BlockSpec(memory_space=pl.ANY)
    grid_spec = pltpu.PrefetchScalarGridSpec(
        num_scalar_prefetch=3,
        grid=(n_experts,),
        in_specs=[hbm, hbm, hbm, hbm],
        out_specs=hbm,
        scratch_shapes=[
            pltpu.VMEM((2, d, dff), F32), pltpu.VMEM((2, d, dff), F32), pltpu.VMEM((2, dff, d), F32),
            pltpu.VMEM((d, dff), BF16), pltpu.VMEM((d, dff), BF16), pltpu.VMEM((dff, d), BF16),
            pltpu.VMEM((2, MOE_X_SLOTS, rb, d), F32), pltpu.VMEM((MOE_Y_SLOTS, rb, d), F32),
            pltpu.SemaphoreType.DMA((3, 2)), pltpu.SemaphoreType.DMA((2, MOE_X_SLOTS)),
            pltpu.SemaphoreType.DMA((MOE_Y_SLOTS,)),
        ],
    )
    return pl.pallas_call(
        functools.partial(_moe_kernel, layer=l),
        out_shape=jax.ShapeDtypeStruct((p, d), F32),
        grid_spec=grid_spec,
        compiler_params=_params("arbitrary"),
        name="moe_experts",
    )(row_start, block_count, fill_start, x_rows, w1, w3, w2)


DISPATCH_DEST1, DISPATCH_DEST2 = 0, 1
META_ROW_START, META_ROW_COUNT = 0, 1


def _dispatch_kernel(route_ref, counts_ref, dest_ref, meta_ref):
    align = float(MOE_ROW_ALIGN)
    counts = jnp.broadcast_to(counts_ref[...], (8, LANES))
    padded = jnp.floor((counts + (align - 1.0)) * (1.0 / align)) * align
    src = lax.broadcasted_iota(jnp.int32, (LANES, LANES), 0)
    dst = lax.broadcasted_iota(jnp.int32, (LANES, LANES), 1)
    earlier = (src < dst).astype(BF16)
    seg_start = sum(_dot(part, earlier) for part in _split_bf16(padded, 3))[0:1]

    route = route_ref[...]
    lane = lax.broadcasted_iota(jnp.int32, route.shape, 1)
    lane_f = lane.astype(F32)

    def field(slot):
        return jnp.sum(jnp.where(lane == slot, route, 0.0), axis=1, keepdims=True)

    def dest(e_slot, rank_slot):
        hit = lane_f == field(e_slot) + float(MOE_GROUPS)
        return jnp.sum(jnp.where(hit, seg_start, 0.0), axis=1, keepdims=True) + field(rank_slot)

    dest1 = dest(ROUTE_E1, ROUTE_RANK1)
    dest2 = dest(ROUTE_E2, ROUTE_RANK2)
    dest_ref[...] = jnp.where(lane == DISPATCH_DEST1, dest1, jnp.where(lane == DISPATCH_DEST2, dest2, 0.0))

    sub = lax.broadcasted_iota(jnp.int32, (8, LANES), 0)
    meta_ref[...] = jnp.where(sub == META_ROW_START, jnp.broadcast_to(seg_start, (8, LANES)),
                              jnp.where(sub == META_ROW_COUNT, padded, 0.0))


def _dispatch(route, counts, *, tm=1024):
    n = route.shape[0]
    tm = min(tm, n)
    return pl.pallas_call(
        _dispatch_kernel,
        out_shape=(jax.ShapeDtypeStruct((n, LANES), F32), jax.ShapeDtypeStruct((8, LANES), F32)),
        grid=(n // tm,),
        in_specs=[pl.BlockSpec((tm, LANES), lambda i: (i, 0)), pl.BlockSpec((1, LANES), lambda i: (0, 0))],
        out_specs=(pl.BlockSpec((tm, LANES), lambda i: (i, 0)), pl.BlockSpec((8, LANES), lambda i: (0, 0))),
        compiler_params=_params("arbitrary"),
        name="moe_dispatch",
    )(route, counts)


def _dispatch_layout(route, counts):
    n = route.shape[0]
    rb = MOE_ROW_BLOCK
    dest, meta = _dispatch(route, counts)
    dest_pair = jnp.concatenate([dest[:, DISPATCH_DEST1], dest[:, DISPATCH_DEST2]]).astype(jnp.int32)
    experts = slice(MOE_GROUPS, MOE_GROUPS + MOE_EXPERTS)
    row_start = meta[META_ROW_START, experts].astype(jnp.int32)
    row_count = meta[META_ROW_COUNT, experts].astype(jnp.int32)
    block_count = (row_count + rb - 1) // rb
    fill_start = (row_start[-1:] + row_count[-1:]) // rb
    n_rows = (n * MOE_TOPK + MOE_EXPERTS * (MOE_ROW_ALIGN - 1) + rb - 1) // rb * rb + rb
    tok = jnp.arange(n, dtype=jnp.int32)
    row_tok = (jnp.arange(n_rows, dtype=jnp.int32) % n).at[dest_pair].set(
        jnp.concatenate([tok, tok]), mode="promise_in_bounds", unique_indices=True)
    return row_tok, row_start, block_count, fill_start, dest_pair


def _combine_kernel(x_ref, y1_ref, y2_ref, route_ref, g_ref, o_ref, *, final_norm):
    route = route_ref[...]
    w1 = route[:, ROUTE_W1:ROUTE_W1 + 1]
    w2 = route[:, ROUTE_W2:ROUTE_W2 + 1]
    out = x_ref[...] + (w1 * y1_ref[...] + w2 * y2_ref[...])
    o_ref[...] = _rms(out, g_ref[...]) if final_norm else out


def _combine(x, y_pair, route, g, *, final_norm, tm=512):
    n, d = x.shape
    tm = min(tm, n)
    steps = n // tm
    rows = lambda width: pl.BlockSpec((tm, width), lambda i: (i, 0))
    return pl.pallas_call(
        functools.partial(_combine_kernel, final_norm=final_norm),
        out_shape=jax.ShapeDtypeStruct((n, d), F32),
        grid=(steps,),
        in_specs=[rows(d), rows(d), pl.BlockSpec((tm, d), lambda i: (steps + i, 0)), rows(LANES),
                  pl.BlockSpec((1, d), lambda i: (0, 0))],
        out_specs=rows(d),
        compiler_params=_params("parallel"),
        name="moe_combine",
    )(x, y_pair, y_pair, route, g)


def _rope_tables(positions):
    inv = 1.0 / (ROPE_THETA ** (jnp.arange(0, MOBA_HEAD_DIM, 2, dtype=F32) / MOBA_HEAD_DIM))
    ang = positions.astype(F32)[:, None] * inv
    cos, sin = jnp.cos(ang), jnp.sin(ang)
    return jnp.concatenate([cos, cos], axis=-1), jnp.concatenate([-sin, sin], axis=-1)


def _layer(x, mem, cos_f, sin_f, p, l, norm_final):
    w_in = p["w_in"]
    w_pool_in = w_in[l, W_IN_POOL:, :]
    w_low = jnp.pad(w_in[l, W_IN_LOW:W_IN_POOL, :], ((0, LANES - GLA_RANK), (0, 0)))
    proj, h = _inproj(x, _layer_vec(p["norm_mix"], l), w_in, w_pool_in, l, cos_f, sin_f)
    o_moba = _moba(proj)
    w_dec = jnp.pad(p["w_gla_decay"][l], ((0, LANES - GLA_RANK), (0, 0))).astype(BF16)
    o_gla = _gla(proj, h, w_low, w_dec, _layer_vec(p["b_gla_decay"], l), _layer_vec(p["gla_norm"], l))
    o_pool = _pool(proj, p["w_pool"][l].astype(BF16), _layer_vec(p["pool_scale"], l))
    merged = _merge(h, o_moba, o_gla, o_pool, p["w_gate"], _layer_vec(p["b_gate"], l), p["w_branch"], l)
    x = _matmul_residual(merged, p["w_mix_out"], l, x)

    kv = _mem_kv(mem, _layer_vec(p["norm_mem"], l), p["w_xkv"], l)
    w_route = jnp.concatenate([p["w_route_group"][l], p["w_route_expert"][l]], axis=1)
    n_route = w_route.shape[1]
    w_route_split = jnp.concatenate(_split_bf16(jnp.pad(w_route, ((0, 0), (0, LANES - n_route))), 2), axis=1)
    b_route = jnp.pad(jnp.concatenate([p["b_route_group"][l], p["b_route_expert"][l]]), (0, LANES - n_route))
    x, h2, route, counts = _xattn(x, _layer_vec(p["norm_xattn"], l), p["w_xq"], kv, p["w_xo"], l,
                                  _layer_vec(p["norm_moe"], l), w_route_split, b_route.reshape(1, -1))

    row_tok, row_start, block_count, fill_start, dest_pair = _dispatch_layout(route, counts)
    x_rows = h2.at[row_tok].get(mode="promise_in_bounds")
    y_rows = _moe_experts(row_start, block_count, fill_start, x_rows,
                          p["w_exp_gate"], p["w_exp_up"], p["w_exp_down"], l)
    y_pair = y_rows.at[dest_pair].get(mode="promise_in_bounds")
    return _combine(x, y_pair, route, norm_final.reshape(1, -1), final_norm=l == p["norm_mix"].shape[0] - 1)


def kernel(x, mem, positions, norm_mix, w_in, w_gla_decay, b_gla_decay, gla_norm, w_pool, pool_scale, w_branch, w_gate, b_gate, w_mix_out, norm_xattn, norm_mem, w_xq, w_xkv, w_xo, norm_moe, w_route_group, b_route_group, w_route_expert, b_route_expert, w_exp_gate, w_exp_up, w_exp_down, norm_final):
    batch, seq, d = x.shape
    assert batch == 1, "kernels are written for a single sequence"
    params = dict(norm_mix=norm_mix, w_in=jnp.swapaxes(w_in, 1, 2).astype(BF16), w_gla_decay=w_gla_decay, b_gla_decay=b_gla_decay,
                  gla_norm=gla_norm, w_pool=w_pool, pool_scale=pool_scale, w_branch=w_branch, w_gate=w_gate,
                  b_gate=b_gate, w_mix_out=w_mix_out, norm_xattn=norm_xattn, norm_mem=norm_mem, w_xq=w_xq,
                  w_xkv=w_xkv, w_xo=w_xo, norm_moe=norm_moe, w_route_group=w_route_group,
                  b_route_group=b_route_group, w_route_expert=w_route_expert, b_route_expert=b_route_expert,
                  w_exp_gate=w_exp_gate, w_exp_up=w_exp_up, w_exp_down=w_exp_down)
    cos_f, sin_f = _rope_tables(positions[0])
    xs = x[0]
    for l in range(norm_mix.shape[0]):
        xs = _layer(xs, mem[0], cos_f, sin_f, params, l, norm_final)
    return xs.reshape(batch, seq, d)
```
